```python
import math
import jax, jax.numpy as jnp
from jax import lax
import numpy as np

D_MODEL = 1024
BATCH = 16
SEQ = 4096
DEPTH = 1

N_MEM = 256
R_HEAD = 64
R_HEADS = D_MODEL // R_HEAD
R_WIDTH = R_HEADS * R_HEAD
R_LORA_W = 64
R_LORA_A = 64
R_LORA_G = 160
LNX_EPS = 64e-5
N_HEADS = 16
N_KV = 4
N_HG = N_HEADS // N_KV
HEAD_DIM = 64
CMP_BLOCK = 32
CMP_STRIDE = 16
CMP_HIDDEN = 256
SEL_BLOCK = 64
SEL_TOPK = 16
WINDOW = 512
Q_BLOCK = 32
FORCE_SCORE = 1e4
REL_BUCKETS = 32
REL_MAX_DIST = 128
CA_HEADS = 4
CA_HEAD_DIM = D_MODEL // CA_HEADS
D_FF = 2816
CONV_W = 3
NORM_EPS = 1e-6
QK_EPS = 1e-6
RWKV_COLS = 3 * R_WIDTH + R_LORA_W + R_LORA_A + R_LORA_G
NSA_KV = N_KV * HEAD_DIM
NSA_COLS = N_HEADS * HEAD_DIM + 6 * NSA_KV + 3 * N_HEADS
GATE_COLS = 2 * D_MODEL
IN_COLS = RWKV_COLS + NSA_COLS + GATE_COLS

kernel_name = "hybrid_rwkv7_nsa_gated_block"


def rms_norm(x, g, eps=NORM_EPS):
    xf = x.astype(jnp.float32)
    y = xf * lax.rsqrt(jnp.mean(xf * xf, axis=-1, keepdims=True) + eps)
    return (y * g.astype(jnp.float32)).astype(x.dtype)


def masked_softmax(logits, mask):
    logits = jnp.where(mask, logits.astype(jnp.float32), -jnp.inf)
    m = jnp.max(logits, axis=-1, keepdims=True)
    m = jnp.where(jnp.isfinite(m), m, 0.0)
    e = jnp.exp(logits - m)
    s = jnp.sum(e, axis=-1, keepdims=True)
    return e / jnp.where(s > 0, s, 1.0)


def t5_bucket(dist):
    n = jnp.maximum(dist, 0)
    exact = REL_BUCKETS // 2
    nf = jnp.maximum(n, 1).astype(jnp.float32)
    large = exact + (jnp.log(nf / exact) / math.log(REL_MAX_DIST / exact) * (REL_BUCKETS - exact)).astype(jnp.int32)
    large = jnp.minimum(large, REL_BUCKETS - 1)
    return jnp.where(n < exact, n, large)


def group_bias(table, bucket):
    tg = table.astype(jnp.float32).reshape(REL_BUCKETS, N_KV, N_HG)
    gi = jnp.arange(N_KV).reshape(1, N_KV, 1, 1)
    return jnp.moveaxis(tg[bucket, gi], -1, -2)


def token_shift(z, mu):
    prev = jnp.pad(z[:, :-1], ((0, 0), (1, 0), (0, 0)))
    return z + (prev - z) * mu


def rwkv7_scan(r, w, k, v, a, b):
    B, S, H, N = r.shape
    xs = tuple(jnp.swapaxes(t.astype(jnp.float32), 0, 1) for t in (r, w, k, v, a, b))

    def step(state, inp):
        r_t, w_t, k_t, v_t, a_t, b_t = inp
        sa = jnp.einsum('bhvk,bhk->bhv', state, a_t)
        state = state * w_t[:, :, None, :] + sa[..., None] * b_t[:, :, None, :] + v_t[..., None] * k_t[:, :, None, :]
        return state, jnp.einsum('bhvk,bhk->bhv', state, r_t)

    s0 = jnp.zeros((B, H, N, N), jnp.float32)
    _, ys = lax.scan(step, s0, xs)
    return jnp.swapaxes(ys, 0, 1)


def rwkv7_mixer(z, mu, w0, w2, a0, a2, g2, k_k, k_a, r_k, lnx_w, lnx_b):
    B, S, _ = z.shape
    z = token_shift(z, mu)
    r, k, v, zw, za, zg = jnp.split(z, np.cumsum([R_WIDTH, R_WIDTH, R_WIDTH, R_LORA_W, R_LORA_A]).tolist(), axis=-1)
    w = -jax.nn.softplus(-(w0 + jnp.tanh(zw) @ w2)) - 0.5
    a = jax.nn.sigmoid(a0 + za @ a2)
    g = jax.nn.sigmoid(zg) @ g2
    hd = lambda t: t.reshape(B, S, R_HEADS, R_HEAD)
    kk = hd(k * k_k).astype(jnp.float32)
    kk = kk / jnp.maximum(jnp.sqrt(jnp.sum(kk * kk, axis=-1, keepdims=True)), 1e-12)
    k = k * (1 + (a - 1) * k_a)
    decay = jnp.exp(-jnp.exp(w.astype(jnp.float32)))
    a_h = hd(a).astype(jnp.float32)
    y = rwkv7_scan(hd(r), hd(decay), hd(k), hd(v), -kk, kk * a_h)
    mean = jnp.mean(y, axis=-1, keepdims=True)
    var = jnp.mean(jnp.square(y - mean), axis=-1, keepdims=True)
    y = ((y - mean) * lax.rsqrt(var + LNX_EPS)).reshape(B, S, R_WIDTH)
    y = y * lnx_w.astype(jnp.float32) + lnx_b.astype(jnp.float32)
    bonus = jnp.sum(hd(r) * hd(k) * r_k, axis=-1, keepdims=True) * hd(v)
    out = (y + bonus.reshape(B, S, R_WIDTH).astype(jnp.float32)) * g.astype(jnp.float32)
    return out.astype(z.dtype)


def nsa_mixer(q, kc, vc, ks, vs, kw, vw, gates, rel_table, q_gain, k_gain, pe_k, pe_v, c1k, c2k, c1v, c2v):
    B, S, _ = q.shape
    q = rms_norm(q.reshape(B, S, N_KV, N_HG, HEAD_DIM), q_gain, QK_EPS)
    kvh = lambda t: t.reshape(B, S, N_KV, HEAD_DIM)
    scale = HEAD_DIM ** -0.5
    n_cmp = (S - CMP_BLOCK) // CMP_STRIDE + 1
    cidx = jnp.arange(n_cmp)[:, None] * CMP_STRIDE + jnp.arange(CMP_BLOCK)[None, :]

    def compress(t, pe, w1, w2):
        blk = kvh(t)[:, cidx] + pe[:, None, :]
        hid = jax.nn.gelu(jnp.einsum('bnlgd,ldc->bngc', blk, w1.reshape(CMP_BLOCK, HEAD_DIM, CMP_HIDDEN)))
        return jnp.einsum('bngc,cd->bngd', hid, w2)

    k_cmp = rms_norm(compress(kc, pe_k, c1k, c2k), k_gain[0], QK_EPS)
    v_cmp = compress(vc, pe_v, c1v, c2v)
    cmp_end = jnp.arange(n_cmp) * CMP_STRIDE + CMP_BLOCK - 1
    n_blk = S // SEL_BLOCK
    n_top = min(SEL_TOPK, n_blk)
    ci = jnp.arange(n_cmp)[:, None] * CMP_STRIDE
    sj = jnp.arange(n_blk)[None, :] * SEL_BLOCK
    cmp_to_sel = ((ci <= sj + SEL_BLOCK - 1) & (ci + CMP_BLOCK - 1 >= sj)).astype(jnp.float32)
    kb = rms_norm(kvh(ks), k_gain[1], QK_EPS).reshape(B, n_blk, SEL_BLOCK, N_KV, HEAD_DIM).transpose(0, 3, 1, 2, 4)
    vb = kvh(vs).reshape(B, n_blk, SEL_BLOCK, N_KV, HEAD_DIM).transpose(0, 3, 1, 2, 4)
    pad = ((0, 0), (WINDOW, 0), (0, 0), (0, 0))
    k_win = jnp.pad(rms_norm(kvh(kw), k_gain[2], QK_EPS), pad)
    v_win = jnp.pad(kvh(vw), pad)
    g_all = jax.nn.sigmoid(gates.astype(jnp.float32)).reshape(B, S, N_KV, N_HG, 3)
    nq = S // Q_BLOCK
    q_blocks = jnp.swapaxes(q.reshape(B, nq, Q_BLOCK, N_KV, N_HG, HEAD_DIM), 0, 1)
    g_blocks = jnp.swapaxes(g_all.reshape(B, nq, Q_BLOCK, N_KV, N_HG, 3), 0, 1)
    starts = jnp.arange(nq, dtype=jnp.int32) * Q_BLOCK
    bi = jnp.arange(B)[:, None, None, None]
    gi = jnp.arange(N_KV)[None, :, None, None]
    blk_ids = jnp.arange(n_blk)

    def block(args):
        qb, gb, q0 = args
        t = q0 + jnp.arange(Q_BLOCK)
        s_c = jnp.einsum('bqghd,bngd->bgqhn', qb, k_cmp).astype(jnp.float32) * scale
        s_c = s_c + group_bias(rel_table, t5_bucket(t[:, None] - cmp_end[None, :])[None, None])
        p_c = masked_softmax(s_c, (cmp_end[None, :] <= t[:, None])[None, None, :, None, :])
        o_c = jnp.einsum('bgqhn,bngd->bqghd', p_c.astype(v_cmp.dtype), v_cmp)
        imp = jnp.einsum('bgqhn,nj->bgqj', p_c, cmp_to_sel)
        cur = t // SEL_BLOCK
        forced = (blk_ids[None, :] == 0) | (blk_ids[None, :] == cur[:, None]) | (blk_ids[None, :] == cur[:, None] - 1)
        valid = blk_ids[None, :] * SEL_BLOCK <= t[:, None]
        imp = jnp.where(valid, jnp.where(forced, FORCE_SCORE, imp), -jnp.inf)
        _, idx = lax.top_k(imp, n_top)
        k_g = kb[bi, gi, idx].reshape(B, N_KV, Q_BLOCK, n_top * SEL_BLOCK, HEAD_DIM)
        v_g = vb[bi, gi, idx].reshape(B, N_KV, Q_BLOCK, n_top * SEL_BLOCK, HEAD_DIM)
        kpos = (idx[..., None] * SEL_BLOCK + jnp.arange(SEL_BLOCK)).reshape(B, N_KV, Q_BLOCK, n_top * SEL_BLOCK)
        dist_s = t[None, None, :, None] - kpos
        s_s = jnp.einsum('bqghd,bgqkd->bgqhk', qb, k_g).astype(jnp.float32) * scale + group_bias(rel_table, t5_bucket(dist_s))
        p_s = masked_softmax(s_s, (dist_s >= 0)[:, :, :, None, :])
        o_s = jnp.einsum('bgqhk,bgqkd->bqghd', p_s.astype(v_g.dtype), v_g)
        kw_b = lax.dynamic_slice_in_dim(k_win, q0, WINDOW + Q_BLOCK, axis=1)
        vw_b = lax.dynamic_slice_in_dim(v_win, q0, WINDOW + Q_BLOCK, axis=1)
        wpos = q0 - WINDOW + jnp.arange(WINDOW + Q_BLOCK)
        dist_w = t[:, None] - wpos[None, :]
        m_w = (wpos[None, :] >= 0) & (dist_w >= 0) & (dist_w < WINDOW)
        s_w = jnp.einsum('bqghd,bkgd->bgqhk', qb, kw_b).astype(jnp.float32) * scale
        s_w = s_w + group_bias(rel_table, t5_bucket(dist_w)[None, None])
        p_w = masked_softmax(s_w, m_w[None, None, :, None, :])
        o_w = jnp.einsum('bgqhk,bkgd->bqghd', p_w.astype(vw_b.dtype), vw_b)
        out = gb[..., 0:1] * o_c + gb[..., 1:2] * o_s + gb[..., 2:3] * o_w
        return out.astype(qb.dtype)

    o = lax.map(block, (q_blocks, g_blocks, starts))
    return jnp.swapaxes(o, 0, 1).reshape(B, S, N_HEADS * HEAD_DIM)


def cross_attn(xn, mn, wq, wkv, q_gain, k_gain, wo):
    B, S, _ = xn.shape
    M = mn.shape[1]
    q = rms_norm((xn @ wq).reshape(B, S, CA_HEADS, CA_HEAD_DIM), q_gain, QK_EPS)
    kv = (mn @ wkv).reshape(B, M, 2, CA_HEADS, CA_HEAD_DIM)
    k = rms_norm(kv[:, :, 0], k_gain, QK_EPS)
    v = kv[:, :, 1]
    s = jnp.einsum('bshd,bmhd->bhsm', q, k).astype(jnp.float32) * (CA_HEAD_DIM ** -0.5)
    p = jax.nn.softmax(s, axis=-1).astype(v.dtype)
    o = jnp.einsum('bhsm,bmhd->bshd', p, v).reshape(B, S, CA_HEADS * CA_HEAD_DIM)
    return o @ wo


def conv_ffn(xn, w_up, conv_w, conv_b, w_down):
    a, b = jnp.split(xn @ w_up, 2, axis=-1)
    a = lax.conv_general_dilated(a, conv_w[:, None, :].astype(a.dtype), (1,), [(CONV_W - 1, 0)],
                                 dimension_numbers=('NWC', 'WIO', 'NWC'), feature_group_count=D_FF) + conv_b
    return (jax.nn.silu(a) * b) @ w_down


def setup_inputs(seed: int = 0) -> dict:
    key = jax.random.key(seed)
    ks = iter(jax.random.split(key, 64))
    L, D = DEPTH, D_MODEL
    nrm = lambda shape, scale: jax.random.normal(next(ks), shape, jnp.float32) * scale
    lin = lambda fi, fo: nrm((L, fi, fo), fi ** -0.5)
    gain = lambda *s: 1.0 + nrm((L,) + s, 0.02)
    return {
        "x": nrm((BATCH, SEQ, D), 1.0),
        "mem": nrm((BATCH, N_MEM, D), 1.0),
        "rel_bias": nrm((REL_BUCKETS, N_HEADS), 0.1),
        "norm_mix": gain(D),
        "w_in": lin(D, IN_COLS),
        "rwkv_mu": jax.random.uniform(next(ks), (L, RWKV_COLS), jnp.float32),
        "rwkv_w0": jax.random.uniform(next(ks), (L, R_WIDTH), jnp.float32, -6.0, 0.5),
        "rwkv_w2": nrm((L, R_LORA_W, R_WIDTH), 0.5 * R_LORA_W ** -0.5),
        "rwkv_a0": nrm((L, R_WIDTH), 0.1),
        "rwkv_a2": nrm((L, R_LORA_A, R_WIDTH), 0.5 * R_LORA_A ** -0.5),
        "rwkv_g2": lin(R_LORA_G, R_WIDTH),
        "rwkv_kk": 0.85 + nrm((L, R_WIDTH), 0.02),
        "rwkv_ka": 1.0 + nrm((L, R_WIDTH), 0.02),
        "rwkv_rk": nrm((L, R_HEADS, R_HEAD), 0.1),
        "rwkv_lnx_w": gain(R_WIDTH),
        "rwkv_lnx_b": nrm((L, R_WIDTH), 0.02),
        "nsa_q_gain": gain(HEAD_DIM),
        "nsa_k_gain": gain(3, HEAD_DIM),
        "cmp_pe_k": nrm((L, CMP_BLOCK, HEAD_DIM), 0.1),
        "cmp_pe_v": nrm((L, CMP_BLOCK, HEAD_DIM), 0.1),
        "cmp_w1_k": lin(CMP_BLOCK * HEAD_DIM, CMP_HIDDEN),
        "cmp_w2_k": lin(CMP_HIDDEN, HEAD_DIM),
        "cmp_w1_v": lin(CMP_BLOCK * HEAD_DIM, CMP_HIDDEN),
        "cmp_w2_v": lin(CMP_HIDDEN, HEAD_DIM),
        "w_branch_rwkv": lin(R_WIDTH, D),
        "w_branch_nsa": lin(N_HEADS * HEAD_DIM, D),
        "w_mix_out": lin(D, D),
        "norm_cross": gain(D),
        "norm_mem": gain(D),
        "ca_wq": lin(D, CA_HEADS * CA_HEAD_DIM),
        "ca_wkv": lin(D, 2 * CA_HEADS * CA_HEAD_DIM),
        "ca_q_gain": gain(CA_HEAD_DIM),
        "ca_k_gain": gain(CA_HEAD_DIM),
        "ca_wo": lin(CA_HEADS * CA_HEAD_DIM, D),
        "norm_ffn": gain(D),
        "ffn_up": lin(D, 2 * D_FF),
        "ffn_conv": nrm((L, CONV_W, D_FF), CONV_W ** -0.5),
        "ffn_conv_b": nrm((L, D_FF), 0.02),
        "ffn_down": lin(D_FF, D),
    }


def reference(x, mem, rel_bias, norm_mix, w_in, rwkv_mu, rwkv_w0, rwkv_w2, rwkv_a0, rwkv_a2, rwkv_g2,
              rwkv_kk, rwkv_ka, rwkv_rk, rwkv_lnx_w, rwkv_lnx_b, nsa_q_gain, nsa_k_gain, cmp_pe_k, cmp_pe_v,
              cmp_w1_k, cmp_w2_k, cmp_w1_v, cmp_w2_v, w_branch_rwkv, w_branch_nsa, w_mix_out,
              norm_cross, norm_mem, ca_wq, ca_wkv, ca_q_gain, ca_k_gain, ca_wo,
              norm_ffn, ffn_up, ffn_conv, ffn_conv_b, ffn_down):
    B, S, D = x.shape
    split_pts = np.cumsum([RWKV_COLS, N_HEADS * HEAD_DIM] + [NSA_KV] * 6 + [3 * N_HEADS]).tolist()
    h = x
    for l in range(DEPTH):
        xn = rms_norm(h, norm_mix[l])
        z = xn @ w_in[l]
        z_r, z_q, z_kc, z_vc, z_ks, z_vs, z_kw, z_vw, z_g, z_m = jnp.split(z, split_pts, axis=-1)
        y_r = rwkv7_mixer(z_r, rwkv_mu[l], rwkv_w0[l], rwkv_w2[l], rwkv_a0[l], rwkv_a2[l], rwkv_g2[l],
                          rwkv_kk[l], rwkv_ka[l], rwkv_rk[l], rwkv_lnx_w[l], rwkv_lnx_b[l])
        y_n = nsa_mixer(z_q, z_kc, z_vc, z_ks, z_vs, z_kw, z_vw, z_g, rel_bias, nsa_q_gain[l], nsa_k_gain[l],
                        cmp_pe_k[l], cmp_pe_v[l], cmp_w1_k[l], cmp_w2_k[l], cmp_w1_v[l], cmp_w2_v[l])
        gm = jax.nn.sigmoid(z_m.astype(jnp.float32)).reshape(B, S, 2, D)
        merged = gm[:, :, 0] * (y_r @ w_branch_rwkv[l]) + gm[:, :, 1] * (y_n @ w_branch_nsa[l])
        h = h + merged.astype(h.dtype) @ w_mix_out[l]
        h = h + cross_attn(rms_norm(h, norm_cross[l]), rms_norm(mem, norm_mem[l]), ca_wq[l], ca_wkv[l],
                           ca_q_gain[l], ca_k_gain[l], ca_wo[l])
        h = h + conv_ffn(rms_norm(h, norm_ffn[l]), ffn_up[l], ffn_conv[l], ffn_conv_b[l], ffn_down[l])
    return h
```

```python
import functools
import math

import jax
import jax.numpy as jnp
import numpy as np
from jax import lax
from jax.experimental import pallas as pl
from jax.experimental.pallas import tpu as pltpu

F32 = jnp.float32
BF16 = jnp.bfloat16
NEG = -1e30

R_HEAD = 64
R_LORA_W = 64
R_LORA_A = 64
R_LORA_G = 160
LNX_EPS = 64e-5
N_HEADS = 16
N_KV = 4
N_HG = N_HEADS // N_KV
HEAD_DIM = 64
CMP_BLOCK = 32
CMP_STRIDE = 16
CMP_HIDDEN = 256
SEL_BLOCK = 64
SEL_TOPK = 16
WINDOW = 512
FORCE_SCORE = 1e4
REL_BUCKETS = 32
REL_MAX_DIST = 128
CA_HEADS = 4
CONV_W = 3
NORM_EPS = 1e-6
QK_EPS = 1e-6

LANE = 128
VMEM_LIMIT = 56 * 1024 * 1024

C_RKV = 0
C_Q = 3072
C_KC = 4096
C_ZG = 5632
C_ZWA = 5888
C_GATE = 6016
C_M = 6144
Z_COLS = 8192


def _dot(a, b):
    return jnp.dot(a, b, preferred_element_type=F32)


def _dot_nt(a, b):
    return lax.dot_general(a, b, (((1,), (1,)), ((), ())), preferred_element_type=F32)


def _split_dot(x, e, parts, mode="xe"):
    acc = None
    rem = x
    for i in range(parts):
        hi = rem.astype(BF16)
        t = _dot(hi, e) if mode == "xe" else (_dot(e, hi) if mode == "ex" else _dot_nt(e, hi))
        acc = t if acc is None else acc + t
        if i + 1 < parts:
            rem = rem - hi.astype(F32)
    return acc


def _rms(x, eps):
    return x * lax.rsqrt(jnp.mean(x * x, axis=-1, keepdims=True) + eps)


def _sigmoid(x):
    return 1.0 / (1.0 + jnp.exp(-x))


def _cparams(sem):
    return pltpu.CompilerParams(dimension_semantics=sem, vmem_limit_bytes=VMEM_LIMIT)


def _norm_matmul_kernel(x_ref, g_ref, w_ref, o_ref, xn_ref):
    @pl.when(pl.program_id(1) == 0)
    def _():
        x = x_ref[...]
        xn_ref[...] = (_rms(x, NORM_EPS) * g_ref[...]).astype(BF16)

    o_ref[...] = _dot(xn_ref[...], w_ref[...]).astype(o_ref.dtype)


def _norm_matmul(x, g, w, *, tm, tn, out_dtype, name):
    m, k = x.shape
    n = w.shape[1]
    return pl.pallas_call(
        _norm_matmul_kernel,
        out_shape=jax.ShapeDtypeStruct((m, n), out_dtype),
        grid=(m // tm, n // tn),
        in_specs=[
            pl.BlockSpec((tm, k), lambda i, j: (i, 0)),
            pl.BlockSpec((1, k), lambda i, j: (0, 0)),
            pl.BlockSpec((k, tn), lambda i, j: (0, j)),
        ],
        out_specs=pl.BlockSpec((tm, tn), lambda i, j: (i, j)),
        scratch_shapes=[pltpu.VMEM((tm, k), BF16)],
        compiler_params=_cparams(("arbitrary", "arbitrary")),
        name=name,
    )(x, g, w)


def _rwkv_kernel(zr_ref, zg_ref, zwa_ref, mur_ref, mug_ref, muwa_ref, w0_ref, w2_ref, a0_ref, a2_ref,
                 g2_ref, kk_ref, ka_ref, rk_ref, lnw_ref, lnb_ref, eh_ref, eht_ref, tri_ref,
                 o_ref, pr_ref, pg_ref, pwa_ref, st_ref, *, L, width):
    i = pl.program_id(1)
    npair = width // LANE

    @pl.when(i == 0)
    def _():
        pr_ref[...] = jnp.zeros_like(pr_ref)
        pg_ref[...] = jnp.zeros_like(pg_ref)
        pwa_ref[...] = jnp.zeros_like(pwa_ref)
        st_ref[...] = jnp.zeros_like(st_ref)

    def shifted8(z, prev_ref, mu):
        rolled = pltpu.roll(z, 1, axis=0)
        row0 = lax.broadcasted_iota(jnp.int32, z.shape, 0) == 0
        prev = jnp.where(row0, prev_ref[7:8, :], rolled)
        prev_ref[...] = z[L - 8:L, :]
        return z + (prev - z) * mu

    zs = shifted8(zr_ref[...], pr_ref, mur_ref[...])
    zsg = shifted8(zg_ref[...], pg_ref, mug_ref[...])
    zswa = shifted8(zwa_ref[...], pwa_ref, muwa_ref[...])

    r = zs[:, 0:width]
    k = zs[:, width:2 * width]
    v = zs[:, 2 * width:3 * width]

    w_lin = w0_ref[...] + _dot(jnp.tanh(zswa).astype(BF16), w2_ref[...])
    a_lin = a0_ref[...] + _dot(zswa.astype(BF16), a2_ref[...])
    y = -w_lin
    softplus = jnp.maximum(y, 0.0) + jnp.log(1.0 + jnp.exp(-jnp.abs(y)))
    ld = -jnp.exp(-softplus - 0.5)
    a = _sigmoid(a_lin)
    g = _dot(_sigmoid(zsg).astype(BF16), g2_ref[...])

    eh = eh_ref[...]
    eht = eht_ref[...]

    def sum_heads(t):
        return _split_dot(t, eh, 2)

    def bcast_heads(t):
        return _split_dot(t, eht, 2)

    kkr = k * kk_ref[...]
    nrm = jnp.maximum(jnp.sqrt(sum_heads(kkr * kkr)), 1e-12)
    kkn = kkr * bcast_heads(1.0 / nrm)
    k2 = k * (1.0 + (a - 1.0) * ka_ref[...])
    av = -kkn
    bv = kkn * a
    bonus = bcast_heads(sum_heads(r * k2 * rk_ref[...])) * v

    lg = _split_dot(ld, tri_ref[...], 3, mode="ex")
    lg_last = lg[L - 1:L, :]
    eg = jnp.exp(lg)
    eng = jnp.exp(-lg)
    egl = jnp.exp(lg_last - lg)
    rt = r * eg
    kt = k2 * eng
    bt = bv * eng
    at = av * jnp.exp(lg - ld)
    kgl = k2 * egl
    bgl = bv * egl
    gl = jnp.exp(lg_last)

    lane = lax.broadcasted_iota(jnp.int32, (L, LANE), 1)
    lo = lane < R_HEAD

    def stack(t):
        return jnp.concatenate([jnp.where(lo, t, 0.0), jnp.where(lo, 0.0, t)], axis=0)

    row = lax.broadcasted_iota(jnp.int32, (2 * L, 2 * L), 0)
    col = lax.broadcasted_iota(jnp.int32, (2 * L, 2 * L), 1)
    strict = row > col
    incl = row >= col
    eye = (row == col).astype(F32)
    nsq = int(math.log2(L)) - 1

    ys = []
    for p in range(npair):
        sl = slice(p * LANE, (p + 1) * LANE)
        at_s, rt_s = stack(at[:, sl]), stack(rt[:, sl])
        bt_s, kt_s = stack(bt[:, sl]), stack(kt[:, sl])
        v_s = stack(v[:, sl]).astype(BF16)
        ar = jnp.concatenate([at_s, rt_s], axis=0).astype(BF16)
        bk = jnp.concatenate([bt_s, kt_s], axis=0).astype(BF16)
        gm = _dot_nt(ar, bk)
        a_ab = jnp.where(strict, gm[0:2 * L, 0:2 * L], 0.0)
        a_ak = jnp.where(strict, gm[0:2 * L, 2 * L:4 * L], 0.0)
        a_rb = jnp.where(incl, gm[2 * L:4 * L, 0:2 * L], 0.0)
        a_rk = jnp.where(incl, gm[2 * L:4 * L, 2 * L:4 * L], 0.0)
        tinv = eye + a_ab
        pw = a_ab
        for _ in range(nsq):
            pwb = pw.astype(BF16)
            pw = _dot(pwb, pwb)
            tinv = tinv + _dot(pw.astype(BF16), tinv.astype(BF16))
        s = st_ref[p]
        sb = s.astype(BF16)
        wmat = _dot_nt(ar[0:2 * L], sb) + _dot(a_ak.astype(BF16), v_s)
        u = _dot(tinv.astype(BF16), wmat.astype(BF16))
        ub = u.astype(BF16)
        yy = _dot_nt(ar[2 * L:4 * L], sb) + _dot(a_rb.astype(BF16), ub) + _dot(a_rk.astype(BF16), v_s)
        ys.append(yy[0:L] + yy[L:2 * L])
        uv = jnp.concatenate([u, v_s.astype(F32)], axis=0)
        bkg = jnp.concatenate([stack(bgl[:, sl]), stack(kgl[:, sl])], axis=0).astype(BF16)
        st_ref[p] = s * gl[:, sl] + _dot(uv.T.astype(BF16), bkg)

    yv = jnp.concatenate(ys, axis=1)
    inv_n = 1.0 / R_HEAD
    mean = bcast_heads(sum_heads(yv) * inv_n)
    yc = yv - mean
    var = bcast_heads(sum_heads(yc * yc) * inv_n)
    yn = yc * lax.rsqrt(var + LNX_EPS) * lnw_ref[...] + lnb_ref[...]
    o_ref[...] = ((yn + bonus) * g).astype(o_ref.dtype)


def _rwkv(z, mu_r, mu_g, mu_wa, w0, w2p, a0, a2p, g2p, kk, ka, rk, lnw, lnb, *, batch, seq, width):
    L = 64
    nt = seq // L
    nheads = width // R_HEAD
    eh = np.zeros((width, LANE), np.float32)
    eh[np.arange(width), np.arange(width) // R_HEAD] = 1.0
    tri = np.tril(np.ones((L, L), np.float32))
    vec = lambda c: pl.BlockSpec((1, c), lambda b, i: (0, 0))
    full = lambda a: pl.BlockSpec(a.shape, lambda b, i: (0,) * a.ndim)
    eh_j = jnp.asarray(eh, BF16)
    eht_j = jnp.asarray(eh.T, BF16)
    tri_j = jnp.asarray(tri, BF16)
    return pl.pallas_call(
        functools.partial(_rwkv_kernel, L=L, width=width),
        out_shape=jax.ShapeDtypeStruct((batch * seq, width), F32),
        grid=(batch, nt),
        in_specs=[
            pl.BlockSpec((L, 3 * width), lambda b, i: (b * nt + i, C_RKV // (3 * width))),
            pl.BlockSpec((L, 256), lambda b, i: (b * nt + i, C_ZG // 256)),
            pl.BlockSpec((L, LANE), lambda b, i: (b * nt + i, C_ZWA // LANE)),
            vec(3 * width), vec(256), vec(LANE),
            vec(width), full(w2p), vec(width), full(a2p), full(g2p),
            vec(width), vec(width), vec(width), vec(width), vec(width),
            full(eh_j), full(eht_j), full(tri_j),
        ],
        out_specs=pl.BlockSpec((L, width), lambda b, i: (b * nt + i, 0)),
        scratch_shapes=[
            pltpu.VMEM((8, 3 * width), F32),
            pltpu.VMEM((8, 256), F32),
            pltpu.VMEM((8, LANE), F32),
            pltpu.VMEM((nheads // 2, 2 * R_HEAD, 2 * R_HEAD), F32),
        ],
        compiler_params=_cparams(("arbitrary", "arbitrary")),
        name="rwkv",
    )(z, z, z, mu_r, mu_g, mu_wa, w0, w2p, a0, a2p, g2p, kk, ka, rk, lnw, lnb, eh_j, eht_j, tri_j)


def _nsa_prep_kernel(zq_ref, zs_ref, zw_ref, qg_ref, kgs_ref, kgw_ref, ehq_ref, ehqt_ref,
                     q_ref, ks_ref, vs_ref, kw_ref, vw_ref):
    inv_d = 1.0 / HEAD_DIM

    def head_norm(x, gain, e, et):
        ss = _split_dot(x * x, e, 2)
        inv = lax.rsqrt(ss * inv_d + QK_EPS)
        return x * _split_dot(inv, et, 2) * gain

    nq = zq_ref.shape[1]
    nk = N_KV * HEAD_DIM
    ehq = ehq_ref[...]
    ehqt = ehqt_ref[...]
    q = head_norm(zq_ref[...], qg_ref[...], ehq, ehqt) * (HEAD_DIM ** -0.5)
    for h in range(nq // HEAD_DIM):
        q_ref[0, h] = q[:, h * HEAD_DIM:(h + 1) * HEAD_DIM].astype(q_ref.dtype)
    zs = zs_ref[...]
    zw = zw_ref[...]
    ehk = ehq[0:nk, :]
    ehkt = ehqt[:, 0:nk]
    ksn = head_norm(zs[:, 0:nk], kgs_ref[...], ehk, ehkt)
    kwn = head_norm(zw[:, 0:nk], kgw_ref[...], ehk, ehkt)
    vsv = zs[:, nk:2 * nk]
    vwv = zw[:, nk:2 * nk]
    for gq in range(N_KV):
        sl = slice(gq * HEAD_DIM, (gq + 1) * HEAD_DIM)
        ks_ref[0, gq] = ksn[:, sl].astype(ks_ref.dtype)
        vs_ref[0, gq] = vsv[:, sl].astype(vs_ref.dtype)
        kw_ref[0, gq] = kwn[:, sl].astype(kw_ref.dtype)
        vw_ref[0, gq] = vwv[:, sl].astype(vw_ref.dtype)


def _nsa_prep(z, q_gain_t, kgs_t, kgw_t, *, batch, seq):
    tb = 256
    nt = seq // tb
    nq = N_HEADS * HEAD_DIM
    nk = N_KV * HEAD_DIM
    eh = np.zeros((nq, LANE), np.float32)
    eh[np.arange(nq), np.arange(nq) // HEAD_DIM] = 1.0
    eh_j = jnp.asarray(eh, BF16)
    eht_j = jnp.asarray(eh.T, BF16)
    kv_shape = jax.ShapeDtypeStruct((batch, N_KV, seq, HEAD_DIM), BF16)
    kv_spec = pl.BlockSpec((1, N_KV, tb, HEAD_DIM), lambda b, i: (b, 0, i, 0))
    return pl.pallas_call(
        _nsa_prep_kernel,
        out_shape=(jax.ShapeDtypeStruct((batch, N_HEADS, seq, HEAD_DIM), BF16),
                   kv_shape, kv_shape, kv_shape, kv_shape),
        grid=(batch, nt),
        in_specs=[
            pl.BlockSpec((tb, nq), lambda b, i: (b * nt + i, C_Q // nq)),
            pl.BlockSpec((tb, 2 * nk), lambda b, i: (b * nt + i, C_KC // (2 * nk) + 1)),
            pl.BlockSpec((tb, 2 * nk), lambda b, i: (b * nt + i, C_KC // (2 * nk) + 2)),
            pl.BlockSpec((1, nq), lambda b, i: (0, 0)),
            pl.BlockSpec((1, nk), lambda b, i: (0, 0)),
            pl.BlockSpec((1, nk), lambda b, i: (0, 0)),
            pl.BlockSpec(eh_j.shape, lambda b, i: (0, 0)),
            pl.BlockSpec(eht_j.shape, lambda b, i: (0, 0)),
        ],
        out_specs=(pl.BlockSpec((1, N_HEADS, tb, HEAD_DIM), lambda b, i: (b, 0, i, 0)),
                   kv_spec, kv_spec, kv_spec, kv_spec),
        compiler_params=_cparams(("arbitrary", "arbitrary")),
        name="nsa_prep",
    )(z, z, z, q_gain_t, kgs_t, kgw_t, eh_j, eht_j)


def _nsa_cmp_kernel(ak_ref, av_ref, pek_ref, pev_ref, w1k_ref, w2k_ref, w1v_ref, w2v_ref, kg_ref,
                    kc_ref, vc_ref, *, n_cmp):
    ncp = kc_ref.shape[2]
    half = w1k_ref.shape[0] // 2

    def compress(a_ref, pe_ref, w1_ref, w2_ref):
        rows = a_ref.shape[2]
        a = a_ref[0, 0].astype(BF16)
        p1 = _dot(a, w1_ref[0:half, :])
        p2 = _dot(a, w1_ref[half:2 * half, :])
        pe8 = jnp.broadcast_to(pe_ref[...], (8, pe_ref.shape[1])).astype(BF16)
        pe_bias = _dot(pe8, w1_ref[...])[0:1, :]
        hpre = p1 + pltpu.roll(p2, rows - 1, axis=0) + pe_bias
        hid = 0.5 * hpre * (1.0 + jnp.tanh(math.sqrt(2.0 / math.pi) * (hpre + 0.044715 * hpre * hpre * hpre)))
        out = _dot(hid.astype(BF16), w2_ref[...])
        if rows < ncp:
            out = jnp.concatenate([out, jnp.zeros((ncp - rows, out.shape[1]), F32)], axis=0)
        return out

    valid = lax.broadcasted_iota(jnp.int32, (ncp, HEAD_DIM), 0) < n_cmp
    kc = compress(ak_ref, pek_ref, w1k_ref, w2k_ref)
    kc = _rms(kc, QK_EPS) * kg_ref[...]
    kc_ref[0, 0] = jnp.where(valid, kc, 0.0).astype(kc_ref.dtype)
    vc = compress(av_ref, pev_ref, w1v_ref, w2v_ref)
    vc_ref[0, 0] = jnp.where(valid, vc, 0.0).astype(vc_ref.dtype)


def _nsa_cmp(ak, av, pek, pev, w1k, w2k, w1v, w2v, kg, *, ncp, n_cmp):
    batch, ng, rows, wid = ak.shape
    a_spec = pl.BlockSpec((1, 1, rows, wid), lambda b, g: (b, g, 0, 0))
    full = lambda a: pl.BlockSpec(a.shape, lambda b, g: (0,) * a.ndim)
    o_shape = jax.ShapeDtypeStruct((batch, ng, ncp, HEAD_DIM), BF16)
    o_spec = pl.BlockSpec((1, 1, ncp, HEAD_DIM), lambda b, g: (b, g, 0, 0))
    return pl.pallas_call(
        functools.partial(_nsa_cmp_kernel, n_cmp=n_cmp),
        out_shape=(o_shape, o_shape),
        grid=(batch, ng),
        in_specs=[a_spec, a_spec, full(pek), full(pev), full(w1k), full(w2k), full(w1v), full(w2v), full(kg)],
        out_specs=(o_spec, o_spec),
        compiler_params=_cparams(("arbitrary", "arbitrary")),
        name="nsa_cmp",
    )(ak, av, pek, pev, w1k, w2k, w1v, w2v, kg)


def _bucket_thresholds():
    exact = REL_BUCKETS // 2
    n = np.arange(0, REL_MAX_DIST + 1)
    nf = np.maximum(n, 1).astype(np.float64)
    large = exact + (np.log(nf / exact) / math.log(REL_MAX_DIST / exact) * (REL_BUCKETS - exact)).astype(np.int64)
    large = np.minimum(large, REL_BUCKETS - 1)
    bucket = np.where(n < exact, n, large)
    assert bucket[-1] == REL_BUCKETS - 1
    return [int(np.argmax(bucket >= b)) for b in range(REL_BUCKETS)]


def _nsa_bias_kernel(rel_ref, bw_ref, pat_ref, *, tq, ncp):
    h = pl.program_id(0)
    thr = _bucket_thresholds()

    def bias_of(d):
        val = jnp.full(d.shape, rel_ref[0, h], F32)
        for b in range(1, REL_BUCKETS):
            val = jnp.where(d >= thr[b], rel_ref[b, h], val)
        return jnp.where(d < 0, NEG, val)

    qq = lax.broadcasted_iota(jnp.int32, (tq, tq), 0)
    kk = lax.broadcasted_iota(jnp.int32, (tq, tq), 1)
    bw_ref[0, 0] = bias_of(qq - kk)
    bw_ref[0, 1] = bias_of(qq - kk + tq)
    qc = lax.broadcasted_iota(jnp.int32, (tq, 2 * ncp), 0)
    cc = lax.broadcasted_iota(jnp.int32, (tq, 2 * ncp), 1)
    pat_ref[0] = bias_of(qc - CMP_STRIDE * (cc - ncp) - (CMP_BLOCK - 1))


def _nsa_bias(rel, *, tq, ncp):
    return pl.pallas_call(
        functools.partial(_nsa_bias_kernel, tq=tq, ncp=ncp),
        out_shape=(jax.ShapeDtypeStruct((N_HEADS, 2, tq, tq), F32),
                   jax.ShapeDtypeStruct((N_HEADS, tq, 2 * ncp), F32)),
        grid=(N_HEADS,),
        in_specs=[pl.BlockSpec(memory_space=pltpu.SMEM)],
        out_specs=(pl.BlockSpec((1, 2, tq, tq), lambda h: (h, 0, 0, 0)),
                   pl.BlockSpec((1, tq, 2 * ncp), lambda h: (h, 0, 0))),
        compiler_params=_cparams(("arbitrary",)),
        name="nsa_bias",
    )(rel)


def _nsa_attn_kernel(rel_ref, q_ref, kc_ref, vc_ref, ks_ref, vs_ref, kw_ref, vw_ref, zg_ref, pat_ref,
                     bw_ref, msel_ref, ex_ref, o_ref, *, tq, ncp, nb):
    g = pl.program_id(1)
    qi = pl.program_id(2)
    rows = N_HG * tq
    q = q_ref[0].reshape(rows, HEAD_DIM)

    def tile4(x):
        return jnp.concatenate([x] * N_HG, axis=0)

    s_c = _dot_nt(q, kc_ref[0, 0])
    shift = qi * (tq // CMP_STRIDE)
    bias_c = jnp.concatenate(
        [pltpu.roll(pat_ref[h], shift, axis=1)[:, ncp:2 * ncp] for h in range(N_HG)], axis=0)
    valid_c = bias_c > 0.5 * NEG
    sc = s_c + bias_c
    m_c = jnp.max(sc, axis=-1, keepdims=True)
    e_c = jnp.where(valid_c, jnp.exp(sc - m_c), 0.0)
    l_c = jnp.sum(e_c, axis=-1, keepdims=True)
    p_c = e_c / jnp.where(l_c > 0.0, l_c, 1.0)
    o_c = _dot(p_c.astype(BF16), vc_ref[0, 0])

    psum = p_c[0:tq]
    for h in range(1, N_HG):
        psum = psum + p_c[h * tq:(h + 1) * tq]
    imp_t = _split_dot(psum, msel_ref[...], 3, mode="ext")
    jj = lax.broadcasted_iota(jnp.int32, (nb, tq), 0)
    tt = qi * tq + lax.broadcasted_iota(jnp.int32, (nb, tq), 1)
    cur = tt // SEL_BLOCK
    forced = (jj == 0) | (jj == cur) | (jj == cur - 1)
    valid_b = jj * SEL_BLOCK <= tt
    imp = jnp.where(valid_b, jnp.where(forced, FORCE_SCORE, imp_t), -jnp.inf)
    cnt = jnp.zeros((nb, tq), F32)
    for i in range(nb):
        ri = imp[i:i + 1, :]
        cnt = cnt + jnp.where(jj > i, jnp.where(ri >= imp, 1.0, 0.0), jnp.where(ri > imp, 1.0, 0.0))
    sel_t = jnp.where((cnt < float(min(SEL_TOPK, nb))) & valid_b, 1.0, 0.0)
    nbp = ex_ref.shape[0]
    if nbp > nb:
        sel_t = jnp.concatenate([sel_t, jnp.zeros((nbp - nb, tq), F32)], axis=0)
    sel = sel_t.T.astype(BF16)

    crow = lax.broadcasted_iota(jnp.int32, (rows, 1), 0) // tq
    cvec = jnp.zeros((rows, 1), F32)
    for h in range(N_HG):
        cvec = jnp.where(crow == h, rel_ref[REL_BUCKETS - 1, g * N_HG + h], cvec)

    def bias_blk(which):
        return jnp.concatenate([bw_ref[h, which] for h in range(N_HG)], axis=0)

    def flash(carry, s, v):
        m, l, acc = carry
        m_new = jnp.maximum(m, jnp.max(s, axis=-1, keepdims=True))
        alpha = jnp.exp(m - m_new)
        p = jnp.exp(s - m_new)
        l = alpha * l + jnp.sum(p, axis=-1, keepdims=True)
        acc = alpha * acc + _dot(p.astype(BF16), v)
        return m_new, l, acc

    init = (jnp.full((rows, 1), NEG, F32), jnp.zeros((rows, 1), F32), jnp.zeros((rows, HEAD_DIM), F32))

    def finish(carry):
        m, l, acc = carry
        return acc / l

    def sel_tile(kt, bias):
        off = pl.multiple_of(kt * tq, tq)
        kx = ks_ref[0, 0, pl.ds(off, tq), :]
        vx = vs_ref[0, 0, pl.ds(off, tq), :]
        s = _dot_nt(q, kx)
        selx = tile4(_dot(sel, ex_ref[:, pl.ds(off, tq)])) > 0.5
        return jnp.where(selx, s + bias, NEG), vx

    def far_body(kt, carry):
        s, vx = sel_tile(kt, cvec)
        return flash(carry, s, vx)

    def pen(d):
        return jnp.where(qi >= d, 0.0, NEG)

    carry = lax.fori_loop(0, jnp.maximum(qi - 1, 0), far_body, init)
    b1 = bias_blk(1) + pen(1)
    b0 = bias_blk(0)
    s, vx = sel_tile(jnp.maximum(qi - 1, 0), b1)
    carry = flash(carry, s, vx)
    s, vx = sel_tile(qi, b0)
    carry = flash(carry, s, vx)
    o_s = finish(carry)

    nd = WINDOW // tq
    qrow = lax.broadcasted_iota(jnp.int32, (rows, tq), 0) % tq
    kcol = lax.broadcasted_iota(jnp.int32, (rows, tq), 1)
    carry = init
    for d in range(nd, -1, -1):
        kt = jnp.maximum(qi - d, 0)
        off = pl.multiple_of(kt * tq, tq)
        kx = kw_ref[0, 0, pl.ds(off, tq), :]
        vx = vw_ref[0, 0, pl.ds(off, tq), :]
        s = _dot_nt(q, kx)
        if d == 0:
            s = s + b0
        elif d == 1:
            s = s + b1
        elif d == nd:
            s = jnp.where(kcol > qrow, s + (cvec + pen(d)), NEG)
        else:
            s = s + (cvec + pen(d))
        carry = flash(carry, s, vx)
    o_w = finish(carry)

    zg = pltpu.roll(zg_ref[...], (LANE - 16 * g) % LANE, axis=1)
    gates = _sigmoid(zg)
    outs = []
    for h in range(N_HG):
        rs = slice(h * tq, (h + 1) * tq)
        outs.append(gates[:, h:h + 1] * o_c[rs] + gates[:, 4 + h:5 + h] * o_s[rs]
                    + gates[:, 8 + h:9 + h] * o_w[rs])
    o_ref[...] = jnp.concatenate(outs, axis=1).astype(o_ref.dtype)


def _nsa_attn(rel, qn, kc, vc, ks, vs, kw, vw, z, pat, bw, *, batch, seq, tq, ncp):
    nt = seq // tq
    nb = seq // SEL_BLOCK
    n_cmp = (seq - CMP_BLOCK) // CMP_STRIDE + 1
    ci = np.arange(ncp)[None, :] * CMP_STRIDE
    sj = np.arange(nb)[:, None] * SEL_BLOCK
    msel = ((ci <= sj + SEL_BLOCK - 1) & (ci + CMP_BLOCK - 1 >= sj) & (np.arange(ncp)[None, :] < n_cmp))
    msel_j = jnp.asarray(msel.astype(np.float32), BF16)
    nbp = -(-nb // LANE) * LANE
    ex = (np.arange(seq)[None, :] // SEL_BLOCK == np.arange(nbp)[:, None])
    ex_j = jnp.asarray(ex.astype(np.float32), BF16)
    kv_spec = pl.BlockSpec((1, 1, seq, HEAD_DIM), lambda b, g, i: (b, g, 0, 0))
    c_spec = pl.BlockSpec((1, 1, ncp, HEAD_DIM), lambda b, g, i: (b, g, 0, 0))
    width = N_HEADS * HEAD_DIM
    return pl.pallas_call(
        functools.partial(_nsa_attn_kernel, tq=tq, ncp=ncp, nb=nb),
        out_shape=jax.ShapeDtypeStruct((batch * seq, width), F32),
        grid=(batch, N_KV, nt),
        in_specs=[
            pl.BlockSpec(memory_space=pltpu.SMEM),
            pl.BlockSpec((1, N_HG, tq, HEAD_DIM), lambda b, g, i: (b, g, i, 0)),
            c_spec, c_spec, kv_spec, kv_spec, kv_spec, kv_spec,
            pl.BlockSpec((tq, LANE), lambda b, g, i: (b * nt + i, C_GATE // LANE)),
            pl.BlockSpec((N_HG, tq, 2 * ncp), lambda b, g, i: (g, 0, 0)),
            pl.BlockSpec((N_HG, 2, tq, tq), lambda b, g, i: (g, 0, 0, 0)),
            pl.BlockSpec(msel_j.shape, lambda b, g, i: (0, 0)),
            pl.BlockSpec(ex_j.shape, lambda b, g, i: (0, 0)),
        ],
        out_specs=pl.BlockSpec((tq, N_HG * HEAD_DIM), lambda b, g, i: (b * nt + i, g)),
        compiler_params=_cparams(("arbitrary", "arbitrary", "arbitrary")),
        name="nsa_attn",
    )(rel, qn, kc, vc, ks, vs, kw, vw, z, pat, bw, msel_j, ex_j)


def _merge_kernel(x_ref, yr_ref, yn_ref, zm_ref, wr_ref, wn_ref, wm_ref, o_ref):
    d = x_ref.shape[1]
    zm = zm_ref[...]
    br = _dot(yr_ref[...].astype(BF16), wr_ref[...])
    bn = _dot(yn_ref[...].astype(BF16), wn_ref[...])
    merged = _sigmoid(zm[:, 0:d]) * br + _sigmoid(zm[:, d:2 * d]) * bn
    o_ref[...] = x_ref[...] + _dot(merged.astype(BF16), wm_ref[...])


def _merge(x, yr, yn, z, wr, wn, wm, *, tm):
    m, d = x.shape
    row = lambda c: pl.BlockSpec((tm, c), lambda i: (i, 0))
    full = lambda a: pl.BlockSpec(a.shape, lambda i: (0, 0))
    return pl.pallas_call(
        _merge_kernel,
        out_shape=jax.ShapeDtypeStruct((m, d), F32),
        grid=(m // tm,),
        in_specs=[row(d), row(d), row(d), pl.BlockSpec((tm, 2 * d), lambda i: (i, C_M // (2 * d))),
                  full(wr), full(wn), full(wm)],
        out_specs=row(d),
        compiler_params=_cparams(("arbitrary",)),
        name="merge",
    )(x, yr, yn, z, wr, wn, wm)


def _ca_kv_kernel(mem_ref, g_ref, w_ref, kg_ref, k_ref, v_ref):
    d = mem_ref.shape[2]
    dh = d // CA_HEADS
    mn = (_rms(mem_ref[0], NORM_EPS) * g_ref[...]).astype(BF16)
    kv = _dot(mn, w_ref[...])
    ks = [(_rms(kv[:, h * dh:(h + 1) * dh], QK_EPS) * kg_ref[...]) for h in range(CA_HEADS)]
    k_ref[0] = jnp.concatenate(ks, axis=1).astype(k_ref.dtype)
    v_ref[0] = kv[:, d:2 * d].astype(v_ref.dtype)


def _ca_kv(mem, g, wkv, kg):
    batch, nm, d = mem.shape
    o_shape = jax.ShapeDtypeStruct((batch, nm, d), BF16)
    o_spec = pl.BlockSpec((1, nm, d), lambda b: (b, 0, 0))
    full = lambda a: pl.BlockSpec(a.shape, lambda b: (0, 0))
    return pl.pallas_call(
        _ca_kv_kernel,
        out_shape=(o_shape, o_shape),
        grid=(batch,),
        in_specs=[o_spec, full(g), full(wkv), full(kg)],
        out_specs=(o_spec, o_spec),
        compiler_params=_cparams(("arbitrary",)),
        name="ca_kv",
    )(mem, g, wkv, kg)


def _cross_kernel(x_ref, g_ref, wq_ref, qg_ref, k_ref, v_ref, wo_ref, o_ref):
    d = x_ref.shape[1]
    dh = d // CA_HEADS
    x = x_ref[...]
    xn = (_rms(x, NORM_EPS) * g_ref[...]).astype(BF16)
    qf = _dot(xn, wq_ref[...])
    k = k_ref[0]
    v = v_ref[0]
    outs = []
    for h in range(CA_HEADS):
        sl = slice(h * dh, (h + 1) * dh)
        qh = (_rms(qf[:, sl], QK_EPS) * qg_ref[...] * (dh ** -0.5)).astype(BF16)
        s = _dot_nt(qh, k[:, sl])
        s = s - jnp.max(s, axis=-1, keepdims=True)
        e = jnp.exp(s)
        p = e / jnp.sum(e, axis=-1, keepdims=True)
        outs.append(_dot(p.astype(BF16), v[:, sl]))
    o = jnp.concatenate(outs, axis=1).astype(BF16)
    o_ref[...] = x + _dot(o, wo_ref[...])


def _cross(h1, g, wq, qg, kn, vv, wo, *, batch, seq, tm):
    m, d = h1.shape
    nt = seq // tm
    nm = kn.shape[1]
    row = pl.BlockSpec((tm, d), lambda b, i: (b * nt + i, 0))
    full = lambda a: pl.BlockSpec(a.shape, lambda b, i: (0, 0))
    kv_spec = pl.BlockSpec((1, nm, d), lambda b, i: (b, 0, 0))
    return pl.pallas_call(
        _cross_kernel,
        out_shape=jax.ShapeDtypeStruct((m, d), F32),
        grid=(batch, nt),
        in_specs=[row, full(g), full(wq), full(qg), kv_spec, kv_spec, full(wo)],
        out_specs=row,
        compiler_params=_cparams(("arbitrary", "arbitrary")),
        name="cross",
    )(h1, g, wq, qg, kn, vv, wo)


def _ffn_kernel(x_ref, g_ref, wa_ref, wb_ref, cw_ref, cb_ref, wd_ref, o_ref, xn_ref, acc_ref, carry_ref,
                *, tiles_per_seq):
    i = pl.program_id(0)
    j = pl.program_id(1)
    nj = pl.num_programs(1)
    tm = x_ref.shape[0]

    @pl.when(j == 0)
    def _():
        xn_ref[...] = (_rms(x_ref[...], NORM_EPS) * g_ref[...]).astype(BF16)
        acc_ref[...] = jnp.zeros_like(acc_ref)

    @pl.when(i % tiles_per_seq == 0)
    def _():
        carry_ref[j] = jnp.zeros(carry_ref.shape[1:], F32)

    xn = xn_ref[...]
    a = _dot(xn, wa_ref[...])
    b = _dot(xn, wb_ref[...])
    car = carry_ref[j]
    rowi = lax.broadcasted_iota(jnp.int32, a.shape, 0)
    p1 = jnp.where(rowi == 0, car[7:8, :], pltpu.roll(a, 1, axis=0))
    p2 = jnp.where(rowi == 0, car[6:7, :], jnp.where(rowi == 1, car[7:8, :], pltpu.roll(a, 2, axis=0)))
    carry_ref[j] = a[tm - 8:tm, :]
    cw = cw_ref[0]
    conv = cw[0:1, :] * p2 + cw[1:2, :] * p1 + cw[2:3, :] * a + cb_ref[0]
    act = conv * _sigmoid(conv) * b
    acc_ref[...] += _dot(act.astype(BF16), wd_ref[...])

    @pl.when(j == nj - 1)
    def _():
        o_ref[...] = x_ref[...] + acc_ref[...]


def _ffn(h2, g, wup, cw, cb, wd, *, seq, tm, tn):
    m, d = h2.shape
    dff = wd.shape[0]
    nj = dff // tn
    cw3 = jnp.zeros((nj, 8, tn), F32).at[:, 0:CONV_W, :].set(cw.reshape(CONV_W, nj, tn).transpose(1, 0, 2))
    cb3 = cb.reshape(nj, 1, tn)
    return pl.pallas_call(
        functools.partial(_ffn_kernel, tiles_per_seq=seq // tm),
        out_shape=jax.ShapeDtypeStruct((m, d), F32),
        grid=(m // tm, nj),
        in_specs=[
            pl.BlockSpec((tm, d), lambda i, j: (i, 0)),
            pl.BlockSpec((1, d), lambda i, j: (0, 0)),
            pl.BlockSpec((d, tn), lambda i, j: (0, j)),
            pl.BlockSpec((d, tn), lambda i, j: (0, nj + j)),
            pl.BlockSpec((1, 8, tn), lambda i, j: (j, 0, 0)),
            pl.BlockSpec((1, 1, tn), lambda i, j: (j, 0, 0)),
            pl.BlockSpec((tn, d), lambda i, j: (j, 0)),
        ],
        out_specs=pl.BlockSpec((tm, d), lambda i, j: (i, 0)),
        scratch_shapes=[pltpu.VMEM((tm, d), BF16), pltpu.VMEM((tm, d), F32), pltpu.VMEM((nj, 8, tn), F32)],
        compiler_params=_cparams(("arbitrary", "arbitrary")),
        name="ffn",
    )(h2, g, wup, wup, cw3, cb3, wd)


def _pack_perm(width):
    o_zw = 3 * width
    o_za = o_zw + R_LORA_W
    o_zg = o_za + R_LORA_A
    o_q = o_zg + R_LORA_G
    nkv = N_KV * HEAD_DIM
    o_kc = o_q + N_HEADS * HEAD_DIM
    o_gate = o_kc + 6 * nkv
    o_m = o_gate + 3 * N_HEADS
    perm = np.full((Z_COLS,), -1, np.int64)
    perm[C_RKV:C_RKV + 3 * width] = np.arange(3 * width)
    perm[C_Q:C_Q + N_HEADS * HEAD_DIM] = o_q + np.arange(N_HEADS * HEAD_DIM)
    perm[C_KC:C_KC + 6 * nkv] = o_kc + np.arange(6 * nkv)
    perm[C_ZG:C_ZG + R_LORA_G] = o_zg + np.arange(R_LORA_G)
    perm[C_ZWA:C_ZWA + R_LORA_W + R_LORA_A] = o_zw + np.arange(R_LORA_W + R_LORA_A)
    for g in range(N_KV):
        for c in range(3):
            for h in range(N_HG):
                perm[C_GATE + 16 * g + 4 * c + h] = o_gate + (g * N_HG + h) * 3 + c
    perm[C_M:C_M + 2 * width] = o_m + np.arange(2 * width)
    return perm, o_m + 2 * width


def _pad_rows(w, rows, offset=0):
    out = jnp.zeros((rows, w.shape[1]), w.dtype)
    return out.at[offset:offset + w.shape[0]].set(w)


def kernel(x, mem, rel_bias, norm_mix, w_in, rwkv_mu, rwkv_w0, rwkv_w2, rwkv_a0, rwkv_a2, rwkv_g2,
           rwkv_kk, rwkv_ka, rwkv_rk, rwkv_lnx_w, rwkv_lnx_b, nsa_q_gain, nsa_k_gain, cmp_pe_k, cmp_pe_v,
           cmp_w1_k, cmp_w2_k, cmp_w1_v, cmp_w2_v, w_branch_rwkv, w_branch_nsa, w_mix_out,
           norm_cross, norm_mem, ca_wq, ca_wkv, ca_q_gain, ca_k_gain, ca_wo,
           norm_ffn, ffn_up, ffn_conv, ffn_conv_b, ffn_down):
    batch, seq, d = x.shape
    depth = w_in.shape[0]
    width = d
    assert d == N_HEADS * HEAD_DIM and 2 * (Z_COLS - C_M) == 4 * d
    perm, in_cols = _pack_perm(width)
    assert w_in.shape[2] == in_cols
    perm_j = jnp.asarray(np.maximum(perm, 0), jnp.int32)
    keep = jnp.asarray(perm >= 0)

    tq = 128
    assert seq % 256 == 0 and tq > REL_MAX_DIST - 1 and WINDOW % tq == 0
    ncp = -(-(seq // CMP_STRIDE) // LANE) * LANE
    n_cmp = (seq - CMP_BLOCK) // CMP_STRIDE + 1
    tm = min(1024, seq)
    row = lambda v: v.reshape(1, -1).astype(F32)
    nkv = N_KV * HEAD_DIM

    h = x.reshape(batch * seq, d)
    for l in range(depth):
        w_in_p = jnp.where(keep[None, :], jnp.take(w_in[l], perm_j, axis=1), 0.0).astype(BF16)
        mu = rwkv_mu[l]
        mu_r = row(mu[0:3 * width])
        mu_wa = row(mu[3 * width:3 * width + R_LORA_W + R_LORA_A])
        mu_g = row(jnp.zeros((256,), F32).at[0:R_LORA_G].set(mu[3 * width + R_LORA_W + R_LORA_A:]))
        w2p = _pad_rows(rwkv_w2[l], LANE, 0).astype(BF16)
        a2p = _pad_rows(rwkv_a2[l], LANE, R_LORA_W).astype(BF16)
        g2p = _pad_rows(rwkv_g2[l], 256, 0).astype(BF16)

        z = _norm_matmul(h, row(norm_mix[l]), w_in_p, tm=tm, tn=1024, out_dtype=F32, name="in_proj")

        y_r = _rwkv(z, mu_r, mu_g, mu_wa, row(rwkv_w0[l]), w2p, row(rwkv_a0[l]), a2p, g2p,
                    row(rwkv_kk[l]), row(rwkv_ka[l]), row(rwkv_rk[l]), row(rwkv_lnx_w[l]),
                    row(rwkv_lnx_b[l]), batch=batch, seq=seq, width=width)

        qn, ks, vs, kw, vw = _nsa_prep(
            z, row(jnp.tile(nsa_q_gain[l], N_HEADS)), row(jnp.tile(nsa_k_gain[l, 1], N_KV)),
            row(jnp.tile(nsa_k_gain[l, 2], N_KV)), batch=batch, seq=seq)
        z3 = z.reshape(batch, seq // CMP_STRIDE, CMP_STRIDE, Z_COLS)

        def groups(c0):
            t = z3[:, :, :, c0:c0 + nkv].reshape(batch, seq // CMP_STRIDE, CMP_STRIDE, N_KV, HEAD_DIM)
            return t.transpose(0, 3, 1, 2, 4).reshape(batch, N_KV, seq // CMP_STRIDE, CMP_STRIDE * HEAD_DIM)

        kc, vc = _nsa_cmp(groups(C_KC), groups(C_KC + nkv), cmp_pe_k[l].reshape(1, -1), cmp_pe_v[l].reshape(1, -1),
                          cmp_w1_k[l].astype(BF16), cmp_w2_k[l].astype(BF16), cmp_w1_v[l].astype(BF16),
                          cmp_w2_v[l].astype(BF16), row(nsa_k_gain[l, 0]), ncp=ncp, n_cmp=n_cmp)
        bw, pat = _nsa_bias(rel_bias.astype(F32), tq=tq, ncp=ncp)
        y_n = _nsa_attn(rel_bias.astype(F32), qn, kc, vc, ks, vs, kw, vw, z, pat, bw,
                        batch=batch, seq=seq, tq=tq, ncp=ncp)

        h1 = _merge(h, y_r, y_n, z, w_branch_rwkv[l].astype(BF16), w_branch_nsa[l].astype(BF16),
                    w_mix_out[l].astype(BF16), tm=min(512, seq))

        kn, vv = _ca_kv(mem, row(norm_mem[l]), ca_wkv[l].astype(BF16), row(ca_k_gain[l]))
        h2 = _cross(h1, row(norm_cross[l]), ca_wq[l].astype(BF16), row(ca_q_gain[l]), kn, vv,
                    ca_wo[l].astype(BF16), batch=batch, seq=seq, tm=min(512, seq))

        h = _ffn(h2, row(norm_ffn[l]), ffn_up[l].astype(BF16), ffn_conv[l], ffn_conv_b[l],
                 ffn_down[l].astype(BF16), seq=seq, tm=tm, tn=256)
    return h.reshape(batch, seq, d)
```

```python
import functools
import math

import jax
import jax.numpy as jnp
import numpy as np
from jax import lax
from jax.experimental import pallas as pl
from jax.experimental.pallas import tpu as pltpu

F32 = jnp.float32
BF16 = jnp.bfloat16
NEG = -1e30
LOG2E = 1.4426950408889634
FAR_TILES = 4

R_HEAD = 64
R_LORA_W = 64
R_LORA_A = 64
R_LORA_G = 160
LNX_EPS = 64e-5
N_HEADS = 16
N_KV = 4
N_HG = N_HEADS // N_KV
HEAD_DIM = 64
CMP_BLOCK = 32
CMP_STRIDE = 16
CMP_HIDDEN = 256
SEL_BLOCK = 64
SEL_TOPK = 16
WINDOW = 512
FORCE_SCORE = 1e4
REL_BUCKETS = 32
REL_MAX_DIST = 128
CA_HEADS = 4
CONV_W = 3
NORM_EPS = 1e-6
QK_EPS = 1e-6

LANE = 128
VMEM_LIMIT = 56 * 1024 * 1024

C_RKV = 0
C_Q = 3072
C_KC = 4096
C_ZG = 5632
C_ZWA = 5888
C_GATE = 6016
C_M = 6144
Z_COLS = 8192


def _dot(a, b):
    return jnp.dot(a, b, preferred_element_type=F32)


def _dot_nt(a, b):
    return lax.dot_general(a, b, (((1,), (1,)), ((), ())), preferred_element_type=F32)


def _split_dot(x, e, parts, mode="xe"):
    acc = None
    rem = x
    for i in range(parts):
        hi = rem.astype(BF16)
        t = _dot(hi, e) if mode == "xe" else (_dot(e, hi) if mode == "ex" else _dot_nt(e, hi))
        acc = t if acc is None else acc + t
        if i + 1 < parts:
            rem = rem - hi.astype(F32)
    return acc


def _rms(x, eps):
    return x * lax.rsqrt(jnp.mean(x * x, axis=-1, keepdims=True) + eps)


def _sigmoid(x):
    return 1.0 / (1.0 + jnp.exp(-x))


def _cparams(sem):
    return pltpu.CompilerParams(dimension_semantics=sem, vmem_limit_bytes=VMEM_LIMIT)


def _norm_matmul_kernel(x_ref, g_ref, w_ref, o_ref, xn_ref):
    @pl.when(pl.program_id(1) == 0)
    def _():
        x = x_ref[...]
        xn_ref[...] = (_rms(x, NORM_EPS) * g_ref[...]).astype(BF16)

    o_ref[...] = _dot(xn_ref[...], w_ref[...]).astype(o_ref.dtype)


def _norm_matmul(x, g, w, *, tm, tn, out_dtype, name):
    m, k = x.shape
    n = w.shape[1]
    return pl.pallas_call(
        _norm_matmul_kernel,
        out_shape=jax.ShapeDtypeStruct((m, n), out_dtype),
        grid=(m // tm, n // tn),
        in_specs=[
            pl.BlockSpec((tm, k), lambda i, j: (i, 0)),
            pl.BlockSpec((1, k), lambda i, j: (0, 0)),
            pl.BlockSpec((k, tn), lambda i, j: (0, j)),
        ],
        out_specs=pl.BlockSpec((tm, tn), lambda i, j: (i, j)),
        scratch_shapes=[pltpu.VMEM((tm, k), BF16)],
        compiler_params=_cparams(("arbitrary", "arbitrary")),
        name=name,
    )(x, g, w)


def _rwkv_kernel(zr_ref, zg_ref, zwa_ref, mur_ref, mug_ref, muwa_ref, w0_ref, w2_ref, a0_ref, a2_ref,
                 g2_ref, kk_ref, ka_ref, rk_ref, lnw_ref, lnb_ref, eh_ref, eht_ref, tri_ref,
                 o_ref, pr_ref, pg_ref, pwa_ref, st_ref, *, L, width):
    i = pl.program_id(1)
    npair = width // LANE

    @pl.when(i == 0)
    def _():
        pr_ref[...] = jnp.zeros_like(pr_ref)
        pg_ref[...] = jnp.zeros_like(pg_ref)
        pwa_ref[...] = jnp.zeros_like(pwa_ref)
        st_ref[...] = jnp.zeros_like(st_ref)

    def shifted8(z, prev_ref, mu):
        rolled = pltpu.roll(z, 1, axis=0)
        row0 = lax.broadcasted_iota(jnp.int32, z.shape, 0) == 0
        prev = jnp.where(row0, prev_ref[7:8, :], rolled)
        prev_ref[...] = z[L - 8:L, :]
        return z + (prev - z) * mu

    zs = shifted8(zr_ref[...], pr_ref, mur_ref[...])
    zsg = shifted8(zg_ref[...], pg_ref, mug_ref[...])
    zswa = shifted8(zwa_ref[...], pwa_ref, muwa_ref[...])

    r = zs[:, 0:width]
    k = zs[:, width:2 * width]
    v = zs[:, 2 * width:3 * width]

    w_lin = w0_ref[...] + _dot(jnp.tanh(zswa).astype(BF16), w2_ref[...])
    a_lin = a0_ref[...] + _dot(zswa.astype(BF16), a2_ref[...])
    y = -w_lin
    softplus = jnp.maximum(y, 0.0) + jnp.log(1.0 + jnp.exp(-jnp.abs(y)))
    ld = -jnp.exp(-softplus - 0.5)
    a = _sigmoid(a_lin)
    g = _dot(_sigmoid(zsg).astype(BF16), g2_ref[...])

    eh = eh_ref[...]
    eht = eht_ref[...]

    def sum_heads(t):
        return _split_dot(t, eh, 2)

    def bcast_heads(t):
        return _split_dot(t, eht, 2)

    kkr = k * kk_ref[...]
    nrm = jnp.maximum(jnp.sqrt(sum_heads(kkr * kkr)), 1e-12)
    kkn = kkr * bcast_heads(1.0 / nrm)
    k2 = k * (1.0 + (a - 1.0) * ka_ref[...])
    av = -kkn
    bv = kkn * a
    bonus = bcast_heads(sum_heads(r * k2 * rk_ref[...])) * v

    lg = _split_dot(ld, tri_ref[...], 3, mode="ex")
    lg_last = lg[L - 1:L, :]
    eg = jnp.exp(lg)
    eng = jnp.exp(-lg)
    egl = jnp.exp(lg_last - lg)
    rt = r * eg
    kt = k2 * eng
    bt = bv * eng
    at = av * jnp.exp(lg - ld)
    kgl = k2 * egl
    bgl = bv * egl
    gl = jnp.exp(lg_last)

    lane = lax.broadcasted_iota(jnp.int32, (L, LANE), 1)
    lo = lane < R_HEAD

    def stack(t):
        return jnp.concatenate([jnp.where(lo, t, 0.0), jnp.where(lo, 0.0, t)], axis=0)

    row = lax.broadcasted_iota(jnp.int32, (2 * L, 2 * L), 0)
    col = lax.broadcasted_iota(jnp.int32, (2 * L, 2 * L), 1)
    strict = row > col
    incl = row >= col
    eye = (row == col).astype(F32)
    nsq = int(math.log2(L)) - 1

    pairs = range(npair)
    sls = [slice(p * LANE, (p + 1) * LANE) for p in pairs]
    bf = lambda t: t.astype(BF16)
    ar = [bf(jnp.concatenate([stack(at[:, sl]), stack(rt[:, sl])], axis=0)) for sl in sls]
    bk = [bf(jnp.concatenate([stack(bt[:, sl]), stack(kt[:, sl])], axis=0)) for sl in sls]
    v_s = [bf(stack(v[:, sl])) for sl in sls]
    gm = [_dot_nt(ar[p], bk[p]) for p in pairs]
    a_ab = [jnp.where(strict, gm[p][0:2 * L, 0:2 * L], 0.0) for p in pairs]
    a_ak = [bf(jnp.where(strict, gm[p][0:2 * L, 2 * L:4 * L], 0.0)) for p in pairs]
    a_rb = [bf(jnp.where(incl, gm[p][2 * L:4 * L, 0:2 * L], 0.0)) for p in pairs]
    a_rk = [bf(jnp.where(incl, gm[p][2 * L:4 * L, 2 * L:4 * L], 0.0)) for p in pairs]
    tinv = [eye + a_ab[p] for p in pairs]
    pw = a_ab
    for _ in range(nsq):
        pwb = [bf(pw[p]) for p in pairs]
        pw = [_dot(pwb[p], pwb[p]) for p in pairs]
        tinv = [tinv[p] + _dot(bf(pw[p]), bf(tinv[p])) for p in pairs]
    h = [st_ref[p] for p in pairs]
    hb = [bf(h[p]) for p in pairs]
    wmat = [_dot(jnp.concatenate([ar[p][0:2 * L], a_ak[p]], axis=1),
                 jnp.concatenate([hb[p], v_s[p]], axis=0)) for p in pairs]
    u = [_dot(bf(tinv[p]), bf(wmat[p])) for p in pairs]
    ub = [bf(u[p]) for p in pairs]
    yy = [_dot(jnp.concatenate([ar[p][2 * L:4 * L], a_rb[p], a_rk[p]], axis=1),
               jnp.concatenate([hb[p], ub[p], v_s[p]], axis=0)) for p in pairs]
    for p in pairs:
        sl = sls[p]
        bkg = jnp.concatenate([stack(bgl[:, sl]), stack(kgl[:, sl])], axis=0)
        glcol = jnp.sum(eye * gl[:, sl], axis=1, keepdims=True)
        st_ref[p] = h[p] * glcol + _dot(bf(bkg.T), jnp.concatenate([ub[p], v_s[p]], axis=0))

    yv = jnp.concatenate([yy[p][0:L] + yy[p][L:2 * L] for p in pairs], axis=1)
    inv_n = 1.0 / R_HEAD
    mean = bcast_heads(sum_heads(yv) * inv_n)
    yc = yv - mean
    var = bcast_heads(sum_heads(yc * yc) * inv_n)
    yn = yc * lax.rsqrt(var + LNX_EPS) * lnw_ref[...] + lnb_ref[...]
    o_ref[...] = ((yn + bonus) * g).astype(o_ref.dtype)


def _rwkv(z, mu_r, mu_g, mu_wa, w0, w2p, a0, a2p, g2p, kk, ka, rk, lnw, lnb, *, batch, seq, width):
    L = 64
    nt = seq // L
    nheads = width // R_HEAD
    eh = np.zeros((width, LANE), np.float32)
    eh[np.arange(width), np.arange(width) // R_HEAD] = 1.0
    tri = np.tril(np.ones((L, L), np.float32))
    vec = lambda c: pl.BlockSpec((1, c), lambda b, i: (0, 0))
    full = lambda a: pl.BlockSpec(a.shape, lambda b, i: (0,) * a.ndim)
    eh_j = jnp.asarray(eh, BF16)
    eht_j = jnp.asarray(eh.T, BF16)
    tri_j = jnp.asarray(tri, BF16)
    return pl.pallas_call(
        functools.partial(_rwkv_kernel, L=L, width=width),
        out_shape=jax.ShapeDtypeStruct((batch * seq, width), F32),
        grid=(batch, nt),
        in_specs=[
            pl.BlockSpec((L, 3 * width), lambda b, i: (b * nt + i, C_RKV // (3 * width))),
            pl.BlockSpec((L, 256), lambda b, i: (b * nt + i, C_ZG // 256)),
            pl.BlockSpec((L, LANE), lambda b, i: (b * nt + i, C_ZWA // LANE)),
            vec(3 * width), vec(256), vec(LANE),
            vec(width), full(w2p), vec(width), full(a2p), full(g2p),
            vec(width), vec(width), vec(width), vec(width), vec(width),
            full(eh_j), full(eht_j), full(tri_j),
        ],
        out_specs=pl.BlockSpec((L, width), lambda b, i: (b * nt + i, 0)),
        scratch_shapes=[
            pltpu.VMEM((8, 3 * width), F32),
            pltpu.VMEM((8, 256), F32),
            pltpu.VMEM((8, LANE), F32),
            pltpu.VMEM((nheads // 2, 2 * R_HEAD, 2 * R_HEAD), F32),
        ],
        compiler_params=_cparams(("arbitrary", "arbitrary")),
        name="rwkv",
    )(z, z, z, mu_r, mu_g, mu_wa, w0, w2p, a0, a2p, g2p, kk, ka, rk, lnw, lnb, eh_j, eht_j, tri_j)


def _nsa_prep_kernel(zq_ref, zs_ref, zw_ref, qg_ref, kgs_ref, kgw_ref, ehq_ref, ehqt_ref,
                     q_ref, ks_ref, vs_ref, kw_ref, vw_ref):
    inv_d = 1.0 / HEAD_DIM

    def head_norm(x, gain, e, et):
        ss = _split_dot(x * x, e, 2)
        inv = lax.rsqrt(ss * inv_d + QK_EPS)
        return x * _split_dot(inv, et, 2) * gain

    nq = zq_ref.shape[1]
    nk = N_KV * HEAD_DIM
    ehq = ehq_ref[...]
    ehqt = ehqt_ref[...]
    q = head_norm(zq_ref[...], qg_ref[...], ehq, ehqt) * (HEAD_DIM ** -0.5 * LOG2E)
    for h in range(nq // HEAD_DIM):
        q_ref[0, h] = q[:, h * HEAD_DIM:(h + 1) * HEAD_DIM].astype(q_ref.dtype)
    zs = zs_ref[...]
    zw = zw_ref[...]
    ehk = ehq[0:nk, :]
    ehkt = ehqt[:, 0:nk]
    ksn = head_norm(zs[:, 0:nk], kgs_ref[...], ehk, ehkt)
    kwn = head_norm(zw[:, 0:nk], kgw_ref[...], ehk, ehkt)
    vst = zs[:, nk:2 * nk].T
    vwt = zw[:, nk:2 * nk].T
    for gq in range(N_KV):
        sl = slice(gq * HEAD_DIM, (gq + 1) * HEAD_DIM)
        ks_ref[0, gq] = ksn[:, sl].astype(ks_ref.dtype)
        vs_ref[0, gq] = vst[sl, :].astype(vs_ref.dtype)
        kw_ref[0, gq] = kwn[:, sl].astype(kw_ref.dtype)
        vw_ref[0, gq] = vwt[sl, :].astype(vw_ref.dtype)


def _nsa_prep(z, q_gain_t, kgs_t, kgw_t, *, batch, seq):
    tb = 256
    nt = seq // tb
    nq = N_HEADS * HEAD_DIM
    nk = N_KV * HEAD_DIM
    eh = np.zeros((nq, LANE), np.float32)
    eh[np.arange(nq), np.arange(nq) // HEAD_DIM] = 1.0
    eh_j = jnp.asarray(eh, BF16)
    eht_j = jnp.asarray(eh.T, BF16)
    k_shape = jax.ShapeDtypeStruct((batch, N_KV, seq, HEAD_DIM), BF16)
    k_spec = pl.BlockSpec((1, N_KV, tb, HEAD_DIM), lambda b, i: (b, 0, i, 0))
    v_shape = jax.ShapeDtypeStruct((batch, N_KV, HEAD_DIM, seq), BF16)
    v_spec = pl.BlockSpec((1, N_KV, HEAD_DIM, tb), lambda b, i: (b, 0, 0, i))
    return pl.pallas_call(
        _nsa_prep_kernel,
        out_shape=(jax.ShapeDtypeStruct((batch, N_HEADS, seq, HEAD_DIM), BF16),
                   k_shape, v_shape, k_shape, v_shape),
        grid=(batch, nt),
        in_specs=[
            pl.BlockSpec((tb, nq), lambda b, i: (b * nt + i, C_Q // nq)),
            pl.BlockSpec((tb, 2 * nk), lambda b, i: (b * nt + i, C_KC // (2 * nk) + 1)),
            pl.BlockSpec((tb, 2 * nk), lambda b, i: (b * nt + i, C_KC // (2 * nk) + 2)),
            pl.BlockSpec((1, nq), lambda b, i: (0, 0)),
            pl.BlockSpec((1, nk), lambda b, i: (0, 0)),
            pl.BlockSpec((1, nk), lambda b, i: (0, 0)),
            pl.BlockSpec(eh_j.shape, lambda b, i: (0, 0)),
            pl.BlockSpec(eht_j.shape, lambda b, i: (0, 0)),
        ],
        out_specs=(pl.BlockSpec((1, N_HEADS, tb, HEAD_DIM), lambda b, i: (b, 0, i, 0)),
                   k_spec, v_spec, k_spec, v_spec),
        compiler_params=_cparams(("arbitrary", "arbitrary")),
        name="nsa_prep",
    )(z, z, z, q_gain_t, kgs_t, kgw_t, eh_j, eht_j)


def _nsa_cmp_kernel(ak_ref, av_ref, pek_ref, pev_ref, w1k_ref, w2k_ref, w1v_ref, w2v_ref, kg_ref,
                    kc_ref, vc_ref, *, n_cmp):
    ncp = kc_ref.shape[2]
    half = w1k_ref.shape[0] // 2

    def compress(a_ref, pe_ref, w1_ref, w2_ref):
        rows = a_ref.shape[2]
        a = a_ref[0, 0].astype(BF16)
        p1 = _dot(a, w1_ref[0:half, :])
        p2 = _dot(a, w1_ref[half:2 * half, :])
        pe8 = jnp.broadcast_to(pe_ref[...], (8, pe_ref.shape[1])).astype(BF16)
        pe_bias = _dot(pe8, w1_ref[...])[0:1, :]
        hpre = p1 + pltpu.roll(p2, rows - 1, axis=0) + pe_bias
        hid = 0.5 * hpre * (1.0 + jnp.tanh(math.sqrt(2.0 / math.pi) * (hpre + 0.044715 * hpre * hpre * hpre)))
        out = _dot(hid.astype(BF16), w2_ref[...])
        if rows < ncp:
            out = jnp.concatenate([out, jnp.zeros((ncp - rows, out.shape[1]), F32)], axis=0)
        return out

    valid = lax.broadcasted_iota(jnp.int32, (ncp, HEAD_DIM), 0) < n_cmp
    kc = compress(ak_ref, pek_ref, w1k_ref, w2k_ref)
    kc = _rms(kc, QK_EPS) * kg_ref[...]
    kc_ref[0, 0] = jnp.where(valid, kc, 0.0).astype(kc_ref.dtype)
    vc = jnp.where(valid, compress(av_ref, pev_ref, w1v_ref, w2v_ref), 0.0)
    vct = jnp.concatenate([vc, jnp.zeros((ncp, LANE - HEAD_DIM), F32)], axis=1).T
    vc_ref[0, 0] = vct[0:HEAD_DIM, :].astype(vc_ref.dtype)


def _nsa_cmp(ak, av, pek, pev, w1k, w2k, w1v, w2v, kg, *, ncp, n_cmp):
    batch, ng, rows, wid = ak.shape
    a_spec = pl.BlockSpec((1, 1, rows, wid), lambda b, g: (b, g, 0, 0))
    full = lambda a: pl.BlockSpec(a.shape, lambda b, g: (0,) * a.ndim)
    k_shape = jax.ShapeDtypeStruct((batch, ng, ncp, HEAD_DIM), BF16)
    k_spec = pl.BlockSpec((1, 1, ncp, HEAD_DIM), lambda b, g: (b, g, 0, 0))
    v_shape = jax.ShapeDtypeStruct((batch, ng, HEAD_DIM, ncp), BF16)
    v_spec = pl.BlockSpec((1, 1, HEAD_DIM, ncp), lambda b, g: (b, g, 0, 0))
    return pl.pallas_call(
        functools.partial(_nsa_cmp_kernel, n_cmp=n_cmp),
        out_shape=(k_shape, v_shape),
        grid=(batch, ng),
        in_specs=[a_spec, a_spec, full(pek), full(pev), full(w1k), full(w2k), full(w1v), full(w2v), full(kg)],
        out_specs=(k_spec, v_spec),
        compiler_params=_cparams(("arbitrary", "arbitrary")),
        name="nsa_cmp",
    )(ak, av, pek, pev, w1k, w2k, w1v, w2v, kg)


def _bucket_thresholds():
    exact = REL_BUCKETS // 2
    n = np.arange(0, REL_MAX_DIST + 1)
    nf = np.maximum(n, 1).astype(np.float64)
    large = exact + (np.log(nf / exact) / math.log(REL_MAX_DIST / exact) * (REL_BUCKETS - exact)).astype(np.int64)
    large = np.minimum(large, REL_BUCKETS - 1)
    bucket = np.where(n < exact, n, large)
    assert bucket[-1] == REL_BUCKETS - 1
    return [int(np.argmax(bucket >= b)) for b in range(REL_BUCKETS)]


def _nsa_bias_kernel(rel_ref, bw_ref, pat_ref, *, tq, ncp):
    h = pl.program_id(0)
    thr = _bucket_thresholds()

    def bias_of(d):
        val = jnp.full(d.shape, rel_ref[0, h], F32)
        for b in range(1, REL_BUCKETS):
            val = jnp.where(d >= thr[b], rel_ref[b, h], val)
        return jnp.where(d < 0, NEG, val * LOG2E)

    kk = lax.broadcasted_iota(jnp.int32, (tq, tq), 0)
    qq = lax.broadcasted_iota(jnp.int32, (tq, tq), 1)
    bw_ref[0, 0] = bias_of(qq - kk)
    bw_ref[0, 1] = bias_of(qq - kk + tq)
    cc = lax.broadcasted_iota(jnp.int32, (2 * ncp, tq), 0)
    qc = lax.broadcasted_iota(jnp.int32, (2 * ncp, tq), 1)
    pat_ref[0] = bias_of(qc - CMP_STRIDE * (cc - ncp) - (CMP_BLOCK - 1))


def _nsa_bias(rel, *, tq, ncp):
    return pl.pallas_call(
        functools.partial(_nsa_bias_kernel, tq=tq, ncp=ncp),
        out_shape=(jax.ShapeDtypeStruct((N_HEADS, 2, tq, tq), F32),
                   jax.ShapeDtypeStruct((N_HEADS, 2 * ncp, tq), F32)),
        grid=(N_HEADS,),
        in_specs=[pl.BlockSpec(memory_space=pltpu.SMEM)],
        out_specs=(pl.BlockSpec((1, 2, tq, tq), lambda h: (h, 0, 0, 0)),
                   pl.BlockSpec((1, 2 * ncp, tq), lambda h: (h, 0, 0))),
        compiler_params=_cparams(("arbitrary",)),
        name="nsa_bias",
    )(rel)


def _nsa_attn_kernel(rel_ref, q_ref, kc_ref, vc_ref, ks_ref, vs_ref, kw_ref, vw_ref, zg_ref, pat_ref,
                     bw_ref, msel_ref, ex_ref, o_ref, sbuf_ref, *, tq, ncp, nb):
    g = pl.program_id(1)
    qi = pl.program_id(2)
    cols = N_HG * tq
    q = q_ref[0].reshape(cols, HEAD_DIM)

    def heads(fn):
        return jnp.concatenate([fn(h) for h in range(N_HG)], axis=1)

    s_c = _dot_nt(kc_ref[0, 0], q)
    start = pl.multiple_of(ncp - qi * (tq // CMP_STRIDE), tq // CMP_STRIDE)
    bias_c = heads(lambda h: pat_ref[h, pl.ds(start, ncp), :])
    valid_c = bias_c > 0.5 * NEG
    sc = s_c + bias_c
    m_c = jnp.max(sc, axis=0, keepdims=True)
    e_c = jnp.where(valid_c, jnp.exp2(sc - m_c), 0.0)
    l_c = jnp.sum(e_c, axis=0, keepdims=True)
    p_c = e_c / jnp.where(l_c > 0.0, l_c, 1.0)
    o_c = _dot(vc_ref[0, 0], p_c.astype(BF16))

    psum = p_c[:, 0:tq]
    for h in range(1, N_HG):
        psum = psum + p_c[:, h * tq:(h + 1) * tq]
    imp_t = _split_dot(psum, msel_ref[...], 3, mode="ex")
    jj = lax.broadcasted_iota(jnp.int32, (nb, tq), 0)
    tt = qi * tq + lax.broadcasted_iota(jnp.int32, (nb, tq), 1)
    cur = tt // SEL_BLOCK
    forced = (jj == 0) | (jj == cur) | (jj == cur - 1)
    valid_b = jj * SEL_BLOCK <= tt
    imp = jnp.where(valid_b, jnp.where(forced, FORCE_SCORE, imp_t), -jnp.inf)
    cnt = jnp.zeros((nb, tq), F32)
    for i in range(nb):
        ri = imp[i:i + 1, :]
        cnt = cnt + jnp.where(jj > i, jnp.where(ri >= imp, 1.0, 0.0), jnp.where(ri > imp, 1.0, 0.0))
    sel_t = jnp.where((cnt < float(min(SEL_TOPK, nb))) & valid_b, 0.0, NEG)
    nbp = ex_ref.shape[1]
    if nbp > nb:
        sel_t = jnp.concatenate([sel_t, jnp.zeros((nbp - nb, tq), F32)], axis=0)
    sel_pen = sel_t
    jjp = lax.broadcasted_iota(jnp.int32, (nbp, tq), 0)

    ccol = lax.broadcasted_iota(jnp.int32, (1, cols), 1) // tq
    cvec = jnp.zeros((1, cols), F32)
    for h in range(N_HG):
        cvec = jnp.where(ccol == h, rel_ref[REL_BUCKETS - 1, g * N_HG + h] * LOG2E, cvec)

    def bias_blk(which):
        return heads(lambda h: bw_ref[h, which])

    def flash(carry, s, vt):
        m, l, acc = carry
        m_new = jnp.maximum(m, jnp.max(s, axis=0, keepdims=True))
        alpha = jnp.exp2(m - m_new)
        p = jnp.exp2(s - m_new)
        l = alpha * l + jnp.sum(p, axis=0, keepdims=True)
        acc = alpha * acc + _dot(vt, p.astype(BF16))
        return m_new, l, acc

    init = (jnp.full((1, cols), NEG, F32), jnp.zeros((1, cols), F32), jnp.zeros((HEAD_DIM, cols), F32))

    def finish(carry):
        m, l, acc = carry
        return acc / l

    def pen(d):
        return jnp.where(qi >= d, 0.0, NEG)

    def sel_scores(off, nkeys, hi):
        kx = ks_ref[0, 0, pl.ds(off, nkeys), :]
        s = _dot_nt(kx, q)
        pen_hi = jnp.where(jjp >= hi, NEG, sel_pen).astype(BF16)
        spen = _dot(ex_ref[pl.ds(off, nkeys), :], pen_hi)
        return s + heads(lambda h: spen)

    nk = FAR_TILES * tq
    per_tile = tq // SEL_BLOCK
    n_far = jnp.maximum(qi - 1, 0)
    nblk = (n_far + FAR_TILES - 1) // FAR_TILES

    def far_start(j):
        return pl.multiple_of(jnp.maximum(n_far - FAR_TILES * (j + 1), 0) * tq, tq)

    def far_scores(j):
        return sel_scores(far_start(j), nk, (n_far - FAR_TILES * j) * per_tile)

    sbuf_ref[0] = far_scores(0)
    b1 = bias_blk(1) + pen(1)
    b0 = bias_blk(0)
    off1 = pl.multiple_of(jnp.maximum(qi - 1, 0) * tq, tq)
    off0 = pl.multiple_of(qi * tq, tq)
    hi_all = (qi + 1) * per_tile
    s = jnp.concatenate([sel_scores(off1, tq, hi_all) + (b1 - cvec),
                         sel_scores(off0, tq, hi_all) + (b0 - cvec)], axis=0)
    vx = jnp.concatenate([vs_ref[0, 0, :, pl.ds(off1, tq)], vs_ref[0, 0, :, pl.ds(off0, tq)]], axis=1)
    carry = flash(init, s, vx)

    def far_body(j, carry):
        slot = j % 2
        s = sbuf_ref[slot]
        sbuf_ref[1 - slot] = far_scores(jnp.minimum(j + 1, nblk - 1))
        return flash(carry, s, vs_ref[0, 0, :, pl.ds(far_start(j), nk)])

    o_s = finish(lax.fori_loop(0, nblk, far_body, carry))

    nd = WINDOW // tq
    krow = lax.broadcasted_iota(jnp.int32, (tq, cols), 0)
    qcol = lax.broadcasted_iota(jnp.int32, (tq, cols), 1) % tq
    s_parts, v_parts = [], []
    for d in range(nd, -1, -1):
        off = pl.multiple_of(jnp.maximum(qi - d, 0) * tq, tq)
        s = _dot_nt(kw_ref[0, 0, pl.ds(off, tq), :], q)
        if d == 0:
            s = s + b0
        elif d == 1:
            s = s + b1
        elif d == nd:
            s = jnp.where(krow > qcol, s + (cvec + pen(d)), NEG)
        else:
            s = s + (cvec + pen(d))
        s_parts.append(s)
        v_parts.append(vw_ref[0, 0, :, pl.ds(off, tq)])
    o_w = finish(flash(init, jnp.concatenate(s_parts, axis=0), jnp.concatenate(v_parts, axis=1)))

    zg = pltpu.roll(zg_ref[...], (LANE - 16 * g) % LANE, axis=1)
    gates = _sigmoid(zg).T
    out = (heads(lambda h: gates[h:h + 1, :]) * o_c + heads(lambda h: gates[4 + h:5 + h, :]) * o_s
           + heads(lambda h: gates[8 + h:9 + h, :]) * o_w)
    for hp in range(N_HG // 2):
        pair = jnp.concatenate([out[:, (2 * hp) * tq:(2 * hp + 1) * tq],
                                out[:, (2 * hp + 1) * tq:(2 * hp + 2) * tq]], axis=0)
        o_ref[:, hp * 2 * HEAD_DIM:(hp + 1) * 2 * HEAD_DIM] = pair.T.astype(o_ref.dtype)


def _nsa_attn(rel, qn, kc, vc, ks, vs, kw, vw, z, pat, bw, *, batch, seq, tq, ncp):
    nt = seq // tq
    nb = seq // SEL_BLOCK
    n_cmp = (seq - CMP_BLOCK) // CMP_STRIDE + 1
    ci = np.arange(ncp)[None, :] * CMP_STRIDE
    sj = np.arange(nb)[:, None] * SEL_BLOCK
    msel = ((ci <= sj + SEL_BLOCK - 1) & (ci + CMP_BLOCK - 1 >= sj) & (np.arange(ncp)[None, :] < n_cmp))
    msel_j = jnp.asarray(msel.astype(np.float32), BF16)
    nbp = -(-nb // LANE) * LANE
    ex = (np.arange(seq)[:, None] // SEL_BLOCK == np.arange(nbp)[None, :])
    ex_j = jnp.asarray(ex.astype(np.float32), BF16)
    k_spec = pl.BlockSpec((1, 1, seq, HEAD_DIM), lambda b, g, i: (b, g, 0, 0))
    v_spec = pl.BlockSpec((1, 1, HEAD_DIM, seq), lambda b, g, i: (b, g, 0, 0))
    kc_spec = pl.BlockSpec((1, 1, ncp, HEAD_DIM), lambda b, g, i: (b, g, 0, 0))
    vc_spec = pl.BlockSpec((1, 1, HEAD_DIM, ncp), lambda b, g, i: (b, g, 0, 0))
    width = N_HEADS * HEAD_DIM
    return pl.pallas_call(
        functools.partial(_nsa_attn_kernel, tq=tq, ncp=ncp, nb=nb),
        out_shape=jax.ShapeDtypeStruct((batch * seq, width), F32),
        grid=(batch, N_KV, nt),
        in_specs=[
            pl.BlockSpec(memory_space=pltpu.SMEM),
            pl.BlockSpec((1, N_HG, tq, HEAD_DIM), lambda b, g, i: (b, g, i, 0)),
            kc_spec, vc_spec, k_spec, v_spec, k_spec, v_spec,
            pl.BlockSpec((tq, LANE), lambda b, g, i: (b * nt + i, C_GATE // LANE)),
            pl.BlockSpec((N_HG, 2 * ncp, tq), lambda b, g, i: (g, 0, 0)),
            pl.BlockSpec((N_HG, 2, tq, tq), lambda b, g, i: (g, 0, 0, 0)),
            pl.BlockSpec(msel_j.shape, lambda b, g, i: (0, 0)),
            pl.BlockSpec(ex_j.shape, lambda b, g, i: (0, 0)),
        ],
        out_specs=pl.BlockSpec((tq, N_HG * HEAD_DIM), lambda b, g, i: (b * nt + i, g)),
        scratch_shapes=[pltpu.VMEM((2, FAR_TILES * tq, N_HG * tq), F32)],
        compiler_params=_cparams(("arbitrary", "arbitrary", "arbitrary")),
        name="nsa_attn",
    )(rel, qn, kc, vc, ks, vs, kw, vw, z, pat, bw, msel_j, ex_j)


def _merge_kernel(x_ref, yr_ref, yn_ref, zm_ref, wr_ref, wn_ref, wm_ref, o_ref):
    d = x_ref.shape[1]
    zm = zm_ref[...]
    br = _dot(yr_ref[...].astype(BF16), wr_ref[...])
    bn = _dot(yn_ref[...].astype(BF16), wn_ref[...])
    merged = _sigmoid(zm[:, 0:d]) * br + _sigmoid(zm[:, d:2 * d]) * bn
    o_ref[...] = x_ref[...] + _dot(merged.astype(BF16), wm_ref[...])


def _merge(x, yr, yn, z, wr, wn, wm, *, tm):
    m, d = x.shape
    row = lambda c: pl.BlockSpec((tm, c), lambda i: (i, 0))
    full = lambda a: pl.BlockSpec(a.shape, lambda i: (0, 0))
    return pl.pallas_call(
        _merge_kernel,
        out_shape=jax.ShapeDtypeStruct((m, d), F32),
        grid=(m // tm,),
        in_specs=[row(d), row(d), row(d), pl.BlockSpec((tm, 2 * d), lambda i: (i, C_M // (2 * d))),
                  full(wr), full(wn), full(wm)],
        out_specs=row(d),
        compiler_params=_cparams(("arbitrary",)),
        name="merge",
    )(x, yr, yn, z, wr, wn, wm)


def _ca_kv_kernel(mem_ref, g_ref, w_ref, kg_ref, k_ref, v_ref):
    d = mem_ref.shape[2]
    dh = d // CA_HEADS
    mn = (_rms(mem_ref[0], NORM_EPS) * g_ref[...]).astype(BF16)
    kv = _dot(mn, w_ref[...])
    ks = [(_rms(kv[:, h * dh:(h + 1) * dh], QK_EPS) * kg_ref[...]) for h in range(CA_HEADS)]
    k_ref[0] = jnp.concatenate(ks, axis=1).astype(k_ref.dtype)
    v_ref[0] = kv[:, d:2 * d].astype(v_ref.dtype)


def _ca_kv(mem, g, wkv, kg):
    batch, nm, d = mem.shape
    o_shape = jax.ShapeDtypeStruct((batch, nm, d), BF16)
    o_spec = pl.BlockSpec((1, nm, d), lambda b: (b, 0, 0))
    full = lambda a: pl.BlockSpec(a.shape, lambda b: (0, 0))
    return pl.pallas_call(
        _ca_kv_kernel,
        out_shape=(o_shape, o_shape),
        grid=(batch,),
        in_specs=[o_spec, full(g), full(wkv), full(kg)],
        out_specs=(o_spec, o_spec),
        compiler_params=_cparams(("arbitrary",)),
        name="ca_kv",
    )(mem, g, wkv, kg)


def _cross_kernel(x_ref, g_ref, wq_ref, qg_ref, k_ref, v_ref, wo_ref, o_ref):
    d = x_ref.shape[1]
    dh = d // CA_HEADS
    x = x_ref[...]
    xn = (_rms(x, NORM_EPS) * g_ref[...]).astype(BF16)
    qf = _dot(xn, wq_ref[...])
    k = k_ref[0]
    v = v_ref[0]
    outs = []
    for h in range(CA_HEADS):
        sl = slice(h * dh, (h + 1) * dh)
        qh = (_rms(qf[:, sl], QK_EPS) * qg_ref[...] * (dh ** -0.5)).astype(BF16)
        s = _dot_nt(qh, k[:, sl])
        s = s - jnp.max(s, axis=-1, keepdims=True)
        e = jnp.exp(s)
        p = e / jnp.sum(e, axis=-1, keepdims=True)
        outs.append(_dot(p.astype(BF16), v[:, sl]))
    o = jnp.concatenate(outs, axis=1).astype(BF16)
    o_ref[...] = x + _dot(o, wo_ref[...])


def _cross(h1, g, wq, qg, kn, vv, wo, *, batch, seq, tm):
    m, d = h1.shape
    nt = seq // tm
    nm = kn.shape[1]
    row = pl.BlockSpec((tm, d), lambda b, i: (b * nt + i, 0))
    full = lambda a: pl.BlockSpec(a.shape, lambda b, i: (0, 0))
    kv_spec = pl.BlockSpec((1, nm, d), lambda b, i: (b, 0, 0))
    return pl.pallas_call(
        _cross_kernel,
        out_shape=jax.ShapeDtypeStruct((m, d), F32),
        grid=(batch, nt),
        in_specs=[row, full(g), full(wq), full(qg), kv_spec, kv_spec, full(wo)],
        out_specs=row,
        compiler_params=_cparams(("arbitrary", "arbitrary")),
        name="cross",
    )(h1, g, wq, qg, kn, vv, wo)


def _ffn_kernel(x_ref, g_ref, wa_ref, wb_ref, cw_ref, cb_ref, wd_ref, o_ref, xn_ref, acc_ref, carry_ref,
                *, tiles_per_seq):
    i = pl.program_id(0)
    j = pl.program_id(1)
    nj = pl.num_programs(1)
    tm = x_ref.shape[0]

    @pl.when(j == 0)
    def _():
        xn_ref[...] = (_rms(x_ref[...], NORM_EPS) * g_ref[...]).astype(BF16)
        acc_ref[...] = jnp.zeros_like(acc_ref)

    @pl.when(i % tiles_per_seq == 0)
    def _():
        carry_ref[j] = jnp.zeros(carry_ref.shape[1:], F32)

    xn = xn_ref[...]
    a = _dot(xn, wa_ref[...])
    b = _dot(xn, wb_ref[...])
    car = carry_ref[j]
    rowi = lax.broadcasted_iota(jnp.int32, a.shape, 0)
    p1 = jnp.where(rowi == 0, car[7:8, :], pltpu.roll(a, 1, axis=0))
    p2 = jnp.where(rowi == 0, car[6:7, :], jnp.where(rowi == 1, car[7:8, :], pltpu.roll(a, 2, axis=0)))
    carry_ref[j] = a[tm - 8:tm, :]
    cw = cw_ref[0]
    conv = cw[0:1, :] * p2 + cw[1:2, :] * p1 + cw[2:3, :] * a + cb_ref[0]
    act = conv * _sigmoid(conv) * b
    acc_ref[...] += _dot(act.astype(BF16), wd_ref[...])

    @pl.when(j == nj - 1)
    def _():
        o_ref[...] = x_ref[...] + acc_ref[...]


def _ffn(h2, g, wup, cw, cb, wd, *, seq, tm, tn):
    m, d = h2.shape
    dff = wd.shape[0]
    nj = dff // tn
    cw3 = jnp.zeros((nj, 8, tn), F32).at[:, 0:CONV_W, :].set(cw.reshape(CONV_W, nj, tn).transpose(1, 0, 2))
    cb3 = cb.reshape(nj, 1, tn)
    return pl.pallas_call(
        functools.partial(_ffn_kernel, tiles_per_seq=seq // tm),
        out_shape=jax.ShapeDtypeStruct((m, d), F32),
        grid=(m // tm, nj),
        in_specs=[
            pl.BlockSpec((tm, d), lambda i, j: (i, 0)),
            pl.BlockSpec((1, d), lambda i, j: (0, 0)),
            pl.BlockSpec((d, tn), lambda i, j: (0, j)),
            pl.BlockSpec((d, tn), lambda i, j: (0, nj + j)),
            pl.BlockSpec((1, 8, tn), lambda i, j: (j, 0, 0)),
            pl.BlockSpec((1, 1, tn), lambda i, j: (j, 0, 0)),
            pl.BlockSpec((tn, d), lambda i, j: (j, 0)),
        ],
        out_specs=pl.BlockSpec((tm, d), lambda i, j: (i, 0)),
        scratch_shapes=[pltpu.VMEM((tm, d), BF16), pltpu.VMEM((tm, d), F32), pltpu.VMEM((nj, 8, tn), F32)],
        compiler_params=_cparams(("arbitrary", "arbitrary")),
        name="ffn",
    )(h2, g, wup, wup, cw3, cb3, wd)


def _pack_perm(width):
    o_zw = 3 * width
    o_za = o_zw + R_LORA_W
    o_zg = o_za + R_LORA_A
    o_q = o_zg + R_LORA_G
    nkv = N_KV * HEAD_DIM
    o_kc = o_q + N_HEADS * HEAD_DIM
    o_gate = o_kc + 6 * nkv
    o_m = o_gate + 3 * N_HEADS
    perm = np.full((Z_COLS,), -1, np.int64)
    perm[C_RKV:C_RKV + 3 * width] = np.arange(3 * width)
    perm[C_Q:C_Q + N_HEADS * HEAD_DIM] = o_q + np.arange(N_HEADS * HEAD_DIM)
    perm[C_KC:C_KC + 6 * nkv] = o_kc + np.arange(6 * nkv)
    perm[C_ZG:C_ZG + R_LORA_G] = o_zg + np.arange(R_LORA_G)
    perm[C_ZWA:C_ZWA + R_LORA_W + R_LORA_A] = o_zw + np.arange(R_LORA_W + R_LORA_A)
    for g in range(N_KV):
        for c in range(3):
            for h in range(N_HG):
                perm[C_GATE + 16 * g + 4 * c + h] = o_gate + (g * N_HG + h) * 3 + c
    perm[C_M:C_M + 2 * width] = o_m + np.arange(2 * width)
    return perm, o_m + 2 * width


def _pad_rows(w, rows, offset=0):
    out = jnp.zeros((rows, w.shape[1]), w.dtype)
    return out.at[offset:offset + w.shape[0]].set(w)


def kernel(x, mem, rel_bias, norm_mix, w_in, rwkv_mu, rwkv_w0, rwkv_w2, rwkv_a0, rwkv_a2, rwkv_g2,
           rwkv_kk, rwkv_ka, rwkv_rk, rwkv_lnx_w, rwkv_lnx_b, nsa_q_gain, nsa_k_gain, cmp_pe_k, cmp_pe_v,
           cmp_w1_k, cmp_w2_k, cmp_w1_v, cmp_w2_v, w_branch_rwkv, w_branch_nsa, w_mix_out,
           norm_cross, norm_mem, ca_wq, ca_wkv, ca_q_gain, ca_k_gain, ca_wo,
           norm_ffn, ffn_up, ffn_conv, ffn_conv_b, ffn_down):
    batch, seq, d = x.shape
    depth = w_in.shape[0]
    width = d
    assert d == N_HEADS * HEAD_DIM and 2 * (Z_COLS - C_M) == 4 * d
    perm, in_cols = _pack_perm(width)
    assert w_in.shape[2] == in_cols
    perm_j = jnp.asarray(np.maximum(perm, 0), jnp.int32)
    keep = jnp.asarray(perm >= 0)

    tq = 128
    assert seq % 256 == 0 and tq > REL_MAX_DIST - 1 and WINDOW % tq == 0
    ncp = -(-(seq // CMP_STRIDE) // LANE) * LANE
    n_cmp = (seq - CMP_BLOCK) // CMP_STRIDE + 1
    tm = min(1024, seq)
    row = lambda v: v.reshape(1, -1).astype(F32)
    nkv = N_KV * HEAD_DIM

    h = x.reshape(batch * seq, d)
    for l in range(depth):
        w_in_p = jnp.where(keep[None, :], jnp.take(w_in[l], perm_j, axis=1), 0.0).astype(BF16)
        mu = rwkv_mu[l]
        mu_r = row(mu[0:3 * width])
        mu_wa = row(mu[3 * width:3 * width + R_LORA_W + R_LORA_A])
        mu_g = row(jnp.zeros((256,), F32).at[0:R_LORA_G].set(mu[3 * width + R_LORA_W + R_LORA_A:]))
        w2p = _pad_rows(rwkv_w2[l], LANE, 0).astype(BF16)
        a2p = _pad_rows(rwkv_a2[l], LANE, R_LORA_W).astype(BF16)
        g2p = _pad_rows(rwkv_g2[l], 256, 0).astype(BF16)

        z = _norm_matmul(h, row(norm_mix[l]), w_in_p, tm=tm, tn=1024, out_dtype=F32, name="in_proj")

        y_r = _rwkv(z, mu_r, mu_g, mu_wa, row(rwkv_w0[l]), w2p, row(rwkv_a0[l]), a2p, g2p,
                    row(rwkv_kk[l]), row(rwkv_ka[l]), row(rwkv_rk[l]), row(rwkv_lnx_w[l]),
                    row(rwkv_lnx_b[l]), batch=batch, seq=seq, width=width)

        qn, ks, vs, kw, vw = _nsa_prep(
            z, row(jnp.tile(nsa_q_gain[l], N_HEADS)), row(jnp.tile(nsa_k_gain[l, 1], N_KV)),
            row(jnp.tile(nsa_k_gain[l, 2], N_KV)), batch=batch, seq=seq)
        z3 = z.reshape(batch, seq // CMP_STRIDE, CMP_STRIDE, Z_COLS)

        def groups(c0):
            t = z3[:, :, :, c0:c0 + nkv].reshape(batch, seq // CMP_STRIDE, CMP_STRIDE, N_KV, HEAD_DIM)
            return t.transpose(0, 3, 1, 2, 4).reshape(batch, N_KV, seq // CMP_STRIDE, CMP_STRIDE * HEAD_DIM)

        kc, vc = _nsa_cmp(groups(C_KC), groups(C_KC + nkv), cmp_pe_k[l].reshape(1, -1), cmp_pe_v[l].reshape(1, -1),
                          cmp_w1_k[l].astype(BF16), cmp_w2_k[l].astype(BF16), cmp_w1_v[l].astype(BF16),
                          cmp_w2_v[l].astype(BF16), row(nsa_k_gain[l, 0]), ncp=ncp, n_cmp=n_cmp)
        bw, pat = _nsa_bias(rel_bias.astype(F32), tq=tq, ncp=ncp)
        y_n = _nsa_attn(rel_bias.astype(F32), qn, kc, vc, ks, vs, kw, vw, z, pat, bw,
                        batch=batch, seq=seq, tq=tq, ncp=ncp)

        h1 = _merge(h, y_r, y_n, z, w_branch_rwkv[l].astype(BF16), w_branch_nsa[l].astype(BF16),
                    w_mix_out[l].astype(BF16), tm=min(512, seq))

        kn, vv = _ca_kv(mem, row(norm_mem[l]), ca_wkv[l].astype(BF16), row(ca_k_gain[l]))
        h2 = _cross(h1, row(norm_cross[l]), ca_wq[l].astype(BF16), row(ca_q_gain[l]), kn, vv,
                    ca_wo[l].astype(BF16), batch=batch, seq=seq, tm=min(512, seq))

        h = _ffn(h2, row(norm_ffn[l]), ffn_up[l].astype(BF16), ffn_conv[l], ffn_conv_b[l],
                 ffn_down[l].astype(BF16), seq=seq, tm=tm, tn=256)
    return h.reshape(batch, seq, d)
```

```python
import functools
import math

import jax
import jax.numpy as jnp
import numpy as np
from jax import lax
from jax.experimental import pallas as pl
from jax.experimental.pallas import tpu as pltpu

F32 = jnp.float32
BF16 = jnp.bfloat16
NEG = -1e30
LOG2E = 1.4426950408889634
FAR_TILES = 4
V_ROWS = 80

R_HEAD = 64
R_LORA_W = 64
R_LORA_A = 64
R_LORA_G = 160
LNX_EPS = 64e-5
N_HEADS = 16
N_KV = 4
N_HG = N_HEADS // N_KV
HEAD_DIM = 64
CMP_BLOCK = 32
CMP_STRIDE = 16
CMP_HIDDEN = 256
SEL_BLOCK = 64
SEL_TOPK = 16
WINDOW = 512
FORCE_SCORE = 1e4
REL_BUCKETS = 32
REL_MAX_DIST = 128
CA_HEADS = 4
CONV_W = 3
NORM_EPS = 1e-6
QK_EPS = 1e-6

LANE = 128
VMEM_LIMIT = 56 * 1024 * 1024

C_RKV = 0
C_Q = 3072
C_KC = 4096
C_ZG = 5632
C_ZWA = 5888
C_GATE = 6016
C_M = 6144
Z_COLS = 8192


def _dot(a, b):
    return jnp.dot(a, b, preferred_element_type=F32)


def _dot_nt(a, b):
    return lax.dot_general(a, b, (((1,), (1,)), ((), ())), preferred_element_type=F32)


def _split_dot(x, e, parts, mode="xe"):
    acc = None
    rem = x
    for i in range(parts):
        hi = rem.astype(BF16)
        t = _dot(hi, e) if mode == "xe" else (_dot(e, hi) if mode == "ex" else _dot_nt(e, hi))
        acc = t if acc is None else acc + t
        if i + 1 < parts:
            rem = rem - hi.astype(F32)
    return acc


def _rms(x, eps):
    return x * lax.rsqrt(jnp.mean(x * x, axis=-1, keepdims=True) + eps)


def _sigmoid(x):
    return 1.0 / (1.0 + jnp.exp(-x))


def _cparams(sem):
    return pltpu.CompilerParams(dimension_semantics=sem, vmem_limit_bytes=VMEM_LIMIT)


def _norm_matmul_kernel(x_ref, g_ref, w_ref, o_ref, xn_ref):
    @pl.when(pl.program_id(1) == 0)
    def _():
        x = x_ref[...]
        xn_ref[...] = (_rms(x, NORM_EPS) * g_ref[...]).astype(BF16)

    o_ref[...] = _dot(xn_ref[...], w_ref[...]).astype(o_ref.dtype)


def _norm_matmul(x, g, w, *, tm, tn, out_dtype, name):
    m, k = x.shape
    n = w.shape[1]
    return pl.pallas_call(
        _norm_matmul_kernel,
        out_shape=jax.ShapeDtypeStruct((m, n), out_dtype),
        grid=(m // tm, n // tn),
        in_specs=[
            pl.BlockSpec((tm, k), lambda i, j: (i, 0)),
            pl.BlockSpec((1, k), lambda i, j: (0, 0)),
            pl.BlockSpec((k, tn), lambda i, j: (0, j)),
        ],
        out_specs=pl.BlockSpec((tm, tn), lambda i, j: (i, j)),
        scratch_shapes=[pltpu.VMEM((tm, k), BF16)],
        compiler_params=_cparams(("arbitrary", "arbitrary")),
        name=name,
    )(x, g, w)


def _rwkv_kernel(zr_ref, zg_ref, zwa_ref, mur_ref, mug_ref, muwa_ref, w0_ref, w2_ref, a0_ref, a2_ref,
                 g2_ref, kk_ref, ka_ref, rk_ref, lnw_ref, lnb_ref, eh_ref, eht_ref, tri_ref,
                 o_ref, pr_ref, pg_ref, pwa_ref, st_ref, *, L, width):
    i = pl.program_id(1)
    npair = width // LANE
    tb = o_ref.shape[0]

    @pl.when(i == 0)
    def _():
        pr_ref[...] = jnp.zeros_like(pr_ref)
        pg_ref[...] = jnp.zeros_like(pg_ref)
        pwa_ref[...] = jnp.zeros_like(pwa_ref)
        st_ref[...] = jnp.zeros_like(st_ref)

    def shifted8(z, prev_ref, mu):
        rolled = pltpu.roll(z, 1, axis=0)
        row0 = lax.broadcasted_iota(jnp.int32, z.shape, 0) == 0
        prev = jnp.where(row0, prev_ref[7:8, :], rolled)
        prev_ref[...] = z[tb - 8:tb, :]
        return z + (prev - z) * mu

    zs = shifted8(zr_ref[...].astype(F32), pr_ref, mur_ref[...])
    zsg = shifted8(zg_ref[...].astype(F32), pg_ref, mug_ref[...])
    zswa = shifted8(zwa_ref[...].astype(F32), pwa_ref, muwa_ref[...])

    r = zs[:, 0:width]
    k = zs[:, width:2 * width]
    v = zs[:, 2 * width:3 * width]

    w_lin = w0_ref[...] + _dot(jnp.tanh(zswa).astype(BF16), w2_ref[...])
    a_lin = a0_ref[...] + _dot(zswa.astype(BF16), a2_ref[...])
    y = -w_lin
    softplus = jnp.maximum(y, 0.0) + jnp.log(1.0 + jnp.exp(-jnp.abs(y)))
    ld = -jnp.exp(-softplus - 0.5)
    a = _sigmoid(a_lin)
    g = _dot(_sigmoid(zsg).astype(BF16), g2_ref[...])

    eh = eh_ref[...]
    eht = eht_ref[...]

    def sum_heads(t):
        return _split_dot(t, eh, 2)

    def bcast_heads(t):
        return _split_dot(t, eht, 2)

    kkr = k * kk_ref[...]
    nrm = jnp.maximum(jnp.sqrt(sum_heads(kkr * kkr)), 1e-12)
    kkn = kkr * bcast_heads(1.0 / nrm)
    k2 = k * (1.0 + (a - 1.0) * ka_ref[...])
    av = -kkn
    bv = kkn * a
    bonus = bcast_heads(sum_heads(r * k2 * rk_ref[...])) * v

    nch = tb // L
    lg = _split_dot(ld, tri_ref[...], 3, mode="ex")
    lasts = [lg[(c + 1) * L - 1:(c + 1) * L, :] for c in range(nch)]
    lg_last = jnp.concatenate([jnp.broadcast_to(t, (L, width)) for t in lasts], axis=0)
    eg = jnp.exp(lg)
    eng = jnp.exp(-lg)
    egl = jnp.exp(lg_last - lg)
    rt = r * eg
    kt = k2 * eng
    bt = bv * eng
    at = av * jnp.exp(lg - ld)
    kgl = k2 * egl
    bgl = bv * egl
    gl = [jnp.exp(t) for t in lasts]

    lane = lax.broadcasted_iota(jnp.int32, (L, LANE), 1)
    lo = lane < R_HEAD

    def stack(t):
        return jnp.concatenate([jnp.where(lo, t, 0.0), jnp.where(lo, 0.0, t)], axis=0)

    row = lax.broadcasted_iota(jnp.int32, (2 * L, 2 * L), 0)
    col = lax.broadcasted_iota(jnp.int32, (2 * L, 2 * L), 1)
    strict = row > col
    incl = row >= col
    eye = (row == col).astype(F32)
    nsq = int(math.log2(L)) - 1

    pairs = range(npair)
    chains = [(c, p) for c in range(nch) for p in pairs]
    sub = lambda t, c, p: t[c * L:(c + 1) * L, p * LANE:(p + 1) * LANE]
    bf = lambda t: t.astype(BF16)
    ar = {cp: bf(jnp.concatenate([stack(sub(at, *cp)), stack(sub(rt, *cp))], axis=0)) for cp in chains}
    bk = {cp: bf(jnp.concatenate([stack(sub(bt, *cp)), stack(sub(kt, *cp))], axis=0)) for cp in chains}
    v_s = {cp: bf(stack(sub(v, *cp))) for cp in chains}
    gm = {cp: _dot_nt(ar[cp], bk[cp]) for cp in chains}
    a_ab = {cp: jnp.where(strict, gm[cp][0:2 * L, 0:2 * L], 0.0) for cp in chains}
    a_ak = {cp: bf(jnp.where(strict, gm[cp][0:2 * L, 2 * L:4 * L], 0.0)) for cp in chains}
    a_rb = {cp: bf(jnp.where(incl, gm[cp][2 * L:4 * L, 0:2 * L], 0.0)) for cp in chains}
    a_rk = {cp: bf(jnp.where(incl, gm[cp][2 * L:4 * L, 2 * L:4 * L], 0.0)) for cp in chains}
    tinv = {cp: eye + a_ab[cp] for cp in chains}
    pw = a_ab
    for _ in range(nsq):
        pwb = {cp: bf(pw[cp]) for cp in chains}
        pw = {cp: _dot(pwb[cp], pwb[cp]) for cp in chains}
        tinv = {cp: tinv[cp] + _dot(bf(pw[cp]), bf(tinv[cp])) for cp in chains}
    tinv = {cp: bf(tinv[cp]) for cp in chains}
    bkgt = {cp: bf(jnp.concatenate([stack(sub(bgl, *cp)), stack(sub(kgl, *cp))], axis=0).T) for cp in chains}
    glcol = {(c, p): jnp.sum(eye * gl[c][:, p * LANE:(p + 1) * LANE], axis=1, keepdims=True)
             for (c, p) in chains}
    h = [st_ref[p] for p in pairs]
    ys = []
    for c in range(nch):
        hb = [bf(h[p]) for p in pairs]
        wmat = [_dot(jnp.concatenate([ar[c, p][0:2 * L], a_ak[c, p]], axis=1),
                     jnp.concatenate([hb[p], v_s[c, p]], axis=0)) for p in pairs]
        ub = [bf(_dot(tinv[c, p], bf(wmat[p]))) for p in pairs]
        yy = [_dot(jnp.concatenate([ar[c, p][2 * L:4 * L], a_rb[c, p], a_rk[c, p]], axis=1),
                   jnp.concatenate([hb[p], ub[p], v_s[c, p]], axis=0)) for p in pairs]
        h = [h[p] * glcol[c, p] + _dot(bkgt[c, p], jnp.concatenate([ub[p], v_s[c, p]], axis=0))
             for p in pairs]
        ys.append(jnp.concatenate([yy[p][0:L] + yy[p][L:2 * L] for p in pairs], axis=1))
    for p in pairs:
        st_ref[p] = h[p]

    yv = jnp.concatenate(ys, axis=0)
    inv_n = 1.0 / R_HEAD
    mean = bcast_heads(sum_heads(yv) * inv_n)
    yc = yv - mean
    var = bcast_heads(sum_heads(yc * yc) * inv_n)
    yn = yc * lax.rsqrt(var + LNX_EPS) * lnw_ref[...] + lnb_ref[...]
    o_ref[...] = ((yn + bonus) * g).astype(o_ref.dtype)


def _rwkv(z, mu_r, mu_g, mu_wa, w0, w2p, a0, a2p, g2p, kk, ka, rk, lnw, lnb, *, batch, seq, width):
    L = 64
    tb = 2 * L
    nt = seq // tb
    nheads = width // R_HEAD
    eh = np.zeros((width, LANE), np.float32)
    eh[np.arange(width), np.arange(width) // R_HEAD] = 1.0
    ti = np.arange(tb)
    tri = ((ti[:, None] >= ti[None, :]) & (ti[:, None] // L == ti[None, :] // L)).astype(np.float32)
    vec = lambda c: pl.BlockSpec((1, c), lambda b, i: (0, 0))
    full = lambda a: pl.BlockSpec(a.shape, lambda b, i: (0,) * a.ndim)
    eh_j = jnp.asarray(eh, BF16)
    eht_j = jnp.asarray(eh.T, BF16)
    tri_j = jnp.asarray(tri, BF16)
    return pl.pallas_call(
        functools.partial(_rwkv_kernel, L=L, width=width),
        out_shape=jax.ShapeDtypeStruct((batch * seq, width), BF16),
        grid=(batch, nt),
        in_specs=[
            pl.BlockSpec((tb, 3 * width), lambda b, i: (b * nt + i, C_RKV // (3 * width))),
            pl.BlockSpec((tb, 256), lambda b, i: (b * nt + i, C_ZG // 256)),
            pl.BlockSpec((tb, LANE), lambda b, i: (b * nt + i, C_ZWA // LANE)),
            vec(3 * width), vec(256), vec(LANE),
            vec(width), full(w2p), vec(width), full(a2p), full(g2p),
            vec(width), vec(width), vec(width), vec(width), vec(width),
            full(eh_j), full(eht_j), full(tri_j),
        ],
        out_specs=pl.BlockSpec((tb, width), lambda b, i: (b * nt + i, 0)),
        scratch_shapes=[
            pltpu.VMEM((8, 3 * width), F32),
            pltpu.VMEM((8, 256), F32),
            pltpu.VMEM((8, LANE), F32),
            pltpu.VMEM((nheads // 2, 2 * R_HEAD, 2 * R_HEAD), F32),
        ],
        compiler_params=_cparams(("arbitrary", "arbitrary")),
        name="rwkv",
    )(z, z, z, mu_r, mu_g, mu_wa, w0, w2p, a0, a2p, g2p, kk, ka, rk, lnw, lnb, eh_j, eht_j, tri_j)


def _nsa_prep_kernel(zq_ref, zs_ref, zw_ref, qg_ref, kgs_ref, kgw_ref, ehq_ref, ehqt_ref,
                     q_ref, ks_ref, vs_ref, kw_ref, vw_ref):
    inv_d = 1.0 / HEAD_DIM

    def head_norm(x, gain, e, et):
        ss = _split_dot(x * x, e, 2)
        inv = lax.rsqrt(ss * inv_d + QK_EPS)
        return x * _split_dot(inv, et, 2) * gain

    nq = zq_ref.shape[1]
    nk = N_KV * HEAD_DIM
    ehq = ehq_ref[...]
    ehqt = ehqt_ref[...]
    q = head_norm(zq_ref[...].astype(F32), qg_ref[...], ehq, ehqt) * (HEAD_DIM ** -0.5 * LOG2E)
    for h in range(nq // HEAD_DIM):
        q_ref[0, h] = q[:, h * HEAD_DIM:(h + 1) * HEAD_DIM].astype(q_ref.dtype)
    zs = zs_ref[...].astype(F32)
    zw = zw_ref[...].astype(F32)
    ehk = ehq[0:nk, :]
    ehkt = ehqt[:, 0:nk]
    ksn = head_norm(zs[:, 0:nk], kgs_ref[...], ehk, ehkt)
    kwn = head_norm(zw[:, 0:nk], kgw_ref[...], ehk, ehkt)
    vst = zs[:, nk:2 * nk].T
    vwt = zw[:, nk:2 * nk].T
    tb = zs.shape[0]
    vextra = jnp.where(lax.broadcasted_iota(jnp.int32, (V_ROWS - HEAD_DIM, tb), 0) == 0, 1.0, 0.0)
    for gq in range(N_KV):
        sl = slice(gq * HEAD_DIM, (gq + 1) * HEAD_DIM)
        ks_ref[0, gq] = ksn[:, sl].astype(ks_ref.dtype)
        vs_ref[0, gq] = jnp.concatenate([vst[sl, :], vextra], axis=0).astype(vs_ref.dtype)
        kw_ref[0, gq] = kwn[:, sl].astype(kw_ref.dtype)
        vw_ref[0, gq] = jnp.concatenate([vwt[sl, :], vextra], axis=0).astype(vw_ref.dtype)


def _nsa_prep(z, q_gain_t, kgs_t, kgw_t, *, batch, seq):
    tb = 256
    nt = seq // tb
    nq = N_HEADS * HEAD_DIM
    nk = N_KV * HEAD_DIM
    eh = np.zeros((nq, LANE), np.float32)
    eh[np.arange(nq), np.arange(nq) // HEAD_DIM] = 1.0
    eh_j = jnp.asarray(eh, BF16)
    eht_j = jnp.asarray(eh.T, BF16)
    k_shape = jax.ShapeDtypeStruct((batch, N_KV, seq, HEAD_DIM), BF16)
    k_spec = pl.BlockSpec((1, N_KV, tb, HEAD_DIM), lambda b, i: (b, 0, i, 0))
    v_shape = jax.ShapeDtypeStruct((batch, N_KV, V_ROWS, seq), BF16)
    v_spec = pl.BlockSpec((1, N_KV, V_ROWS, tb), lambda b, i: (b, 0, 0, i))
    return pl.pallas_call(
        _nsa_prep_kernel,
        out_shape=(jax.ShapeDtypeStruct((batch, N_HEADS, seq, HEAD_DIM), BF16),
                   k_shape, v_shape, k_shape, v_shape),
        grid=(batch, nt),
        in_specs=[
            pl.BlockSpec((tb, nq), lambda b, i: (b * nt + i, C_Q // nq)),
            pl.BlockSpec((tb, 2 * nk), lambda b, i: (b * nt + i, C_KC // (2 * nk) + 1)),
            pl.BlockSpec((tb, 2 * nk), lambda b, i: (b * nt + i, C_KC // (2 * nk) + 2)),
            pl.BlockSpec((1, nq), lambda b, i: (0, 0)),
            pl.BlockSpec((1, nk), lambda b, i: (0, 0)),
            pl.BlockSpec((1, nk), lambda b, i: (0, 0)),
            pl.BlockSpec(eh_j.shape, lambda b, i: (0, 0)),
            pl.BlockSpec(eht_j.shape, lambda b, i: (0, 0)),
        ],
        out_specs=(pl.BlockSpec((1, N_HEADS, tb, HEAD_DIM), lambda b, i: (b, 0, i, 0)),
                   k_spec, v_spec, k_spec, v_spec),
        compiler_params=_cparams(("arbitrary", "arbitrary")),
        name="nsa_prep",
    )(z, z, z, q_gain_t, kgs_t, kgw_t, eh_j, eht_j)


def _nsa_cmp_kernel(ak_ref, av_ref, pek_ref, pev_ref, w1k_ref, w2k_ref, w1v_ref, w2v_ref, kg_ref,
                    kc_ref, vc_ref, *, n_cmp):
    ncp = kc_ref.shape[2]
    half = w1k_ref.shape[0] // 2

    def compress(a_ref, pe_ref, w1_ref, w2_ref):
        rows = a_ref.shape[2]
        a = a_ref[0, 0].astype(BF16)
        p1 = _dot(a, w1_ref[0:half, :])
        p2 = _dot(a, w1_ref[half:2 * half, :])
        pe8 = jnp.broadcast_to(pe_ref[...], (8, pe_ref.shape[1])).astype(BF16)
        pe_bias = _dot(pe8, w1_ref[...])[0:1, :]
        hpre = p1 + pltpu.roll(p2, rows - 1, axis=0) + pe_bias
        hid = 0.5 * hpre * (1.0 + jnp.tanh(math.sqrt(2.0 / math.pi) * (hpre + 0.044715 * hpre * hpre * hpre)))
        out = _dot(hid.astype(BF16), w2_ref[...])
        if rows < ncp:
            out = jnp.concatenate([out, jnp.zeros((ncp - rows, out.shape[1]), F32)], axis=0)
        return out

    valid = lax.broadcasted_iota(jnp.int32, (ncp, HEAD_DIM), 0) < n_cmp
    kc = compress(ak_ref, pek_ref, w1k_ref, w2k_ref)
    kc = _rms(kc, QK_EPS) * kg_ref[...]
    kc_ref[0, 0] = jnp.where(valid, kc, 0.0).astype(kc_ref.dtype)
    vc = jnp.where(valid, compress(av_ref, pev_ref, w1v_ref, w2v_ref), 0.0)
    vct = jnp.concatenate([vc, jnp.zeros((ncp, LANE - HEAD_DIM), F32)], axis=1).T
    vc_ref[0, 0] = vct[0:HEAD_DIM, :].astype(vc_ref.dtype)


def _nsa_cmp(ak, av, pek, pev, w1k, w2k, w1v, w2v, kg, *, ncp, n_cmp):
    batch, ng, rows, wid = ak.shape
    a_spec = pl.BlockSpec((1, 1, rows, wid), lambda b, g: (b, g, 0, 0))
    full = lambda a: pl.BlockSpec(a.shape, lambda b, g: (0,) * a.ndim)
    k_shape = jax.ShapeDtypeStruct((batch, ng, ncp, HEAD_DIM), BF16)
    k_spec = pl.BlockSpec((1, 1, ncp, HEAD_DIM), lambda b, g: (b, g, 0, 0))
    v_shape = jax.ShapeDtypeStruct((batch, ng, HEAD_DIM, ncp), BF16)
    v_spec = pl.BlockSpec((1, 1, HEAD_DIM, ncp), lambda b, g: (b, g, 0, 0))
    return pl.pallas_call(
        functools.partial(_nsa_cmp_kernel, n_cmp=n_cmp),
        out_shape=(k_shape, v_shape),
        grid=(batch, ng),
        in_specs=[a_spec, a_spec, full(pek), full(pev), full(w1k), full(w2k), full(w1v), full(w2v), full(kg)],
        out_specs=(k_spec, v_spec),
        compiler_params=_cparams(("arbitrary", "arbitrary")),
        name="nsa_cmp",
    )(ak, av, pek, pev, w1k, w2k, w1v, w2v, kg)


def _bucket_thresholds():
    exact = REL_BUCKETS // 2
    n = np.arange(0, REL_MAX_DIST + 1)
    nf = np.maximum(n, 1).astype(np.float64)
    large = exact + (np.log(nf / exact) / math.log(REL_MAX_DIST / exact) * (REL_BUCKETS - exact)).astype(np.int64)
    large = np.minimum(large, REL_BUCKETS - 1)
    bucket = np.where(n < exact, n, large)
    assert bucket[-1] == REL_BUCKETS - 1
    return [int(np.argmax(bucket >= b)) for b in range(REL_BUCKETS)]


def _nsa_bias_kernel(rel_ref, bw_ref, pat_ref, *, tq, ncp):
    h = pl.program_id(0)
    thr = _bucket_thresholds()

    def bias_of(d):
        val = jnp.full(d.shape, rel_ref[0, h], F32)
        for b in range(1, REL_BUCKETS):
            val = jnp.where(d >= thr[b], rel_ref[b, h], val)
        return jnp.where(d < 0, NEG, val * LOG2E)

    kk = lax.broadcasted_iota(jnp.int32, (tq, tq), 0)
    qq = lax.broadcasted_iota(jnp.int32, (tq, tq), 1)
    bw_ref[0, 0] = bias_of(qq - kk)
    bw_ref[0, 1] = bias_of(qq - kk + tq)
    cc = lax.broadcasted_iota(jnp.int32, (2 * ncp, tq), 0)
    qc = lax.broadcasted_iota(jnp.int32, (2 * ncp, tq), 1)
    pat_ref[0] = bias_of(qc - CMP_STRIDE * (cc - ncp) - (CMP_BLOCK - 1))


def _nsa_bias(rel, *, tq, ncp):
    return pl.pallas_call(
        functools.partial(_nsa_bias_kernel, tq=tq, ncp=ncp),
        out_shape=(jax.ShapeDtypeStruct((N_HEADS, 2, tq, tq), F32),
                   jax.ShapeDtypeStruct((N_HEADS, 2 * ncp, tq), F32)),
        grid=(N_HEADS,),
        in_specs=[pl.BlockSpec(memory_space=pltpu.SMEM)],
        out_specs=(pl.BlockSpec((1, 2, tq, tq), lambda h: (h, 0, 0, 0)),
                   pl.BlockSpec((1, 2 * ncp, tq), lambda h: (h, 0, 0))),
        compiler_params=_cparams(("arbitrary",)),
        name="nsa_bias",
    )(rel)


def _nsa_attn_kernel(rel_ref, q_ref, kc_ref, vc_ref, ks_ref, vs_ref, kw_ref, vw_ref, zg_ref, pat_ref,
                     bw_ref, msel_ref, o_ref, sbuf_ref, pen_ref, pbuf_ref, *, tq, ncp, nb):
    g = pl.program_id(1)
    qi = pl.program_id(2)
    cols = N_HG * tq
    q = q_ref[0].reshape(cols, HEAD_DIM)

    def heads(fn):
        return jnp.concatenate([fn(h) for h in range(N_HG)], axis=1)

    ccol = lax.broadcasted_iota(jnp.int32, (1, cols), 1) // tq
    cvec = jnp.zeros((1, cols), F32)
    for h in range(N_HG):
        cvec = jnp.where(ccol == h, rel_ref[REL_BUCKETS - 1, g * N_HG + h] * LOG2E, cvec)

    def bias_blk(which):
        return heads(lambda h: bw_ref[h, which])

    vrows = vs_ref.shape[2]

    def flash(carry, s, vt):
        m, acc = carry
        m_new = jnp.maximum(m, jnp.max(s, axis=0, keepdims=True))
        p = jnp.exp2(s - m_new)
        acc = jnp.exp2(m - m_new) * acc + _dot(vt, p.astype(BF16))
        return m_new, acc

    init = (jnp.full((1, cols), NEG, F32), jnp.zeros((vrows, cols), F32))

    def finish(carry):
        acc = carry[1]
        return acc[0:HEAD_DIM] / acc[HEAD_DIM:HEAD_DIM + 1]

    def pen(d):
        return jnp.where(qi >= d, 0.0, NEG)

    b1 = bias_blk(1) + pen(1)
    b0 = bias_blk(0)

    s_c = _dot_nt(kc_ref[0, 0], q)
    start = pl.multiple_of(ncp - qi * (tq // CMP_STRIDE), tq // CMP_STRIDE)
    bias_c = heads(lambda h: pat_ref[h, pl.ds(start, ncp), :])

    nd = WINDOW // tq
    krow = lax.broadcasted_iota(jnp.int32, (tq, cols), 0)
    qcol = lax.broadcasted_iota(jnp.int32, (tq, cols), 1) % tq
    s_parts, v_parts = [], []
    for d in range(nd, -1, -1):
        off = pl.multiple_of(jnp.maximum(qi - d, 0) * tq, tq)
        s = _dot_nt(kw_ref[0, 0, pl.ds(off, tq), :], q)
        if d == 0:
            s = s + b0
        elif d == 1:
            s = s + b1
        elif d == nd:
            s = jnp.where(krow > qcol, s + (cvec + pen(d)), NEG)
        else:
            s = s + (cvec + pen(d))
        s_parts.append(s)
        v_parts.append(vw_ref[0, 0, :, pl.ds(off, tq)])
    o_w = finish(flash(init, jnp.concatenate(s_parts, axis=0), jnp.concatenate(v_parts, axis=1)))

    valid_c = bias_c > 0.5 * NEG
    sc = s_c + bias_c
    m_c = jnp.max(sc, axis=0, keepdims=True)
    e_c = jnp.where(valid_c, jnp.exp2(sc - m_c), 0.0)
    l_c = jnp.sum(e_c, axis=0, keepdims=True)
    p_c = e_c / jnp.where(l_c > 0.0, l_c, 1.0)
    o_c = _dot(vc_ref[0, 0], p_c.astype(BF16))

    psum = p_c[:, 0:tq]
    for h in range(1, N_HG):
        psum = psum + p_c[:, h * tq:(h + 1) * tq]
    imp_t = _split_dot(psum, msel_ref[...], 3, mode="ex")
    jj = lax.broadcasted_iota(jnp.int32, (nb, tq), 0)
    tt = qi * tq + lax.broadcasted_iota(jnp.int32, (nb, tq), 1)
    cur = tt // SEL_BLOCK
    forced = (jj == 0) | (jj == cur) | (jj == cur - 1)
    valid_b = jj * SEL_BLOCK <= tt
    imp = jnp.where(valid_b, jnp.where(forced, FORCE_SCORE, imp_t), -jnp.inf)
    cnt = jnp.zeros((nb, tq), F32)
    for i in range(nb):
        ri = imp[i:i + 1, :]
        cnt = cnt + jnp.where(jj > i, jnp.where(ri >= imp, 1.0, 0.0), jnp.where(ri > imp, 1.0, 0.0))
    sel_t = jnp.where((cnt < float(min(SEL_TOPK, nb))) & valid_b, 0.0, NEG)

    per_tile = tq // SEL_BLOCK
    for t in range(pen_ref.shape[0]):
        pen_ref[t] = jnp.concatenate([sel_t[t * per_tile:(t + 1) * per_tile, :],
                                      jnp.zeros((8 - per_tile, tq), F32)], axis=0)
    blk8 = lax.broadcasted_iota(jnp.int32, (8, tq), 0)

    def tile_pen(t, hi):
        rows = jnp.where(t * per_tile + blk8 >= hi, NEG, pen_ref[t])
        return jnp.concatenate([jnp.broadcast_to(rows[u:u + 1, :], (SEL_BLOCK, tq)) for u in range(per_tile)],
                               axis=0)

    def sel_scores(t0, ntile, hi):
        off = pl.multiple_of(t0 * tq, tq)
        s = _dot_nt(ks_ref[0, 0, pl.ds(off, ntile * tq), :], q)
        spen = jnp.concatenate([tile_pen(t0 + u, hi) for u in range(ntile)], axis=0)
        return s + heads(lambda h: spen)

    nk = FAR_TILES * tq
    n_far = jnp.maximum(qi - 1, 0)
    nblk = (n_far + FAR_TILES - 1) // FAR_TILES

    def far_t0(j):
        return jnp.maximum(n_far - FAR_TILES * (j + 1), 0)

    def far_scores(j):
        return sel_scores(far_t0(j), FAR_TILES, (n_far - FAR_TILES * j) * per_tile)

    sbuf_ref[0] = far_scores(0)
    t1 = jnp.maximum(qi - 1, 0)
    hi_all = (qi + 1) * per_tile
    s = jnp.concatenate([sel_scores(t1, 1, hi_all) + (b1 - cvec), sel_scores(qi, 1, hi_all) + (b0 - cvec)],
                        axis=0)
    vx = jnp.concatenate([vs_ref[0, 0, :, pl.ds(pl.multiple_of(t1 * tq, tq), tq)],
                          vs_ref[0, 0, :, pl.ds(pl.multiple_of(qi * tq, tq), tq)]], axis=1)
    carry = flash(init, s, vx)

    def far_v(j):
        return vs_ref[0, 0, :, pl.ds(pl.multiple_of(far_t0(jnp.maximum(j, 0)) * tq, tq), nk)]

    def far_body(j, carry):
        m, acc, alpha_prev = carry
        slot = j % 2
        s = sbuf_ref[slot]
        pv = _dot(far_v(j - 1), pbuf_ref[1 - slot])
        sbuf_ref[1 - slot] = far_scores(jnp.minimum(j + 1, nblk - 1))
        m_new = jnp.maximum(m, jnp.max(s, axis=0, keepdims=True))
        pbuf_ref[slot] = jnp.exp2(s - m_new).astype(BF16)
        return m_new, alpha_prev * acc + pv, jnp.exp2(m - m_new)

    pbuf_ref[1] = jnp.zeros(pbuf_ref.shape[1:], BF16)
    m, acc, alpha = lax.fori_loop(0, nblk, far_body, (carry[0], carry[1], jnp.ones((1, cols), F32)))
    last = jnp.where(nblk > 0, (nblk - 1) % 2, 1)
    o_s = finish((m, alpha * acc + _dot(far_v(nblk - 1), pbuf_ref[last])))

    zg = pltpu.roll(zg_ref[...].astype(F32), (LANE - 16 * g) % LANE, axis=1)
    gates = _sigmoid(zg).T
    out = (heads(lambda h: gates[h:h + 1, :]) * o_c + heads(lambda h: gates[4 + h:5 + h, :]) * o_s
           + heads(lambda h: gates[8 + h:9 + h, :]) * o_w)
    for hp in range(N_HG // 2):
        pair = jnp.concatenate([out[:, (2 * hp) * tq:(2 * hp + 1) * tq],
                                out[:, (2 * hp + 1) * tq:(2 * hp + 2) * tq]], axis=0)
        o_ref[:, hp * 2 * HEAD_DIM:(hp + 1) * 2 * HEAD_DIM] = pair.T.astype(o_ref.dtype)


def _nsa_attn(rel, qn, kc, vc, ks, vs, kw, vw, z, pat, bw, *, batch, seq, tq, ncp):
    nt = seq // tq
    nb = seq // SEL_BLOCK
    n_cmp = (seq - CMP_BLOCK) // CMP_STRIDE + 1
    ci = np.arange(ncp)[None, :] * CMP_STRIDE
    sj = np.arange(nb)[:, None] * SEL_BLOCK
    msel = ((ci <= sj + SEL_BLOCK - 1) & (ci + CMP_BLOCK - 1 >= sj) & (np.arange(ncp)[None, :] < n_cmp))
    msel_j = jnp.asarray(msel.astype(np.float32), BF16)
    assert nb <= LANE and seq >= FAR_TILES * tq
    ks_spec = pl.BlockSpec((1, 1, seq, ks.shape[3]), lambda b, g, i: (b, g, 0, 0))
    kw_spec = pl.BlockSpec((1, 1, seq, HEAD_DIM), lambda b, g, i: (b, g, 0, 0))
    v_spec = pl.BlockSpec((1, 1, vs.shape[2], seq), lambda b, g, i: (b, g, 0, 0))
    kc_spec = pl.BlockSpec((1, 1, ncp, HEAD_DIM), lambda b, g, i: (b, g, 0, 0))
    vc_spec = pl.BlockSpec((1, 1, HEAD_DIM, ncp), lambda b, g, i: (b, g, 0, 0))
    width = N_HEADS * HEAD_DIM
    return pl.pallas_call(
        functools.partial(_nsa_attn_kernel, tq=tq, ncp=ncp, nb=nb),
        out_shape=jax.ShapeDtypeStruct((batch * seq, width), BF16),
        grid=(batch, N_KV, nt),
        in_specs=[
            pl.BlockSpec(memory_space=pltpu.SMEM),
            pl.BlockSpec((1, N_HG, tq, HEAD_DIM), lambda b, g, i: (b, g, i, 0)),
            kc_spec, vc_spec, ks_spec, v_spec, kw_spec, v_spec,
            pl.BlockSpec((tq, LANE), lambda b, g, i: (b * nt + i, C_GATE // LANE)),
            pl.BlockSpec((N_HG, 2 * ncp, tq), lambda b, g, i: (g, 0, 0)),
            pl.BlockSpec((N_HG, 2, tq, tq), lambda b, g, i: (g, 0, 0, 0)),
            pl.BlockSpec(msel_j.shape, lambda b, g, i: (0, 0)),
        ],
        out_specs=pl.BlockSpec((tq, N_HG * HEAD_DIM), lambda b, g, i: (b * nt + i, g)),
        scratch_shapes=[pltpu.VMEM((2, FAR_TILES * tq, N_HG * tq), F32),
                        pltpu.VMEM((nt, 8, tq), F32),
                        pltpu.VMEM((2, FAR_TILES * tq, N_HG * tq), BF16)],
        compiler_params=_cparams(("arbitrary", "arbitrary", "arbitrary")),
        name="nsa_attn",
    )(rel, qn, kc, vc, ks, vs, kw, vw, z, pat, bw, msel_j)


def _merge_kernel(x_ref, yr_ref, yn_ref, zm_ref, wr_ref, wn_ref, wm_ref, o_ref):
    d = x_ref.shape[1]
    zm = zm_ref[...].astype(F32)
    br = _dot(yr_ref[...].astype(BF16), wr_ref[...])
    bn = _dot(yn_ref[...].astype(BF16), wn_ref[...])
    merged = _sigmoid(zm[:, 0:d]) * br + _sigmoid(zm[:, d:2 * d]) * bn
    o_ref[...] = x_ref[...] + _dot(merged.astype(BF16), wm_ref[...])


def _merge(x, yr, yn, z, wr, wn, wm, *, tm):
    m, d = x.shape
    row = lambda c: pl.BlockSpec((tm, c), lambda i: (i, 0))
    full = lambda a: pl.BlockSpec(a.shape, lambda i: (0, 0))
    return pl.pallas_call(
        _merge_kernel,
        out_shape=jax.ShapeDtypeStruct((m, d), F32),
        grid=(m // tm,),
        in_specs=[row(d), row(d), row(d), pl.BlockSpec((tm, 2 * d), lambda i: (i, C_M // (2 * d))),
                  full(wr), full(wn), full(wm)],
        out_specs=row(d),
        compiler_params=_cparams(("arbitrary",)),
        name="merge",
    )(x, yr, yn, z, wr, wn, wm)


def _ca_kv_kernel(mem_ref, g_ref, w_ref, kg_ref, k_ref, v_ref):
    d = mem_ref.shape[2]
    dh = d // CA_HEADS
    mn = (_rms(mem_ref[0], NORM_EPS) * g_ref[...]).astype(BF16)
    kv = _dot(mn, w_ref[...])
    ks = [(_rms(kv[:, h * dh:(h + 1) * dh], QK_EPS) * kg_ref[...]) for h in range(CA_HEADS)]
    k_ref[0] = jnp.concatenate(ks, axis=1).astype(k_ref.dtype)
    v_ref[0] = kv[:, d:2 * d].astype(v_ref.dtype)


def _ca_kv(mem, g, wkv, kg):
    batch, nm, d = mem.shape
    o_shape = jax.ShapeDtypeStruct((batch, nm, d), BF16)
    o_spec = pl.BlockSpec((1, nm, d), lambda b: (b, 0, 0))
    full = lambda a: pl.BlockSpec(a.shape, lambda b: (0, 0))
    return pl.pallas_call(
        _ca_kv_kernel,
        out_shape=(o_shape, o_shape),
        grid=(batch,),
        in_specs=[o_spec, full(g), full(wkv), full(kg)],
        out_specs=(o_spec, o_spec),
        compiler_params=_cparams(("arbitrary",)),
        name="ca_kv",
    )(mem, g, wkv, kg)


def _cross_kernel(x_ref, g_ref, wq_ref, qg_ref, k_ref, v_ref, wo_ref, o_ref):
    d = x_ref.shape[1]
    dh = d // CA_HEADS
    x = x_ref[...]
    xn = (_rms(x, NORM_EPS) * g_ref[...]).astype(BF16)
    qf = _dot(xn, wq_ref[...])
    k = k_ref[0]
    v = v_ref[0]
    outs = []
    for h in range(CA_HEADS):
        sl = slice(h * dh, (h + 1) * dh)
        qh = (_rms(qf[:, sl], QK_EPS) * qg_ref[...] * (dh ** -0.5)).astype(BF16)
        s = _dot_nt(qh, k[:, sl])
        s = s - jnp.max(s, axis=-1, keepdims=True)
        e = jnp.exp(s)
        p = e / jnp.sum(e, axis=-1, keepdims=True)
        outs.append(_dot(p.astype(BF16), v[:, sl]))
    o = jnp.concatenate(outs, axis=1).astype(BF16)
    o_ref[...] = x + _dot(o, wo_ref[...])


def _cross(h1, g, wq, qg, kn, vv, wo, *, batch, seq, tm):
    m, d = h1.shape
    nt = seq // tm
    nm = kn.shape[1]
    row = pl.BlockSpec((tm, d), lambda b, i: (b * nt + i, 0))
    full = lambda a: pl.BlockSpec(a.shape, lambda b, i: (0, 0))
    kv_spec = pl.BlockSpec((1, nm, d), lambda b, i: (b, 0, 0))
    return pl.pallas_call(
        _cross_kernel,
        out_shape=jax.ShapeDtypeStruct((m, d), F32),
        grid=(batch, nt),
        in_specs=[row, full(g), full(wq), full(qg), kv_spec, kv_spec, full(wo)],
        out_specs=row,
        compiler_params=_cparams(("arbitrary", "arbitrary")),
        name="cross",
    )(h1, g, wq, qg, kn, vv, wo)


def _ffn_kernel(x_ref, g_ref, wa_ref, wb_ref, cw_ref, cb_ref, wd_ref, o_ref, xn_ref, acc_ref, carry_ref,
                *, tiles_per_seq):
    i = pl.program_id(0)
    j = pl.program_id(1)
    nj = pl.num_programs(1)
    tm = x_ref.shape[0]

    @pl.when(j == 0)
    def _():
        xn_ref[...] = (_rms(x_ref[...], NORM_EPS) * g_ref[...]).astype(BF16)
        acc_ref[...] = jnp.zeros_like(acc_ref)

    @pl.when(i % tiles_per_seq == 0)
    def _():
        carry_ref[j] = jnp.zeros(carry_ref.shape[1:], F32)

    xn = xn_ref[...]
    a = _dot(xn, wa_ref[...])
    b = _dot(xn, wb_ref[...])
    car = carry_ref[j]
    rowi = lax.broadcasted_iota(jnp.int32, a.shape, 0)
    p1 = jnp.where(rowi == 0, car[7:8, :], pltpu.roll(a, 1, axis=0))
    p2 = jnp.where(rowi == 0, car[6:7, :], jnp.where(rowi == 1, car[7:8, :], pltpu.roll(a, 2, axis=0)))
    carry_ref[j] = a[tm - 8:tm, :]
    cw = cw_ref[0]
    conv = cw[0:1, :] * p2 + cw[1:2, :] * p1 + cw[2:3, :] * a + cb_ref[0]
    act = conv * _sigmoid(conv) * b
    acc_ref[...] += _dot(act.astype(BF16), wd_ref[...])

    @pl.when(j == nj - 1)
    def _():
        o_ref[...] = x_ref[...] + acc_ref[...]


def _ffn(h2, g, wup, cw, cb, wd, *, seq, tm, tn):
    m, d = h2.shape
    dff = wd.shape[0]
    nj = dff // tn
    cw3 = jnp.zeros((nj, 8, tn), F32).at[:, 0:CONV_W, :].set(cw.reshape(CONV_W, nj, tn).transpose(1, 0, 2))
    cb3 = cb.reshape(nj, 1, tn)
    return pl.pallas_call(
        functools.partial(_ffn_kernel, tiles_per_seq=seq // tm),
        out_shape=jax.ShapeDtypeStruct((m, d), F32),
        grid=(m // tm, nj),
        in_specs=[
            pl.BlockSpec((tm, d), lambda i, j: (i, 0)),
            pl.BlockSpec((1, d), lambda i, j: (0, 0)),
            pl.BlockSpec((d, tn), lambda i, j: (0, j)),
            pl.BlockSpec((d, tn), lambda i, j: (0, nj + j)),
            pl.BlockSpec((1, 8, tn), lambda i, j: (j, 0, 0)),
            pl.BlockSpec((1, 1, tn), lambda i, j: (j, 0, 0)),
            pl.BlockSpec((tn, d), lambda i, j: (j, 0)),
        ],
        out_specs=pl.BlockSpec((tm, d), lambda i, j: (i, 0)),
        scratch_shapes=[pltpu.VMEM((tm, d), BF16), pltpu.VMEM((tm, d), F32), pltpu.VMEM((nj, 8, tn), F32)],
        compiler_params=_cparams(("arbitrary", "arbitrary")),
        name="ffn",
    )(h2, g, wup, wup, cw3, cb3, wd)


def _pack_perm(width):
    o_zw = 3 * width
    o_za = o_zw + R_LORA_W
    o_zg = o_za + R_LORA_A
    o_q = o_zg + R_LORA_G
    nkv = N_KV * HEAD_DIM
    o_kc = o_q + N_HEADS * HEAD_DIM
    o_gate = o_kc + 6 * nkv
    o_m = o_gate + 3 * N_HEADS
    perm = np.full((Z_COLS,), -1, np.int64)
    perm[C_RKV:C_RKV + 3 * width] = np.arange(3 * width)
    perm[C_Q:C_Q + N_HEADS * HEAD_DIM] = o_q + np.arange(N_HEADS * HEAD_DIM)
    perm[C_KC:C_KC + 6 * nkv] = o_kc + np.arange(6 * nkv)
    perm[C_ZG:C_ZG + R_LORA_G] = o_zg + np.arange(R_LORA_G)
    perm[C_ZWA:C_ZWA + R_LORA_W + R_LORA_A] = o_zw + np.arange(R_LORA_W + R_LORA_A)
    for g in range(N_KV):
        for c in range(3):
            for h in range(N_HG):
                perm[C_GATE + 16 * g + 4 * c + h] = o_gate + (g * N_HG + h) * 3 + c
    perm[C_M:C_M + 2 * width] = o_m + np.arange(2 * width)
    return perm, o_m + 2 * width


def _pad_rows(w, rows, offset=0):
    out = jnp.zeros((rows, w.shape[1]), w.dtype)
    return out.at[offset:offset + w.shape[0]].set(w)


def kernel(x, mem, rel_bias, norm_mix, w_in, rwkv_mu, rwkv_w0, rwkv_w2, rwkv_a0, rwkv_a2, rwkv_g2,
           rwkv_kk, rwkv_ka, rwkv_rk, rwkv_lnx_w, rwkv_lnx_b, nsa_q_gain, nsa_k_gain, cmp_pe_k, cmp_pe_v,
           cmp_w1_k, cmp_w2_k, cmp_w1_v, cmp_w2_v, w_branch_rwkv, w_branch_nsa, w_mix_out,
           norm_cross, norm_mem, ca_wq, ca_wkv, ca_q_gain, ca_k_gain, ca_wo,
           norm_ffn, ffn_up, ffn_conv, ffn_conv_b, ffn_down):
    batch, seq, d = x.shape
    depth = w_in.shape[0]
    width = d
    assert d == N_HEADS * HEAD_DIM and 2 * (Z_COLS - C_M) == 4 * d
    perm, in_cols = _pack_perm(width)
    assert w_in.shape[2] == in_cols
    perm_j = jnp.asarray(np.maximum(perm, 0), jnp.int32)
    keep = jnp.asarray(perm >= 0)

    tq = 128
    assert seq % 256 == 0 and tq > REL_MAX_DIST - 1 and WINDOW % tq == 0
    ncp = -(-(seq // CMP_STRIDE) // LANE) * LANE
    n_cmp = (seq - CMP_BLOCK) // CMP_STRIDE + 1
    tm = min(1024, seq)
    row = lambda v: v.reshape(1, -1).astype(F32)
    nkv = N_KV * HEAD_DIM

    h = x.reshape(batch * seq, d)
    for l in range(depth):
        w_in_p = jnp.where(keep[None, :], jnp.take(w_in[l], perm_j, axis=1), 0.0).astype(BF16)
        mu = rwkv_mu[l]
        mu_r = row(mu[0:3 * width])
        mu_wa = row(mu[3 * width:3 * width + R_LORA_W + R_LORA_A])
        mu_g = row(jnp.zeros((256,), F32).at[0:R_LORA_G].set(mu[3 * width + R_LORA_W + R_LORA_A:]))
        w2p = _pad_rows(rwkv_w2[l], LANE, 0).astype(BF16)
        a2p = _pad_rows(rwkv_a2[l], LANE, R_LORA_W).astype(BF16)
        g2p = _pad_rows(rwkv_g2[l], 256, 0).astype(BF16)

        z = _norm_matmul(h, row(norm_mix[l]), w_in_p, tm=tm, tn=1024, out_dtype=BF16, name="in_proj")

        y_r = _rwkv(z, mu_r, mu_g, mu_wa, row(rwkv_w0[l]), w2p, row(rwkv_a0[l]), a2p, g2p,
                    row(rwkv_kk[l]), row(rwkv_ka[l]), row(rwkv_rk[l]), row(rwkv_lnx_w[l]),
                    row(rwkv_lnx_b[l]), batch=batch, seq=seq, width=width)

        qn, ks, vs, kw, vw = _nsa_prep(
            z, row(jnp.tile(nsa_q_gain[l], N_HEADS)), row(jnp.tile(nsa_k_gain[l, 1], N_KV)),
            row(jnp.tile(nsa_k_gain[l, 2], N_KV)), batch=batch, seq=seq)
        z3 = z.reshape(batch, seq // CMP_STRIDE, CMP_STRIDE, Z_COLS)

        def groups(c0):
            t = z3[:, :, :, c0:c0 + nkv].reshape(batch, seq // CMP_STRIDE, CMP_STRIDE, N_KV, HEAD_DIM)
            return t.transpose(0, 3, 1, 2, 4).reshape(batch, N_KV, seq // CMP_STRIDE, CMP_STRIDE * HEAD_DIM)

        kc, vc = _nsa_cmp(groups(C_KC), groups(C_KC + nkv), cmp_pe_k[l].reshape(1, -1), cmp_pe_v[l].reshape(1, -1),
                          cmp_w1_k[l].astype(BF16), cmp_w2_k[l].astype(BF16), cmp_w1_v[l].astype(BF16),
                          cmp_w2_v[l].astype(BF16), row(nsa_k_gain[l, 0]), ncp=ncp, n_cmp=n_cmp)
        bw, pat = _nsa_bias(rel_bias.astype(F32), tq=tq, ncp=ncp)
        y_n = _nsa_attn(rel_bias.astype(F32), qn, kc, vc, ks, vs, kw, vw, z, pat, bw,
                        batch=batch, seq=seq, tq=tq, ncp=ncp)

        h1 = _merge(h, y_r, y_n, z, w_branch_rwkv[l].astype(BF16), w_branch_nsa[l].astype(BF16),
                    w_mix_out[l].astype(BF16), tm=min(512, seq))

        kn, vv = _ca_kv(mem, row(norm_mem[l]), ca_wkv[l].astype(BF16), row(ca_k_gain[l]))
        h2 = _cross(h1, row(norm_cross[l]), ca_wq[l].astype(BF16), row(ca_q_gain[l]), kn, vv,
                    ca_wo[l].astype(BF16), batch=batch, seq=seq, tm=min(512, seq))

        h = _ffn(h2, row(norm_ffn[l]), ffn_up[l].astype(BF16), ffn_conv[l], ffn_conv_b[l],
                 ffn_down[l].astype(BF16), seq=seq, tm=tm, tn=256)
    return h.reshape(batch, seq, d)
```

```python
import functools
import math

import jax
import jax.numpy as jnp
import numpy as np
from jax import lax
from jax.experimental import pallas as pl
from jax.experimental.pallas import tpu as pltpu

F32 = jnp.float32
BF16 = jnp.bfloat16
NEG = -1e30
LOG2E = 1.4426950408889634
FAR_KEYS = 512
V_ROWS = 80

R_HEAD = 64
R_LORA_W = 64
R_LORA_A = 64
R_LORA_G = 160
LNX_EPS = 64e-5
N_HEADS = 16
N_KV = 4
N_HG = N_HEADS // N_KV
HEAD_DIM = 64
CMP_BLOCK = 32
CMP_STRIDE = 16
CMP_HIDDEN = 256
SEL_BLOCK = 64
SEL_TOPK = 16
WINDOW = 512
FORCE_SCORE = 1e4
REL_BUCKETS = 32
REL_MAX_DIST = 128
CA_HEADS = 4
CONV_W = 3
NORM_EPS = 1e-6
QK_EPS = 1e-6

LANE = 128
VMEM_LIMIT = 56 * 1024 * 1024

C_RKV = 0
C_Q = 3072
C_KC = 4096
C_ZG = 5632
C_ZWA = 5888
C_GATE = 6016
C_M = 6144
Z_COLS = 8192


def _dot(a, b):
    return jnp.dot(a, b, preferred_element_type=F32)


def _dot_nt(a, b):
    return lax.dot_general(a, b, (((1,), (1,)), ((), ())), preferred_element_type=F32)


def _split_dot(x, e, parts, mode="xe"):
    acc = None
    rem = x
    for i in range(parts):
        hi = rem.astype(BF16)
        t = _dot(hi, e) if mode == "xe" else (_dot(e, hi) if mode == "ex" else _dot_nt(e, hi))
        acc = t if acc is None else acc + t
        if i + 1 < parts:
            rem = rem - hi.astype(F32)
    return acc


def _rms(x, eps):
    return x * lax.rsqrt(jnp.mean(x * x, axis=-1, keepdims=True) + eps)


def _sigmoid(x):
    return 1.0 / (1.0 + jnp.exp(-x))


def _cparams(sem):
    return pltpu.CompilerParams(dimension_semantics=sem, vmem_limit_bytes=VMEM_LIMIT)


def _norm_matmul_kernel(x_ref, g_ref, w_ref, o_ref, xn_ref):
    @pl.when(pl.program_id(1) == 0)
    def _():
        x = x_ref[...]
        xn_ref[...] = (_rms(x, NORM_EPS) * g_ref[...]).astype(BF16)

    o_ref[...] = _dot(xn_ref[...], w_ref[...]).astype(o_ref.dtype)


def _norm_matmul(x, g, w, *, tm, tn, out_dtype, name):
    m, k = x.shape
    n = w.shape[1]
    return pl.pallas_call(
        _norm_matmul_kernel,
        out_shape=jax.ShapeDtypeStruct((m, n), out_dtype),
        grid=(m // tm, n // tn),
        in_specs=[
            pl.BlockSpec((tm, k), lambda i, j: (i, 0)),
            pl.BlockSpec((1, k), lambda i, j: (0, 0)),
            pl.BlockSpec((k, tn), lambda i, j: (0, j)),
        ],
        out_specs=pl.BlockSpec((tm, tn), lambda i, j: (i, j)),
        scratch_shapes=[pltpu.VMEM((tm, k), BF16)],
        compiler_params=_cparams(("arbitrary", "arbitrary")),
        name=name,
    )(x, g, w)


def _rwkv_kernel(zr_ref, zg_ref, zwa_ref, mur_ref, mug_ref, muwa_ref, w0_ref, w2_ref, a0_ref, a2_ref,
                 g2_ref, kk_ref, ka_ref, rk_ref, lnw_ref, lnb_ref, eh_ref, eht_ref, tri_ref,
                 o_ref, pr_ref, pg_ref, pwa_ref, st_ref, *, L, width):
    i = pl.program_id(1)
    npair = width // LANE
    tb = o_ref.shape[0]

    @pl.when(i == 0)
    def _():
        pr_ref[...] = jnp.zeros_like(pr_ref)
        pg_ref[...] = jnp.zeros_like(pg_ref)
        pwa_ref[...] = jnp.zeros_like(pwa_ref)
        st_ref[...] = jnp.zeros_like(st_ref)

    def shifted8(z, prev_ref, mu):
        rolled = pltpu.roll(z, 1, axis=0)
        row0 = lax.broadcasted_iota(jnp.int32, z.shape, 0) == 0
        prev = jnp.where(row0, prev_ref[7:8, :], rolled)
        prev_ref[...] = z[tb - 8:tb, :]
        return z + (prev - z) * mu

    zs = shifted8(zr_ref[...].astype(F32), pr_ref, mur_ref[...])
    zsg = shifted8(zg_ref[...].astype(F32), pg_ref, mug_ref[...])
    zswa = shifted8(zwa_ref[...].astype(F32), pwa_ref, muwa_ref[...])

    r = zs[:, 0:width]
    k = zs[:, width:2 * width]
    v = zs[:, 2 * width:3 * width]

    w_lin = w0_ref[...] + _dot(jnp.tanh(zswa).astype(BF16), w2_ref[...])
    a_lin = a0_ref[...] + _dot(zswa.astype(BF16), a2_ref[...])
    y = -w_lin
    softplus = jnp.maximum(y, 0.0) + jnp.log(1.0 + jnp.exp(-jnp.abs(y)))
    ld = -jnp.exp(-softplus - 0.5)
    a = _sigmoid(a_lin)
    g = _dot(_sigmoid(zsg).astype(BF16), g2_ref[...])

    eh = eh_ref[...]
    eht = eht_ref[...]

    def sum_heads(t):
        return _split_dot(t, eh, 2)

    def bcast_heads(t):
        return _split_dot(t, eht, 2)

    kkr = k * kk_ref[...]
    nrm = jnp.maximum(jnp.sqrt(sum_heads(kkr * kkr)), 1e-12)
    kkn = kkr * bcast_heads(1.0 / nrm)
    k2 = k * (1.0 + (a - 1.0) * ka_ref[...])
    av = -kkn
    bv = kkn * a
    bonus = bcast_heads(sum_heads(r * k2 * rk_ref[...])) * v

    nch = tb // L
    lg = _split_dot(ld, tri_ref[...], 3, mode="ex")
    lasts = [lg[(c + 1) * L - 1:(c + 1) * L, :] for c in range(nch)]
    lg_last = jnp.concatenate([jnp.broadcast_to(t, (L, width)) for t in lasts], axis=0)
    eg = jnp.exp(lg)
    eng = jnp.exp(-lg)
    egl = jnp.exp(lg_last - lg)
    rt = r * eg
    kt = k2 * eng
    bt = bv * eng
    at = av * jnp.exp(lg - ld)
    kgl = k2 * egl
    bgl = bv * egl
    gl = [jnp.exp(t) for t in lasts]

    lane = lax.broadcasted_iota(jnp.int32, (L, LANE), 1)
    lo = lane < R_HEAD

    def stack(t):
        return jnp.concatenate([jnp.where(lo, t, 0.0), jnp.where(lo, 0.0, t)], axis=0)

    row = lax.broadcasted_iota(jnp.int32, (2 * L, 2 * L), 0)
    col = lax.broadcasted_iota(jnp.int32, (2 * L, 2 * L), 1)
    strict = row > col
    incl = row >= col
    eye = (row == col).astype(F32)
    nsq = int(math.log2(L)) - 1

    pairs = range(npair)
    chains = [(c, p) for c in range(nch) for p in pairs]
    sub = lambda t, c, p: t[c * L:(c + 1) * L, p * LANE:(p + 1) * LANE]
    bf = lambda t: t.astype(BF16)
    ar = {cp: bf(jnp.concatenate([stack(sub(at, *cp)), stack(sub(rt, *cp))], axis=0)) for cp in chains}
    bk = {cp: bf(jnp.concatenate([stack(sub(bt, *cp)), stack(sub(kt, *cp))], axis=0)) for cp in chains}
    v_s = {cp: bf(stack(sub(v, *cp))) for cp in chains}
    gm = {cp: _dot_nt(ar[cp], bk[cp]) for cp in chains}
    a_ab = {cp: jnp.where(strict, gm[cp][0:2 * L, 0:2 * L], 0.0) for cp in chains}
    a_ak = {cp: bf(jnp.where(strict, gm[cp][0:2 * L, 2 * L:4 * L], 0.0)) for cp in chains}
    a_rb = {cp: bf(jnp.where(incl, gm[cp][2 * L:4 * L, 0:2 * L], 0.0)) for cp in chains}
    a_rk = {cp: bf(jnp.where(incl, gm[cp][2 * L:4 * L, 2 * L:4 * L], 0.0)) for cp in chains}
    tinv = {cp: eye + a_ab[cp] for cp in chains}
    pw = a_ab
    for _ in range(nsq):
        pwb = {cp: bf(pw[cp]) for cp in chains}
        pw = {cp: _dot(pwb[cp], pwb[cp]) for cp in chains}
        tinv = {cp: tinv[cp] + _dot(bf(pw[cp]), bf(tinv[cp])) for cp in chains}
    tinv = {cp: bf(tinv[cp]) for cp in chains}
    bkgt = {cp: bf(jnp.concatenate([stack(sub(bgl, *cp)), stack(sub(kgl, *cp))], axis=0).T) for cp in chains}
    glcol = {(c, p): jnp.sum(eye * gl[c][:, p * LANE:(p + 1) * LANE], axis=1, keepdims=True)
             for (c, p) in chains}
    h = [st_ref[p] for p in pairs]
    ys = []
    for c in range(nch):
        hb = [bf(h[p]) for p in pairs]
        wmat = [_dot(jnp.concatenate([ar[c, p][0:2 * L], a_ak[c, p]], axis=1),
                     jnp.concatenate([hb[p], v_s[c, p]], axis=0)) for p in pairs]
        ub = [bf(_dot(tinv[c, p], bf(wmat[p]))) for p in pairs]
        yy = [_dot(jnp.concatenate([ar[c, p][2 * L:4 * L], a_rb[c, p], a_rk[c, p]], axis=1),
                   jnp.concatenate([hb[p], ub[p], v_s[c, p]], axis=0)) for p in pairs]
        h = [h[p] * glcol[c, p] + _dot(bkgt[c, p], jnp.concatenate([ub[p], v_s[c, p]], axis=0))
             for p in pairs]
        ys.append(jnp.concatenate([yy[p][0:L] + yy[p][L:2 * L] for p in pairs], axis=1))
    for p in pairs:
        st_ref[p] = h[p]

    yv = jnp.concatenate(ys, axis=0)
    inv_n = 1.0 / R_HEAD
    mean = bcast_heads(sum_heads(yv) * inv_n)
    yc = yv - mean
    var = bcast_heads(sum_heads(yc * yc) * inv_n)
    yn = yc * lax.rsqrt(var + LNX_EPS) * lnw_ref[...] + lnb_ref[...]
    o_ref[...] = ((yn + bonus) * g).astype(o_ref.dtype)


def _rwkv(z, mu_r, mu_g, mu_wa, w0, w2p, a0, a2p, g2p, kk, ka, rk, lnw, lnb, *, batch, seq, width):
    L = 64
    tb = 2 * L
    nt = seq // tb
    nheads = width // R_HEAD
    eh = np.zeros((width, LANE), np.float32)
    eh[np.arange(width), np.arange(width) // R_HEAD] = 1.0
    ti = np.arange(tb)
    tri = ((ti[:, None] >= ti[None, :]) & (ti[:, None] // L == ti[None, :] // L)).astype(np.float32)
    vec = lambda c: pl.BlockSpec((1, c), lambda b, i: (0, 0))
    full = lambda a: pl.BlockSpec(a.shape, lambda b, i: (0,) * a.ndim)
    eh_j = jnp.asarray(eh, BF16)
    eht_j = jnp.asarray(eh.T, BF16)
    tri_j = jnp.asarray(tri, BF16)
    return pl.pallas_call(
        functools.partial(_rwkv_kernel, L=L, width=width),
        out_shape=jax.ShapeDtypeStruct((batch * seq, width), BF16),
        grid=(batch, nt),
        in_specs=[
            pl.BlockSpec((tb, 3 * width), lambda b, i: (b * nt + i, C_RKV // (3 * width))),
            pl.BlockSpec((tb, 256), lambda b, i: (b * nt + i, C_ZG // 256)),
            pl.BlockSpec((tb, LANE), lambda b, i: (b * nt + i, C_ZWA // LANE)),
            vec(3 * width), vec(256), vec(LANE),
            vec(width), full(w2p), vec(width), full(a2p), full(g2p),
            vec(width), vec(width), vec(width), vec(width), vec(width),
            full(eh_j), full(eht_j), full(tri_j),
        ],
        out_specs=pl.BlockSpec((tb, width), lambda b, i: (b * nt + i, 0)),
        scratch_shapes=[
            pltpu.VMEM((8, 3 * width), F32),
            pltpu.VMEM((8, 256), F32),
            pltpu.VMEM((8, LANE), F32),
            pltpu.VMEM((nheads // 2, 2 * R_HEAD, 2 * R_HEAD), F32),
        ],
        compiler_params=_cparams(("arbitrary", "arbitrary")),
        name="rwkv",
    )(z, z, z, mu_r, mu_g, mu_wa, w0, w2p, a0, a2p, g2p, kk, ka, rk, lnw, lnb, eh_j, eht_j, tri_j)


def _nsa_prep_kernel(zq_ref, zs_ref, zw_ref, qg_ref, kgs_ref, kgw_ref, ehq_ref, ehqt_ref,
                     q_ref, ks_ref, vs_ref, kw_ref, vw_ref):
    inv_d = 1.0 / HEAD_DIM

    def head_norm(x, gain, e, et):
        ss = _split_dot(x * x, e, 2)
        inv = lax.rsqrt(ss * inv_d + QK_EPS)
        return x * _split_dot(inv, et, 2) * gain

    nq = zq_ref.shape[1]
    nk = N_KV * HEAD_DIM
    ehq = ehq_ref[...]
    ehqt = ehqt_ref[...]
    q = head_norm(zq_ref[...].astype(F32), qg_ref[...], ehq, ehqt) * (HEAD_DIM ** -0.5 * LOG2E)
    qt = q.T
    for gq in range(N_KV):
        rows = [qt[(gq * N_HG + h) * HEAD_DIM:(gq * N_HG + h + 1) * HEAD_DIM, :] for h in range(N_HG)]
        q_ref[0, gq, 0] = jnp.concatenate(rows, axis=1).astype(q_ref.dtype)
    zs = zs_ref[...].astype(F32)
    zw = zw_ref[...].astype(F32)
    ehk = ehq[0:nk, :]
    ehkt = ehqt[:, 0:nk]
    ksn = head_norm(zs[:, 0:nk], kgs_ref[...], ehk, ehkt)
    kwn = head_norm(zw[:, 0:nk], kgw_ref[...], ehk, ehkt)
    vst = zs[:, nk:2 * nk].T
    vwt = zw[:, nk:2 * nk].T
    tb = zs.shape[0]
    vextra = jnp.where(lax.broadcasted_iota(jnp.int32, (V_ROWS - HEAD_DIM, tb), 0) == 0, 1.0, 0.0)
    for gq in range(N_KV):
        sl = slice(gq * HEAD_DIM, (gq + 1) * HEAD_DIM)
        ks_ref[0, gq] = ksn[:, sl].astype(ks_ref.dtype)
        vs_ref[0, gq] = jnp.concatenate([vst[sl, :], vextra], axis=0).astype(vs_ref.dtype)
        kw_ref[0, gq] = kwn[:, sl].astype(kw_ref.dtype)
        vw_ref[0, gq] = jnp.concatenate([vwt[sl, :], vextra], axis=0).astype(vw_ref.dtype)


def _nsa_prep(z, q_gain_t, kgs_t, kgw_t, *, batch, seq, tq):
    tb = tq
    nt = seq // tb
    nq = N_HEADS * HEAD_DIM
    nk = N_KV * HEAD_DIM
    eh = np.zeros((nq, LANE), np.float32)
    eh[np.arange(nq), np.arange(nq) // HEAD_DIM] = 1.0
    eh_j = jnp.asarray(eh, BF16)
    eht_j = jnp.asarray(eh.T, BF16)
    k_shape = jax.ShapeDtypeStruct((batch, N_KV, seq, HEAD_DIM), BF16)
    k_spec = pl.BlockSpec((1, N_KV, tb, HEAD_DIM), lambda b, i: (b, 0, i, 0))
    v_shape = jax.ShapeDtypeStruct((batch, N_KV, V_ROWS, seq), BF16)
    v_spec = pl.BlockSpec((1, N_KV, V_ROWS, tb), lambda b, i: (b, 0, 0, i))
    return pl.pallas_call(
        _nsa_prep_kernel,
        out_shape=(jax.ShapeDtypeStruct((batch, N_KV, nt, HEAD_DIM, N_HG * tq), BF16),
                   k_shape, v_shape, k_shape, v_shape),
        grid=(batch, nt),
        in_specs=[
            pl.BlockSpec((tb, nq), lambda b, i: (b * nt + i, C_Q // nq)),
            pl.BlockSpec((tb, 2 * nk), lambda b, i: (b * nt + i, C_KC // (2 * nk) + 1)),
            pl.BlockSpec((tb, 2 * nk), lambda b, i: (b * nt + i, C_KC // (2 * nk) + 2)),
            pl.BlockSpec((1, nq), lambda b, i: (0, 0)),
            pl.BlockSpec((1, nk), lambda b, i: (0, 0)),
            pl.BlockSpec((1, nk), lambda b, i: (0, 0)),
            pl.BlockSpec(eh_j.shape, lambda b, i: (0, 0)),
            pl.BlockSpec(eht_j.shape, lambda b, i: (0, 0)),
        ],
        out_specs=(pl.BlockSpec((1, N_KV, 1, HEAD_DIM, N_HG * tq), lambda b, i: (b, 0, i, 0, 0)),
                   k_spec, v_spec, k_spec, v_spec),
        compiler_params=_cparams(("arbitrary", "arbitrary")),
        name="nsa_prep",
    )(z, z, z, q_gain_t, kgs_t, kgw_t, eh_j, eht_j)


def _nsa_cmp_kernel(ak_ref, av_ref, pek_ref, pev_ref, w1k_ref, w2k_ref, w1v_ref, w2v_ref, kg_ref,
                    kc_ref, vc_ref, *, n_cmp):
    ncp = kc_ref.shape[2]
    half = w1k_ref.shape[0] // 2

    def compress(a_ref, pe_ref, w1_ref, w2_ref):
        rows = a_ref.shape[2]
        a = a_ref[0, 0].astype(BF16)
        p1 = _dot(a, w1_ref[0:half, :])
        p2 = _dot(a, w1_ref[half:2 * half, :])
        pe8 = jnp.broadcast_to(pe_ref[...], (8, pe_ref.shape[1])).astype(BF16)
        pe_bias = _dot(pe8, w1_ref[...])[0:1, :]
        hpre = p1 + pltpu.roll(p2, rows - 1, axis=0) + pe_bias
        hid = 0.5 * hpre * (1.0 + jnp.tanh(math.sqrt(2.0 / math.pi) * (hpre + 0.044715 * hpre * hpre * hpre)))
        out = _dot(hid.astype(BF16), w2_ref[...])
        if rows < ncp:
            out = jnp.concatenate([out, jnp.zeros((ncp - rows, out.shape[1]), F32)], axis=0)
        return out

    valid = lax.broadcasted_iota(jnp.int32, (ncp, HEAD_DIM), 0) < n_cmp
    kc = compress(ak_ref, pek_ref, w1k_ref, w2k_ref)
    kc = _rms(kc, QK_EPS) * kg_ref[...]
    kc_ref[0, 0] = jnp.where(valid, kc, 0.0).astype(kc_ref.dtype)
    vc = jnp.where(valid, compress(av_ref, pev_ref, w1v_ref, w2v_ref), 0.0)
    vct = jnp.concatenate([vc, jnp.zeros((ncp, LANE - HEAD_DIM), F32)], axis=1).T
    vc_ref[0, 0] = vct[0:HEAD_DIM, :].astype(vc_ref.dtype)


def _nsa_cmp(ak, av, pek, pev, w1k, w2k, w1v, w2v, kg, *, ncp, n_cmp):
    batch, ng, rows, wid = ak.shape
    a_spec = pl.BlockSpec((1, 1, rows, wid), lambda b, g: (b, g, 0, 0))
    full = lambda a: pl.BlockSpec(a.shape, lambda b, g: (0,) * a.ndim)
    k_shape = jax.ShapeDtypeStruct((batch, ng, ncp, HEAD_DIM), BF16)
    k_spec = pl.BlockSpec((1, 1, ncp, HEAD_DIM), lambda b, g: (b, g, 0, 0))
    v_shape = jax.ShapeDtypeStruct((batch, ng, HEAD_DIM, ncp), BF16)
    v_spec = pl.BlockSpec((1, 1, HEAD_DIM, ncp), lambda b, g: (b, g, 0, 0))
    return pl.pallas_call(
        functools.partial(_nsa_cmp_kernel, n_cmp=n_cmp),
        out_shape=(k_shape, v_shape),
        grid=(batch, ng),
        in_specs=[a_spec, a_spec, full(pek), full(pev), full(w1k), full(w2k), full(w1v), full(w2v), full(kg)],
        out_specs=(k_spec, v_spec),
        compiler_params=_cparams(("arbitrary", "arbitrary")),
        name="nsa_cmp",
    )(ak, av, pek, pev, w1k, w2k, w1v, w2v, kg)


def _bucket_thresholds():
    exact = REL_BUCKETS // 2
    n = np.arange(0, REL_MAX_DIST + 1)
    nf = np.maximum(n, 1).astype(np.float64)
    large = exact + (np.log(nf / exact) / math.log(REL_MAX_DIST / exact) * (REL_BUCKETS - exact)).astype(np.int64)
    large = np.minimum(large, REL_BUCKETS - 1)
    bucket = np.where(n < exact, n, large)
    assert bucket[-1] == REL_BUCKETS - 1
    return [int(np.argmax(bucket >= b)) for b in range(REL_BUCKETS)]


def _nsa_bias_kernel(rel_ref, bw_ref, pat_ref, *, tq, ncp):
    h = pl.program_id(0)
    thr = _bucket_thresholds()

    def bias_of(d):
        val = jnp.full(d.shape, rel_ref[0, h], F32)
        for b in range(1, REL_BUCKETS):
            val = jnp.where(d >= thr[b], rel_ref[b, h], val)
        return jnp.where(d < 0, NEG, val * LOG2E)

    kk = lax.broadcasted_iota(jnp.int32, (tq, tq), 0)
    qq = lax.broadcasted_iota(jnp.int32, (tq, tq), 1)
    bw_ref[0, 0] = bias_of(qq - kk)
    bw_ref[0, 1] = bias_of(qq - kk + tq)
    cc = lax.broadcasted_iota(jnp.int32, (2 * ncp, tq), 0)
    qc = lax.broadcasted_iota(jnp.int32, (2 * ncp, tq), 1)
    pat_ref[0] = bias_of(qc - CMP_STRIDE * (cc - ncp) - (CMP_BLOCK - 1))


def _nsa_bias(rel, *, tq, ncp):
    return pl.pallas_call(
        functools.partial(_nsa_bias_kernel, tq=tq, ncp=ncp),
        out_shape=(jax.ShapeDtypeStruct((N_HEADS, 2, tq, tq), F32),
                   jax.ShapeDtypeStruct((N_HEADS, 2 * ncp, tq), F32)),
        grid=(N_HEADS,),
        in_specs=[pl.BlockSpec(memory_space=pltpu.SMEM)],
        out_specs=(pl.BlockSpec((1, 2, tq, tq), lambda h: (h, 0, 0, 0)),
                   pl.BlockSpec((1, 2 * ncp, tq), lambda h: (h, 0, 0))),
        compiler_params=_cparams(("arbitrary",)),
        name="nsa_bias",
    )(rel)


def _nsa_attn_kernel(rel_ref, q_ref, kc_ref, vc_ref, ks_ref, vs_ref, kw_ref, vw_ref, zg_ref, pat_ref,
                     bw_ref, msel_ref, o_ref, sbuf_ref, pen_ref, pbuf_ref, *, tq, ncp, nb):
    g = pl.program_id(1)
    qi = pl.program_id(2)
    cols = N_HG * tq
    qt = q_ref[0, 0, 0]

    def heads(fn):
        return jnp.concatenate([fn(h) for h in range(N_HG)], axis=1)

    ccol = lax.broadcasted_iota(jnp.int32, (1, cols), 1) // tq
    cvec = jnp.zeros((1, cols), F32)
    for h in range(N_HG):
        cvec = jnp.where(ccol == h, rel_ref[REL_BUCKETS - 1, g * N_HG + h] * LOG2E, cvec)

    def bias_blk(which):
        return heads(lambda h: bw_ref[h, which])

    vrows = vs_ref.shape[2]

    def flash(carry, s, vt):
        m, acc = carry
        m_new = jnp.maximum(m, jnp.max(s, axis=0, keepdims=True))
        p = jnp.exp2(s - m_new)
        acc = jnp.exp2(m - m_new) * acc + _dot(vt, p.astype(BF16))
        return m_new, acc

    init = (jnp.full((1, cols), NEG, F32), jnp.zeros((vrows, cols), F32))

    def finish(carry):
        acc = carry[1]
        return acc[0:HEAD_DIM] / acc[HEAD_DIM:HEAD_DIM + 1]

    def pen(d):
        return jnp.where(qi >= d, 0.0, NEG)

    b1 = bias_blk(1) + pen(1)
    b0 = bias_blk(0)

    s_c = _dot(kc_ref[0, 0], qt)
    start = pl.multiple_of(ncp - qi * (tq // CMP_STRIDE), tq // CMP_STRIDE)
    bias_c = heads(lambda h: pat_ref[h, pl.ds(start, ncp), :])

    nd = WINDOW // tq
    krow = lax.broadcasted_iota(jnp.int32, (tq, cols), 0)
    qcol = lax.broadcasted_iota(jnp.int32, (tq, cols), 1) % tq
    s_parts, v_parts = [], []
    for d in range(nd, -1, -1):
        off = pl.multiple_of(jnp.maximum(qi - d, 0) * tq, tq)
        s = _dot(kw_ref[0, 0, pl.ds(off, tq), :], qt)
        if d == 0:
            s = s + b0
        elif d == 1:
            s = s + b1
        elif d == nd:
            s = jnp.where(krow > qcol, s + (cvec + pen(d)), NEG)
        else:
            s = s + (cvec + pen(d))
        s_parts.append(s)
        v_parts.append(vw_ref[0, 0, :, pl.ds(off, tq)])
    o_w = finish(flash(init, jnp.concatenate(s_parts, axis=0), jnp.concatenate(v_parts, axis=1)))

    valid_c = bias_c > 0.5 * NEG
    sc = s_c + bias_c
    m_c = jnp.max(sc, axis=0, keepdims=True)
    e_c = jnp.where(valid_c, jnp.exp2(sc - m_c), 0.0)
    l_c = jnp.sum(e_c, axis=0, keepdims=True)
    p_c = e_c / jnp.where(l_c > 0.0, l_c, 1.0)
    o_c = _dot(vc_ref[0, 0], p_c.astype(BF16))

    psum = p_c[:, 0:tq]
    for h in range(1, N_HG):
        psum = psum + p_c[:, h * tq:(h + 1) * tq]
    imp_t = _split_dot(psum, msel_ref[...], 3, mode="ex")
    jj = lax.broadcasted_iota(jnp.int32, (nb, tq), 0)
    tt = qi * tq + lax.broadcasted_iota(jnp.int32, (nb, tq), 1)
    cur = tt // SEL_BLOCK
    forced = (jj == 0) | (jj == cur) | (jj == cur - 1)
    valid_b = jj * SEL_BLOCK <= tt
    imp = jnp.where(valid_b, jnp.where(forced, FORCE_SCORE, imp_t), -jnp.inf)
    cnt = jnp.zeros((nb, tq), F32)
    for i in range(nb):
        ri = imp[i:i + 1, :]
        cnt = cnt + jnp.where(jj > i, jnp.where(ri >= imp, 1.0, 0.0), jnp.where(ri > imp, 1.0, 0.0))
    sel_t = jnp.where((cnt < float(min(SEL_TOPK, nb))) & valid_b, 0.0, NEG)

    per_tile = tq // SEL_BLOCK
    for t in range(pen_ref.shape[0]):
        pen_ref[t] = jnp.concatenate([sel_t[t * per_tile:(t + 1) * per_tile, :],
                                      jnp.zeros((8 - per_tile, tq), F32)], axis=0)
    blk8 = lax.broadcasted_iota(jnp.int32, (8, tq), 0)

    def tile_pen(t, hi):
        rows = jnp.where(t * per_tile + blk8 >= hi, NEG, pen_ref[t])
        return jnp.concatenate([jnp.broadcast_to(rows[u:u + 1, :], (SEL_BLOCK, tq)) for u in range(per_tile)],
                               axis=0)

    def sel_scores(t0, ntile, hi):
        off = pl.multiple_of(t0 * tq, tq)
        s = _dot(ks_ref[0, 0, pl.ds(off, ntile * tq), :], qt)
        spen = jnp.concatenate([tile_pen(t0 + u, hi) for u in range(ntile)], axis=0)
        return s + heads(lambda h: spen)

    FAR_TILES = FAR_KEYS // tq
    nk = FAR_KEYS
    n_far = jnp.maximum(qi - 1, 0)
    nblk = (n_far + FAR_TILES - 1) // FAR_TILES

    def far_t0(j):
        return jnp.maximum(n_far - FAR_TILES * (j + 1), 0)

    def far_scores(j):
        return sel_scores(far_t0(j), FAR_TILES, (n_far - FAR_TILES * j) * per_tile)

    sbuf_ref[0] = far_scores(0)
    t1 = jnp.maximum(qi - 1, 0)
    hi_all = (qi + 1) * per_tile
    s = jnp.concatenate([sel_scores(t1, 1, hi_all) + (b1 - cvec), sel_scores(qi, 1, hi_all) + (b0 - cvec)],
                        axis=0)
    vx = jnp.concatenate([vs_ref[0, 0, :, pl.ds(pl.multiple_of(t1 * tq, tq), tq)],
                          vs_ref[0, 0, :, pl.ds(pl.multiple_of(qi * tq, tq), tq)]], axis=1)
    carry = flash(init, s, vx)

    def far_v(j):
        return vs_ref[0, 0, :, pl.ds(pl.multiple_of(far_t0(jnp.maximum(j, 0)) * tq, tq), nk)]

    def far_body(j, carry):
        m, acc, alpha_prev = carry
        slot = j % 2
        s = sbuf_ref[slot]
        pv = _dot(far_v(j - 1), pbuf_ref[1 - slot])
        sbuf_ref[1 - slot] = far_scores(jnp.minimum(j + 1, nblk - 1))
        m_new = jnp.maximum(m, jnp.max(s, axis=0, keepdims=True))
        pbuf_ref[slot] = jnp.exp2(s - m_new).astype(BF16)
        return m_new, alpha_prev * acc + pv, jnp.exp2(m - m_new)

    pbuf_ref[1] = jnp.zeros(pbuf_ref.shape[1:], BF16)
    m, acc, alpha = lax.fori_loop(0, nblk, far_body, (carry[0], carry[1], jnp.ones((1, cols), F32)))
    last = jnp.where(nblk > 0, (nblk - 1) % 2, 1)
    o_s = finish((m, alpha * acc + _dot(far_v(nblk - 1), pbuf_ref[last])))

    zg = pltpu.roll(zg_ref[...].astype(F32), (LANE - 16 * g) % LANE, axis=1)
    gates = _sigmoid(zg).T
    out = (heads(lambda h: gates[h:h + 1, :]) * o_c + heads(lambda h: gates[4 + h:5 + h, :]) * o_s
           + heads(lambda h: gates[8 + h:9 + h, :]) * o_w)
    for hp in range(N_HG // 2):
        pair = jnp.concatenate([out[:, (2 * hp) * tq:(2 * hp + 1) * tq],
                                out[:, (2 * hp + 1) * tq:(2 * hp + 2) * tq]], axis=0)
        o_ref[:, hp * 2 * HEAD_DIM:(hp + 1) * 2 * HEAD_DIM] = pair.T.astype(o_ref.dtype)


def _nsa_attn(rel, qn, kc, vc, ks, vs, kw, vw, z, pat, bw, *, batch, seq, tq, ncp):
    nt = seq // tq
    nb = seq // SEL_BLOCK
    n_cmp = (seq - CMP_BLOCK) // CMP_STRIDE + 1
    ci = np.arange(ncp)[None, :] * CMP_STRIDE
    sj = np.arange(nb)[:, None] * SEL_BLOCK
    msel = ((ci <= sj + SEL_BLOCK - 1) & (ci + CMP_BLOCK - 1 >= sj) & (np.arange(ncp)[None, :] < n_cmp))
    msel_j = jnp.asarray(msel.astype(np.float32), BF16)
    assert nb <= LANE and seq >= FAR_KEYS and FAR_KEYS % tq == 0
    ks_spec = pl.BlockSpec((1, 1, seq, ks.shape[3]), lambda b, g, i: (b, g, 0, 0))
    kw_spec = pl.BlockSpec((1, 1, seq, HEAD_DIM), lambda b, g, i: (b, g, 0, 0))
    v_spec = pl.BlockSpec((1, 1, vs.shape[2], seq), lambda b, g, i: (b, g, 0, 0))
    kc_spec = pl.BlockSpec((1, 1, ncp, HEAD_DIM), lambda b, g, i: (b, g, 0, 0))
    vc_spec = pl.BlockSpec((1, 1, HEAD_DIM, ncp), lambda b, g, i: (b, g, 0, 0))
    width = N_HEADS * HEAD_DIM
    return pl.pallas_call(
        functools.partial(_nsa_attn_kernel, tq=tq, ncp=ncp, nb=nb),
        out_shape=jax.ShapeDtypeStruct((batch * seq, width), BF16),
        grid=(batch, N_KV, nt),
        in_specs=[
            pl.BlockSpec(memory_space=pltpu.SMEM),
            pl.BlockSpec((1, 1, 1, HEAD_DIM, N_HG * tq), lambda b, g, i: (b, g, i, 0, 0)),
            kc_spec, vc_spec, ks_spec, v_spec, kw_spec, v_spec,
            pl.BlockSpec((tq, LANE), lambda b, g, i: (b * nt + i, C_GATE // LANE)),
            pl.BlockSpec((N_HG, 2 * ncp, tq), lambda b, g, i: (g, 0, 0)),
            pl.BlockSpec((N_HG, 2, tq, tq), lambda b, g, i: (g, 0, 0, 0)),
            pl.BlockSpec(msel_j.shape, lambda b, g, i: (0, 0)),
        ],
        out_specs=pl.BlockSpec((tq, N_HG * HEAD_DIM), lambda b, g, i: (b * nt + i, g)),
        scratch_shapes=[pltpu.VMEM((2, FAR_KEYS, N_HG * tq), F32),
                        pltpu.VMEM((nt, 8, tq), F32),
                        pltpu.VMEM((2, FAR_KEYS, N_HG * tq), BF16)],
        compiler_params=_cparams(("arbitrary", "arbitrary", "arbitrary")),
        name="nsa_attn",
    )(rel, qn, kc, vc, ks, vs, kw, vw, z, pat, bw, msel_j)


def _merge_kernel(x_ref, yr_ref, yn_ref, zm_ref, wr_ref, wn_ref, wm_ref, o_ref):
    d = x_ref.shape[1]
    zm = zm_ref[...].astype(F32)
    br = _dot(yr_ref[...].astype(BF16), wr_ref[...])
    bn = _dot(yn_ref[...].astype(BF16), wn_ref[...])
    merged = _sigmoid(zm[:, 0:d]) * br + _sigmoid(zm[:, d:2 * d]) * bn
    o_ref[...] = x_ref[...] + _dot(merged.astype(BF16), wm_ref[...])


def _merge(x, yr, yn, z, wr, wn, wm, *, tm):
    m, d = x.shape
    row = lambda c: pl.BlockSpec((tm, c), lambda i: (i, 0))
    full = lambda a: pl.BlockSpec(a.shape, lambda i: (0, 0))
    return pl.pallas_call(
        _merge_kernel,
        out_shape=jax.ShapeDtypeStruct((m, d), F32),
        grid=(m // tm,),
        in_specs=[row(d), row(d), row(d), pl.BlockSpec((tm, 2 * d), lambda i: (i, C_M // (2 * d))),
                  full(wr), full(wn), full(wm)],
        out_specs=row(d),
        compiler_params=_cparams(("arbitrary",)),
        name="merge",
    )(x, yr, yn, z, wr, wn, wm)


def _ca_kv_kernel(mem_ref, g_ref, w_ref, kg_ref, k_ref, v_ref):
    d = mem_ref.shape[2]
    dh = d // CA_HEADS
    mn = (_rms(mem_ref[0], NORM_EPS) * g_ref[...]).astype(BF16)
    kv = _dot(mn, w_ref[...])
    ks = [(_rms(kv[:, h * dh:(h + 1) * dh], QK_EPS) * kg_ref[...]) for h in range(CA_HEADS)]
    k_ref[0] = jnp.concatenate(ks, axis=1).astype(k_ref.dtype)
    v_ref[0] = kv[:, d:2 * d].astype(v_ref.dtype)


def _ca_kv(mem, g, wkv, kg):
    batch, nm, d = mem.shape
    o_shape = jax.ShapeDtypeStruct((batch, nm, d), BF16)
    o_spec = pl.BlockSpec((1, nm, d), lambda b: (b, 0, 0))
    full = lambda a: pl.BlockSpec(a.shape, lambda b: (0, 0))
    return pl.pallas_call(
        _ca_kv_kernel,
        out_shape=(o_shape, o_shape),
        grid=(batch,),
        in_specs=[o_spec, full(g), full(wkv), full(kg)],
        out_specs=(o_spec, o_spec),
        compiler_params=_cparams(("arbitrary",)),
        name="ca_kv",
    )(mem, g, wkv, kg)


def _cross_kernel(x_ref, g_ref, wq_ref, qg_ref, k_ref, v_ref, wo_ref, o_ref):
    d = x_ref.shape[1]
    dh = d // CA_HEADS
    x = x_ref[...]
    xn = (_rms(x, NORM_EPS) * g_ref[...]).astype(BF16)
    qf = _dot(xn, wq_ref[...])
    k = k_ref[0]
    v = v_ref[0]
    outs = []
    for h in range(CA_HEADS):
        sl = slice(h * dh, (h + 1) * dh)
        qh = (_rms(qf[:, sl], QK_EPS) * qg_ref[...] * (dh ** -0.5)).astype(BF16)
        s = _dot_nt(qh, k[:, sl])
        s = s - jnp.max(s, axis=-1, keepdims=True)
        e = jnp.exp(s)
        p = e / jnp.sum(e, axis=-1, keepdims=True)
        outs.append(_dot(p.astype(BF16), v[:, sl]))
    o = jnp.concatenate(outs, axis=1).astype(BF16)
    o_ref[...] = x + _dot(o, wo_ref[...])


def _cross(h1, g, wq, qg, kn, vv, wo, *, batch, seq, tm):
    m, d = h1.shape
    nt = seq // tm
    nm = kn.shape[1]
    row = pl.BlockSpec((tm, d), lambda b, i: (b * nt + i, 0))
    full = lambda a: pl.BlockSpec(a.shape, lambda b, i: (0, 0))
    kv_spec = pl.BlockSpec((1, nm, d), lambda b, i: (b, 0, 0))
    return pl.pallas_call(
        _cross_kernel,
        out_shape=jax.ShapeDtypeStruct((m, d), F32),
        grid=(batch, nt),
        in_specs=[row, full(g), full(wq), full(qg), kv_spec, kv_spec, full(wo)],
        out_specs=row,
        compiler_params=_cparams(("arbitrary", "arbitrary")),
        name="cross",
    )(h1, g, wq, qg, kn, vv, wo)


def _ffn_kernel(x_ref, g_ref, wa_ref, wb_ref, cw_ref, cb_ref, wd_ref, o_ref, xn_ref, acc_ref, carry_ref,
                *, tiles_per_seq):
    i = pl.program_id(0)
    j = pl.program_id(1)
    nj = pl.num_programs(1)
    tm = x_ref.shape[0]

    @pl.when(j == 0)
    def _():
        xn_ref[...] = (_rms(x_ref[...], NORM_EPS) * g_ref[...]).astype(BF16)
        acc_ref[...] = jnp.zeros_like(acc_ref)

    @pl.when(i % tiles_per_seq == 0)
    def _():
        carry_ref[j] = jnp.zeros(carry_ref.shape[1:], F32)

    xn = xn_ref[...]
    a = _dot(xn, wa_ref[...])
    b = _dot(xn, wb_ref[...])
    car = carry_ref[j]
    rowi = lax.broadcasted_iota(jnp.int32, a.shape, 0)
    p1 = jnp.where(rowi == 0, car[7:8, :], pltpu.roll(a, 1, axis=0))
    p2 = jnp.where(rowi == 0, car[6:7, :], jnp.where(rowi == 1, car[7:8, :], pltpu.roll(a, 2, axis=0)))
    carry_ref[j] = a[tm - 8:tm, :]
    cw = cw_ref[0]
    conv = cw[0:1, :] * p2 + cw[1:2, :] * p1 + cw[2:3, :] * a + cb_ref[0]
    act = conv * _sigmoid(conv) * b
    acc_ref[...] += _dot(act.astype(BF16), wd_ref[...])

    @pl.when(j == nj - 1)
    def _():
        o_ref[...] = x_ref[...] + acc_ref[...]


def _ffn(h2, g, wup, cw, cb, wd, *, seq, tm, tn):
    m, d = h2.shape
    dff = wd.shape[0]
    nj = dff // tn
    cw3 = jnp.zeros((nj, 8, tn), F32).at[:, 0:CONV_W, :].set(cw.reshape(CONV_W, nj, tn).transpose(1, 0, 2))
    cb3 = cb.reshape(nj, 1, tn)
    return pl.pallas_call(
        functools.partial(_ffn_kernel, tiles_per_seq=seq // tm),
        out_shape=jax.ShapeDtypeStruct((m, d), F32),
        grid=(m // tm, nj),
        in_specs=[
            pl.BlockSpec((tm, d), lambda i, j: (i, 0)),
            pl.BlockSpec((1, d), lambda i, j: (0, 0)),
            pl.BlockSpec((d, tn), lambda i, j: (0, j)),
            pl.BlockSpec((d, tn), lambda i, j: (0, nj + j)),
            pl.BlockSpec((1, 8, tn), lambda i, j: (j, 0, 0)),
            pl.BlockSpec((1, 1, tn), lambda i, j: (j, 0, 0)),
            pl.BlockSpec((tn, d), lambda i, j: (j, 0)),
        ],
        out_specs=pl.BlockSpec((tm, d), lambda i, j: (i, 0)),
        scratch_shapes=[pltpu.VMEM((tm, d), BF16), pltpu.VMEM((tm, d), F32), pltpu.VMEM((nj, 8, tn), F32)],
        compiler_params=_cparams(("arbitrary", "arbitrary")),
        name="ffn",
    )(h2, g, wup, wup, cw3, cb3, wd)


def _pack_perm(width):
    o_zw = 3 * width
    o_za = o_zw + R_LORA_W
    o_zg = o_za + R_LORA_A
    o_q = o_zg + R_LORA_G
    nkv = N_KV * HEAD_DIM
    o_kc = o_q + N_HEADS * HEAD_DIM
    o_gate = o_kc + 6 * nkv
    o_m = o_gate + 3 * N_HEADS
    perm = np.full((Z_COLS,), -1, np.int64)
    perm[C_RKV:C_RKV + 3 * width] = np.arange(3 * width)
    perm[C_Q:C_Q + N_HEADS * HEAD_DIM] = o_q + np.arange(N_HEADS * HEAD_DIM)
    perm[C_KC:C_KC + 6 * nkv] = o_kc + np.arange(6 * nkv)
    perm[C_ZG:C_ZG + R_LORA_G] = o_zg + np.arange(R_LORA_G)
    perm[C_ZWA:C_ZWA + R_LORA_W + R_LORA_A] = o_zw + np.arange(R_LORA_W + R_LORA_A)
    for g in range(N_KV):
        for c in range(3):
            for h in range(N_HG):
                perm[C_GATE + 16 * g + 4 * c + h] = o_gate + (g * N_HG + h) * 3 + c
    perm[C_M:C_M + 2 * width] = o_m + np.arange(2 * width)
    return perm, o_m + 2 * width


def _pad_rows(w, rows, offset=0):
    out = jnp.zeros((rows, w.shape[1]), w.dtype)
    return out.at[offset:offset + w.shape[0]].set(w)


def kernel(x, mem, rel_bias, norm_mix, w_in, rwkv_mu, rwkv_w0, rwkv_w2, rwkv_a0, rwkv_a2, rwkv_g2,
           rwkv_kk, rwkv_ka, rwkv_rk, rwkv_lnx_w, rwkv_lnx_b, nsa_q_gain, nsa_k_gain, cmp_pe_k, cmp_pe_v,
           cmp_w1_k, cmp_w2_k, cmp_w1_v, cmp_w2_v, w_branch_rwkv, w_branch_nsa, w_mix_out,
           norm_cross, norm_mem, ca_wq, ca_wkv, ca_q_gain, ca_k_gain, ca_wo,
           norm_ffn, ffn_up, ffn_conv, ffn_conv_b, ffn_down):
    batch, seq, d = x.shape
    depth = w_in.shape[0]
    width = d
    assert d == N_HEADS * HEAD_DIM and 2 * (Z_COLS - C_M) == 4 * d
    perm, in_cols = _pack_perm(width)
    assert w_in.shape[2] == in_cols
    perm_j = jnp.asarray(np.maximum(perm, 0), jnp.int32)
    keep = jnp.asarray(perm >= 0)

    tq = 256
    assert seq % 256 == 0 and tq > REL_MAX_DIST - 1 and WINDOW % tq == 0
    ncp = -(-(seq // CMP_STRIDE) // LANE) * LANE
    n_cmp = (seq - CMP_BLOCK) // CMP_STRIDE + 1
    tm = min(1024, seq)
    row = lambda v: v.reshape(1, -1).astype(F32)
    nkv = N_KV * HEAD_DIM

    h = x.reshape(batch * seq, d)
    for l in range(depth):
        w_in_p = jnp.where(keep[None, :], jnp.take(w_in[l], perm_j, axis=1), 0.0).astype(BF16)
        mu = rwkv_mu[l]
        mu_r = row(mu[0:3 * width])
        mu_wa = row(mu[3 * width:3 * width + R_LORA_W + R_LORA_A])
        mu_g = row(jnp.zeros((256,), F32).at[0:R_LORA_G].set(mu[3 * width + R_LORA_W + R_LORA_A:]))
        w2p = _pad_rows(rwkv_w2[l], LANE, 0).astype(BF16)
        a2p = _pad_rows(rwkv_a2[l], LANE, R_LORA_W).astype(BF16)
        g2p = _pad_rows(rwkv_g2[l], 256, 0).astype(BF16)

        z = _norm_matmul(h, row(norm_mix[l]), w_in_p, tm=tm, tn=1024, out_dtype=BF16, name="in_proj")

        y_r = _rwkv(z, mu_r, mu_g, mu_wa, row(rwkv_w0[l]), w2p, row(rwkv_a0[l]), a2p, g2p,
                    row(rwkv_kk[l]), row(rwkv_ka[l]), row(rwkv_rk[l]), row(rwkv_lnx_w[l]),
                    row(rwkv_lnx_b[l]), batch=batch, seq=seq, width=width)

        qn, ks, vs, kw, vw = _nsa_prep(
            z, row(jnp.tile(nsa_q_gain[l], N_HEADS)), row(jnp.tile(nsa_k_gain[l, 1], N_KV)),
            row(jnp.tile(nsa_k_gain[l, 2], N_KV)), batch=batch, seq=seq, tq=tq)
        z3 = z.reshape(batch, seq // CMP_STRIDE, CMP_STRIDE, Z_COLS)

        def groups(c0):
            t = z3[:, :, :, c0:c0 + nkv].reshape(batch, seq // CMP_STRIDE, CMP_STRIDE, N_KV, HEAD_DIM)
            return t.transpose(0, 3, 1, 2, 4).reshape(batch, N_KV, seq // CMP_STRIDE, CMP_STRIDE * HEAD_DIM)

        kc, vc = _nsa_cmp(groups(C_KC), groups(C_KC + nkv), cmp_pe_k[l].reshape(1, -1), cmp_pe_v[l].reshape(1, -1),
                          cmp_w1_k[l].astype(BF16), cmp_w2_k[l].astype(BF16), cmp_w1_v[l].astype(BF16),
                          cmp_w2_v[l].astype(BF16), row(nsa_k_gain[l, 0]), ncp=ncp, n_cmp=n_cmp)
        bw, pat = _nsa_bias(rel_bias.astype(F32), tq=tq, ncp=ncp)
        y_n = _nsa_attn(rel_bias.astype(F32), qn, kc, vc, ks, vs, kw, vw, z, pat, bw,
                        batch=batch, seq=seq, tq=tq, ncp=ncp)

        h1 = _merge(h, y_r, y_n, z, w_branch_rwkv[l].astype(BF16), w_branch_nsa[l].astype(BF16),
                    w_mix_out[l].astype(BF16), tm=min(512, seq))

        kn, vv = _ca_kv(mem, row(norm_mem[l]), ca_wkv[l].astype(BF16), row(ca_k_gain[l]))
        h2 = _cross(h1, row(norm_cross[l]), ca_wq[l].astype(BF16), row(ca_q_gain[l]), kn, vv,
                    ca_wo[l].astype(BF16), batch=batch, seq=seq, tm=min(512, seq))

        h = _ffn(h2, row(norm_ffn[l]), ffn_up[l].astype(BF16), ffn_conv[l], ffn_conv_b[l],
                 ffn_down[l].astype(BF16), seq=seq, tm=tm, tn=256)
    return h.reshape(batch, seq, d)
```

```python
import functools
import math

import jax
import jax.numpy as jnp
import numpy as np
from jax import lax
from jax.experimental import pallas as pl
from jax.experimental.pallas import tpu as pltpu

F32 = jnp.float32
BF16 = jnp.bfloat16
NEG = -1e30
LOG2E = 1.4426950408889634
FAR_KEYS = 512
T_DIAG, T_NEAR, T_NONE, T_FAR, T_EDGE, T_DIAG_REL, T_NEAR_REL = range(7)
N_TABLES = 7
V_ROWS = 80

R_HEAD = 64
R_LORA_W = 64
R_LORA_A = 64
R_LORA_G = 160
LNX_EPS = 64e-5
N_HEADS = 16
N_KV = 4
N_HG = N_HEADS // N_KV
HEAD_DIM = 64
CMP_BLOCK = 32
CMP_STRIDE = 16
CMP_HIDDEN = 256
SEL_BLOCK = 64
SEL_TOPK = 16
WINDOW = 512
FORCE_SCORE = 1e4
REL_BUCKETS = 32
REL_MAX_DIST = 128
CA_HEADS = 4
CONV_W = 3
NORM_EPS = 1e-6
QK_EPS = 1e-6

LANE = 128
VMEM_LIMIT = 56 * 1024 * 1024

C_RKV = 0
C_Q = 3072
C_KC = 4096
C_ZG = 5632
C_ZWA = 5888
C_GATE = 6016
C_M = 6144
Z_COLS = 8192


def _dot(a, b):
    return jnp.dot(a, b, preferred_element_type=F32)


def _dot_nt(a, b):
    return lax.dot_general(a, b, (((1,), (1,)), ((), ())), preferred_element_type=F32)


def _split_dot(x, e, parts, mode="xe"):
    acc = None
    rem = x
    for i in range(parts):
        hi = rem.astype(BF16)
        t = _dot(hi, e) if mode == "xe" else (_dot(e, hi) if mode == "ex" else _dot_nt(e, hi))
        acc = t if acc is None else acc + t
        if i + 1 < parts:
            rem = rem - hi.astype(F32)
    return acc


def _rms(x, eps):
    return x * lax.rsqrt(jnp.mean(x * x, axis=-1, keepdims=True) + eps)


def _sigmoid(x):
    return 1.0 / (1.0 + jnp.exp(-x))


def _cparams(sem):
    return pltpu.CompilerParams(dimension_semantics=sem, vmem_limit_bytes=VMEM_LIMIT)


def _norm_matmul_kernel(x_ref, g_ref, w_ref, o_ref, xn_ref):
    @pl.when(pl.program_id(1) == 0)
    def _():
        x = x_ref[...]
        xn_ref[...] = (_rms(x, NORM_EPS) * g_ref[...]).astype(BF16)

    o_ref[...] = _dot(xn_ref[...], w_ref[...]).astype(o_ref.dtype)


def _norm_matmul(x, g, w, *, tm, tn, out_dtype, name):
    m, k = x.shape
    n = w.shape[1]
    return pl.pallas_call(
        _norm_matmul_kernel,
        out_shape=jax.ShapeDtypeStruct((m, n), out_dtype),
        grid=(m // tm, n // tn),
        in_specs=[
            pl.BlockSpec((tm, k), lambda i, j: (i, 0)),
            pl.BlockSpec((1, k), lambda i, j: (0, 0)),
            pl.BlockSpec((k, tn), lambda i, j: (0, j)),
        ],
        out_specs=pl.BlockSpec((tm, tn), lambda i, j: (i, j)),
        scratch_shapes=[pltpu.VMEM((tm, k), BF16)],
        compiler_params=_cparams(("arbitrary", "arbitrary")),
        name=name,
    )(x, g, w)


def _rwkv_kernel(zr_ref, zg_ref, zwa_ref, mur_ref, mug_ref, muwa_ref, w0_ref, w2_ref, a0_ref, a2_ref,
                 g2_ref, kk_ref, ka_ref, rk_ref, lnw_ref, lnb_ref, eh_ref, eht_ref, tri_ref,
                 o_ref, pr_ref, pg_ref, pwa_ref, st_ref, *, L, width):
    i = pl.program_id(1)
    npair = width // LANE
    tb = o_ref.shape[0]

    @pl.when(i == 0)
    def _():
        pr_ref[...] = jnp.zeros_like(pr_ref)
        pg_ref[...] = jnp.zeros_like(pg_ref)
        pwa_ref[...] = jnp.zeros_like(pwa_ref)
        st_ref[...] = jnp.zeros_like(st_ref)

    def shifted8(z, prev_ref, mu):
        rolled = pltpu.roll(z, 1, axis=0)
        row0 = lax.broadcasted_iota(jnp.int32, z.shape, 0) == 0
        prev = jnp.where(row0, prev_ref[7:8, :], rolled)
        prev_ref[...] = z[tb - 8:tb, :]
        return z + (prev - z) * mu

    zs = shifted8(zr_ref[...].astype(F32), pr_ref, mur_ref[...])
    zsg = shifted8(zg_ref[...].astype(F32), pg_ref, mug_ref[...])
    zswa = shifted8(zwa_ref[...].astype(F32), pwa_ref, muwa_ref[...])

    r = zs[:, 0:width]
    k = zs[:, width:2 * width]
    v = zs[:, 2 * width:3 * width]

    w_lin = w0_ref[...] + _dot(jnp.tanh(zswa).astype(BF16), w2_ref[...])
    a_lin = a0_ref[...] + _dot(zswa.astype(BF16), a2_ref[...])
    y = -w_lin
    softplus = jnp.maximum(y, 0.0) + jnp.log(1.0 + jnp.exp(-jnp.abs(y)))
    ld = -jnp.exp(-softplus - 0.5)
    a = _sigmoid(a_lin)
    g = _dot(_sigmoid(zsg).astype(BF16), g2_ref[...])

    eh = eh_ref[...]
    eht = eht_ref[...]

    def sum_heads(t):
        return _split_dot(t, eh, 2)

    def bcast_heads(t):
        return _split_dot(t, eht, 2)

    kkr = k * kk_ref[...]
    nrm = jnp.maximum(jnp.sqrt(sum_heads(kkr * kkr)), 1e-12)
    kkn = kkr * bcast_heads(1.0 / nrm)
    k2 = k * (1.0 + (a - 1.0) * ka_ref[...])
    av = -kkn
    bv = kkn * a
    bonus = bcast_heads(sum_heads(r * k2 * rk_ref[...])) * v

    nch = tb // L
    lg = _split_dot(ld, tri_ref[...], 3, mode="ex")
    lasts = [lg[(c + 1) * L - 1:(c + 1) * L, :] for c in range(nch)]
    lg_last = jnp.concatenate([jnp.broadcast_to(t, (L, width)) for t in lasts], axis=0)
    eg = jnp.exp(lg)
    eng = jnp.exp(-lg)
    egl = jnp.exp(lg_last - lg)
    rt = r * eg
    kt = k2 * eng
    bt = bv * eng
    at = av * jnp.exp(lg - ld)
    kgl = k2 * egl
    bgl = bv * egl
    gl = [jnp.exp(t) for t in lasts]

    lane = lax.broadcasted_iota(jnp.int32, (L, LANE), 1)
    lo = lane < R_HEAD

    def stack(t):
        return jnp.concatenate([jnp.where(lo, t, 0.0), jnp.where(lo, 0.0, t)], axis=0)

    row = lax.broadcasted_iota(jnp.int32, (2 * L, 2 * L), 0)
    col = lax.broadcasted_iota(jnp.int32, (2 * L, 2 * L), 1)
    strict = row > col
    incl = row >= col
    eye = (row == col).astype(F32)
    nsq = int(math.log2(L)) - 1

    pairs = range(npair)
    chains = [(c, p) for c in range(nch) for p in pairs]
    sub = lambda t, c, p: t[c * L:(c + 1) * L, p * LANE:(p + 1) * LANE]
    bf = lambda t: t.astype(BF16)
    ar = {cp: bf(jnp.concatenate([stack(sub(at, *cp)), stack(sub(rt, *cp))], axis=0)) for cp in chains}
    bk = {cp: bf(jnp.concatenate([stack(sub(bt, *cp)), stack(sub(kt, *cp))], axis=0)) for cp in chains}
    v_s = {cp: bf(stack(sub(v, *cp))) for cp in chains}
    gm = {cp: _dot_nt(ar[cp], bk[cp]) for cp in chains}
    a_ab = {cp: jnp.where(strict, gm[cp][0:2 * L, 0:2 * L], 0.0) for cp in chains}
    a_ak = {cp: bf(jnp.where(strict, gm[cp][0:2 * L, 2 * L:4 * L], 0.0)) for cp in chains}
    a_rb = {cp: bf(jnp.where(incl, gm[cp][2 * L:4 * L, 0:2 * L], 0.0)) for cp in chains}
    a_rk = {cp: bf(jnp.where(incl, gm[cp][2 * L:4 * L, 2 * L:4 * L], 0.0)) for cp in chains}
    tinv = {cp: eye + a_ab[cp] for cp in chains}
    pw = a_ab
    for _ in range(nsq):
        pwb = {cp: bf(pw[cp]) for cp in chains}
        pw = {cp: _dot(pwb[cp], pwb[cp]) for cp in chains}
        tinv = {cp: tinv[cp] + _dot(bf(pw[cp]), bf(tinv[cp])) for cp in chains}
    tinv = {cp: bf(tinv[cp]) for cp in chains}
    bkgt = {cp: bf(jnp.concatenate([stack(sub(bgl, *cp)), stack(sub(kgl, *cp))], axis=0).T) for cp in chains}
    glcol = {(c, p): jnp.sum(eye * gl[c][:, p * LANE:(p + 1) * LANE], axis=1, keepdims=True)
             for (c, p) in chains}
    h = [st_ref[p] for p in pairs]
    ys = []
    for c in range(nch):
        hb = [bf(h[p]) for p in pairs]
        wmat = [_dot(jnp.concatenate([ar[c, p][0:2 * L], a_ak[c, p]], axis=1),
                     jnp.concatenate([hb[p], v_s[c, p]], axis=0)) for p in pairs]
        ub = [bf(_dot(tinv[c, p], bf(wmat[p]))) for p in pairs]
        yy = [_dot(jnp.concatenate([ar[c, p][2 * L:4 * L], a_rb[c, p], a_rk[c, p]], axis=1),
                   jnp.concatenate([hb[p], ub[p], v_s[c, p]], axis=0)) for p in pairs]
        h = [h[p] * glcol[c, p] + _dot(bkgt[c, p], jnp.concatenate([ub[p], v_s[c, p]], axis=0))
             for p in pairs]
        ys.append(jnp.concatenate([yy[p][0:L] + yy[p][L:2 * L] for p in pairs], axis=1))
    for p in pairs:
        st_ref[p] = h[p]

    yv = jnp.concatenate(ys, axis=0)
    inv_n = 1.0 / R_HEAD
    mean = bcast_heads(sum_heads(yv) * inv_n)
    yc = yv - mean
    var = bcast_heads(sum_heads(yc * yc) * inv_n)
    yn = yc * lax.rsqrt(var + LNX_EPS) * lnw_ref[...] + lnb_ref[...]
    o_ref[...] = ((yn + bonus) * g).astype(o_ref.dtype)


def _rwkv(z, mu_r, mu_g, mu_wa, w0, w2p, a0, a2p, g2p, kk, ka, rk, lnw, lnb, *, batch, seq, width):
    L = 64
    tb = 4 * L
    nt = seq // tb
    nheads = width // R_HEAD
    eh = np.zeros((width, LANE), np.float32)
    eh[np.arange(width), np.arange(width) // R_HEAD] = 1.0
    ti = np.arange(tb)
    tri = ((ti[:, None] >= ti[None, :]) & (ti[:, None] // L == ti[None, :] // L)).astype(np.float32)
    vec = lambda c: pl.BlockSpec((1, c), lambda b, i: (0, 0))
    full = lambda a: pl.BlockSpec(a.shape, lambda b, i: (0,) * a.ndim)
    eh_j = jnp.asarray(eh, BF16)
    eht_j = jnp.asarray(eh.T, BF16)
    tri_j = jnp.asarray(tri, BF16)
    return pl.pallas_call(
        functools.partial(_rwkv_kernel, L=L, width=width),
        out_shape=jax.ShapeDtypeStruct((batch * seq, width), BF16),
        grid=(batch, nt),
        in_specs=[
            pl.BlockSpec((tb, 3 * width), lambda b, i: (b * nt + i, C_RKV // (3 * width))),
            pl.BlockSpec((tb, 256), lambda b, i: (b * nt + i, C_ZG // 256)),
            pl.BlockSpec((tb, LANE), lambda b, i: (b * nt + i, C_ZWA // LANE)),
            vec(3 * width), vec(256), vec(LANE),
            vec(width), full(w2p), vec(width), full(a2p), full(g2p),
            vec(width), vec(width), vec(width), vec(width), vec(width),
            full(eh_j), full(eht_j), full(tri_j),
        ],
        out_specs=pl.BlockSpec((tb, width), lambda b, i: (b * nt + i, 0)),
        scratch_shapes=[
            pltpu.VMEM((8, 3 * width), F32),
            pltpu.VMEM((8, 256), F32),
            pltpu.VMEM((8, LANE), F32),
            pltpu.VMEM((nheads // 2, 2 * R_HEAD, 2 * R_HEAD), F32),
        ],
        compiler_params=_cparams(("arbitrary", "arbitrary")),
        name="rwkv",
    )(z, z, z, mu_r, mu_g, mu_wa, w0, w2p, a0, a2p, g2p, kk, ka, rk, lnw, lnb, eh_j, eht_j, tri_j)


def _nsa_prep_kernel(zq_ref, zs_ref, zw_ref, qg_ref, kgs_ref, kgw_ref, ehq_ref, ehqt_ref,
                     q_ref, ks_ref, vs_ref, kw_ref, vw_ref):
    inv_d = 1.0 / HEAD_DIM

    def head_norm(x, gain, e, et):
        ss = _split_dot(x * x, e, 2)
        inv = lax.rsqrt(ss * inv_d + QK_EPS)
        return x * _split_dot(inv, et, 2) * gain

    nk = N_KV * HEAD_DIM
    ehq = ehq_ref[...]
    ehqt = ehqt_ref[...]
    q = head_norm(zq_ref[...].astype(F32), qg_ref[...], ehq, ehqt) * (HEAD_DIM ** -0.5 * LOG2E)
    qt = q.T
    for gq in range(N_KV):
        rows = [qt[(gq * N_HG + h) * HEAD_DIM:(gq * N_HG + h + 1) * HEAD_DIM, :] for h in range(N_HG)]
        q_ref[0, gq, 0] = jnp.concatenate(rows, axis=1).astype(q_ref.dtype)
    zs = zs_ref[...].astype(F32)
    zw = zw_ref[...].astype(F32)
    ehk = ehq[0:nk, :]
    ehkt = ehqt[:, 0:nk]
    ksn = head_norm(zs[:, 0:nk], kgs_ref[...], ehk, ehkt)
    kwn = head_norm(zw[:, 0:nk], kgw_ref[...], ehk, ehkt)
    vst = zs[:, nk:2 * nk].T
    vwt = zw[:, nk:2 * nk].T
    tb = zs.shape[0]
    vextra = jnp.where(lax.broadcasted_iota(jnp.int32, (V_ROWS - HEAD_DIM, tb), 0) == 0, 1.0, 0.0)
    for gq in range(N_KV):
        sl = slice(gq * HEAD_DIM, (gq + 1) * HEAD_DIM)
        ks_ref[0, gq] = ksn[:, sl].astype(ks_ref.dtype)
        vs_ref[0, gq] = jnp.concatenate([vst[sl, :], vextra], axis=0).astype(vs_ref.dtype)
        kw_ref[0, gq] = kwn[:, sl].astype(kw_ref.dtype)
        vw_ref[0, gq] = jnp.concatenate([vwt[sl, :], vextra], axis=0).astype(vw_ref.dtype)


def _nsa_prep(z, q_gain_t, kgs_t, kgw_t, *, batch, seq, tq):
    tb = tq
    nt = seq // tb
    nq = N_HEADS * HEAD_DIM
    nk = N_KV * HEAD_DIM
    eh = np.zeros((nq, LANE), np.float32)
    eh[np.arange(nq), np.arange(nq) // HEAD_DIM] = 1.0
    eh_j = jnp.asarray(eh, BF16)
    eht_j = jnp.asarray(eh.T, BF16)
    k_shape = jax.ShapeDtypeStruct((batch, N_KV, seq, HEAD_DIM), BF16)
    k_spec = pl.BlockSpec((1, N_KV, tb, HEAD_DIM), lambda b, i: (b, 0, i, 0))
    v_shape = jax.ShapeDtypeStruct((batch, N_KV, V_ROWS, seq), BF16)
    v_spec = pl.BlockSpec((1, N_KV, V_ROWS, tb), lambda b, i: (b, 0, 0, i))
    return pl.pallas_call(
        _nsa_prep_kernel,
        out_shape=(jax.ShapeDtypeStruct((batch, N_KV, nt, HEAD_DIM, N_HG * tq), BF16),
                   k_shape, v_shape, k_shape, v_shape),
        grid=(batch, nt),
        in_specs=[
            pl.BlockSpec((tb, nq), lambda b, i: (b * nt + i, C_Q // nq)),
            pl.BlockSpec((tb, 2 * nk), lambda b, i: (b * nt + i, C_KC // (2 * nk) + 1)),
            pl.BlockSpec((tb, 2 * nk), lambda b, i: (b * nt + i, C_KC // (2 * nk) + 2)),
            pl.BlockSpec((1, nq), lambda b, i: (0, 0)),
            pl.BlockSpec((1, nk), lambda b, i: (0, 0)),
            pl.BlockSpec((1, nk), lambda b, i: (0, 0)),
            pl.BlockSpec(eh_j.shape, lambda b, i: (0, 0)),
            pl.BlockSpec(eht_j.shape, lambda b, i: (0, 0)),
        ],
        out_specs=(pl.BlockSpec((1, N_KV, 1, HEAD_DIM, N_HG * tq), lambda b, i: (b, 0, i, 0, 0)),
                   k_spec, v_spec, k_spec, v_spec),
        compiler_params=_cparams(("arbitrary", "arbitrary")),
        name="nsa_prep",
    )(z, z, z, q_gain_t, kgs_t, kgw_t, eh_j, eht_j)


def _nsa_cmp_kernel(ak_ref, av_ref, pek_ref, pev_ref, w1k_ref, w2k_ref, w1v_ref, w2v_ref, kg_ref,
                    kc_ref, vc_ref, *, n_cmp):
    ncp = kc_ref.shape[2]
    half = w1k_ref.shape[0] // 2

    def compress(a_ref, pe_ref, w1_ref, w2_ref):
        rows = a_ref.shape[2]
        a = a_ref[0, 0].astype(BF16)
        p1 = _dot(a, w1_ref[0:half, :])
        p2 = _dot(a, w1_ref[half:2 * half, :])
        pe8 = jnp.broadcast_to(pe_ref[...], (8, pe_ref.shape[1])).astype(BF16)
        pe_bias = _dot(pe8, w1_ref[...])[0:1, :]
        hpre = p1 + pltpu.roll(p2, rows - 1, axis=0) + pe_bias
        hid = 0.5 * hpre * (1.0 + jnp.tanh(math.sqrt(2.0 / math.pi) * (hpre + 0.044715 * hpre * hpre * hpre)))
        out = _dot(hid.astype(BF16), w2_ref[...])
        if rows < ncp:
            out = jnp.concatenate([out, jnp.zeros((ncp - rows, out.shape[1]), F32)], axis=0)
        return out

    valid = lax.broadcasted_iota(jnp.int32, (ncp, HEAD_DIM), 0) < n_cmp
    kc = compress(ak_ref, pek_ref, w1k_ref, w2k_ref)
    kc = _rms(kc, QK_EPS) * kg_ref[...]
    kc_ref[0, 0] = jnp.where(valid, kc, 0.0).astype(kc_ref.dtype)
    vc = jnp.where(valid, compress(av_ref, pev_ref, w1v_ref, w2v_ref), 0.0)
    vct = jnp.concatenate([vc, jnp.zeros((ncp, LANE - HEAD_DIM), F32)], axis=1).T
    vc_ref[0, 0] = vct[0:HEAD_DIM, :].astype(vc_ref.dtype)


def _nsa_cmp(ak, av, pek, pev, w1k, w2k, w1v, w2v, kg, *, ncp, n_cmp):
    batch, ng, rows, wid = ak.shape
    a_spec = pl.BlockSpec((1, 1, rows, wid), lambda b, g: (b, g, 0, 0))
    full = lambda a: pl.BlockSpec(a.shape, lambda b, g: (0,) * a.ndim)
    k_shape = jax.ShapeDtypeStruct((batch, ng, ncp, HEAD_DIM), BF16)
    k_spec = pl.BlockSpec((1, 1, ncp, HEAD_DIM), lambda b, g: (b, g, 0, 0))
    v_shape = jax.ShapeDtypeStruct((batch, ng, HEAD_DIM, ncp), BF16)
    v_spec = pl.BlockSpec((1, 1, HEAD_DIM, ncp), lambda b, g: (b, g, 0, 0))
    return pl.pallas_call(
        functools.partial(_nsa_cmp_kernel, n_cmp=n_cmp),
        out_shape=(k_shape, v_shape),
        grid=(batch, ng),
        in_specs=[a_spec, a_spec, full(pek), full(pev), full(w1k), full(w2k), full(w1v), full(w2v), full(kg)],
        out_specs=(k_spec, v_spec),
        compiler_params=_cparams(("arbitrary", "arbitrary")),
        name="nsa_cmp",
    )(ak, av, pek, pev, w1k, w2k, w1v, w2v, kg)


def _bucket_thresholds():
    exact = REL_BUCKETS // 2
    n = np.arange(0, REL_MAX_DIST + 1)
    nf = np.maximum(n, 1).astype(np.float64)
    large = exact + (np.log(nf / exact) / math.log(REL_MAX_DIST / exact) * (REL_BUCKETS - exact)).astype(np.int64)
    large = np.minimum(large, REL_BUCKETS - 1)
    bucket = np.where(n < exact, n, large)
    assert bucket[-1] == REL_BUCKETS - 1
    return [int(np.argmax(bucket >= b)) for b in range(REL_BUCKETS)]


def _nsa_bias_kernel(rel_ref, bw_ref, pat_ref, *, tq, ncp):
    h = pl.program_id(0)
    thr = _bucket_thresholds()

    def bias_of(d):
        val = jnp.full(d.shape, rel_ref[0, h], F32)
        for b in range(1, REL_BUCKETS):
            val = jnp.where(d >= thr[b], rel_ref[b, h], val)
        return jnp.where(d < 0, NEG, val * LOG2E)

    kk = lax.broadcasted_iota(jnp.int32, (tq, tq), 0)
    qq = lax.broadcasted_iota(jnp.int32, (tq, tq), 1)
    far = rel_ref[REL_BUCKETS - 1, h] * LOG2E
    diag = bias_of(qq - kk)
    near = bias_of(qq - kk + tq)
    bw_ref[0, T_DIAG] = diag
    bw_ref[0, T_NEAR] = near
    bw_ref[0, T_NONE] = jnp.full((tq, tq), NEG, F32)
    bw_ref[0, T_FAR] = jnp.full((tq, tq), far, F32)
    bw_ref[0, T_EDGE] = jnp.where(kk > qq, far, NEG)
    bw_ref[0, T_DIAG_REL] = jnp.where(qq >= kk, diag - far, NEG)
    bw_ref[0, T_NEAR_REL] = near - far
    cc = lax.broadcasted_iota(jnp.int32, (2 * ncp, tq), 0)
    qc = lax.broadcasted_iota(jnp.int32, (2 * ncp, tq), 1)
    pat_ref[0] = bias_of(qc - CMP_STRIDE * (cc - ncp) - (CMP_BLOCK - 1))


def _nsa_bias(rel, *, tq, ncp):
    return pl.pallas_call(
        functools.partial(_nsa_bias_kernel, tq=tq, ncp=ncp),
        out_shape=(jax.ShapeDtypeStruct((N_HEADS, N_TABLES, tq, tq), F32),
                   jax.ShapeDtypeStruct((N_HEADS, 2 * ncp, tq), F32)),
        grid=(N_HEADS,),
        in_specs=[pl.BlockSpec(memory_space=pltpu.SMEM)],
        out_specs=(pl.BlockSpec((1, N_TABLES, tq, tq), lambda h: (h, 0, 0, 0)),
                   pl.BlockSpec((1, 2 * ncp, tq), lambda h: (h, 0, 0))),
        compiler_params=_cparams(("arbitrary",)),
        name="nsa_bias",
    )(rel)


def _nsa_attn_kernel(q_ref, kc_ref, vc_ref, ks_ref, vs_ref, kw_ref, vw_ref, zg_ref, pat_ref,
                     bw_ref, msel_ref, o_ref, pen_ref, pbuf_ref, *, tq, ncp, nb):
    g = pl.program_id(1)
    qi = pl.program_id(2)
    cols = N_HG * tq
    qt = q_ref[0, 0, 0]

    def heads(fn):
        return jnp.concatenate([fn(h) for h in range(N_HG)], axis=1)

    def table(which, ok=None):
        idx = which if ok is None else jnp.where(ok, which, T_NONE)
        return heads(lambda h: bw_ref[h, idx])

    vrows = vs_ref.shape[2]

    def flash(carry, s, vt):
        m, acc = carry
        m_new = jnp.maximum(m, jnp.max(s, axis=0, keepdims=True))
        p = jnp.exp2(s - m_new)
        acc = jnp.exp2(m - m_new) * acc + _dot(vt, p.astype(BF16))
        return m_new, acc

    init = (jnp.full((1, cols), NEG, F32), jnp.zeros((vrows, cols), F32))

    def finish(carry):
        acc = carry[1]
        return acc[0:HEAD_DIM] / acc[HEAD_DIM:HEAD_DIM + 1]

    s_c = _dot(kc_ref[0, 0], qt)
    start = pl.multiple_of(ncp - qi * (tq // CMP_STRIDE), tq // CMP_STRIDE)
    bias_c = heads(lambda h: pat_ref[h, pl.ds(start, ncp), :])

    nd = WINDOW // tq
    s_parts, v_parts = [], []
    for d in range(nd, -1, -1):
        off = pl.multiple_of(jnp.maximum(qi - d, 0) * tq, tq)
        s = _dot(kw_ref[0, 0, pl.ds(off, tq), :], qt)
        if d == 0:
            s = s + table(T_DIAG)
        elif d == nd:
            s = s + table(T_EDGE, qi >= d)
        elif d == 1:
            s = s + table(T_NEAR, qi >= d)
        else:
            s = s + table(T_FAR, qi >= d)
        s_parts.append(s)
        v_parts.append(vw_ref[0, 0, :, pl.ds(off, tq)])
    o_w = finish(flash(init, jnp.concatenate(s_parts, axis=0), jnp.concatenate(v_parts, axis=1)))

    valid_c = bias_c > 0.5 * NEG
    sc = s_c + bias_c
    m_c = jnp.max(sc, axis=0, keepdims=True)
    e_c = jnp.where(valid_c, jnp.exp2(sc - m_c), 0.0)
    l_c = jnp.sum(e_c, axis=0, keepdims=True)
    p_c = e_c / jnp.where(l_c > 0.0, l_c, 1.0)
    o_c = _dot(vc_ref[0, 0], p_c.astype(BF16))

    psum = p_c[:, 0:tq]
    for h in range(1, N_HG):
        psum = psum + p_c[:, h * tq:(h + 1) * tq]
    imp_t = _split_dot(psum, msel_ref[...], 3, mode="ex")
    jj = lax.broadcasted_iota(jnp.int32, (nb, tq), 0)
    tt = qi * tq + lax.broadcasted_iota(jnp.int32, (nb, tq), 1)
    cur = tt // SEL_BLOCK
    forced = (jj == 0) | (jj == cur) | (jj == cur - 1)
    valid_b = jj * SEL_BLOCK <= tt
    imp = jnp.where(valid_b, jnp.where(forced, FORCE_SCORE, imp_t), -jnp.inf)
    groups8 = [imp[r:r + 8, :] for r in range(0, nb, 8)]
    jj8 = lax.broadcasted_iota(jnp.int32, (8, tq), 0)
    cnts = [jnp.zeros((8, tq), F32) for _ in groups8]
    for i in range(nb):
        ri = jnp.broadcast_to(imp[i:i + 1, :], (8, tq))
        for r, grp in enumerate(groups8):
            if 8 * r > i:
                beats = ri >= grp
            elif 8 * r + 7 <= i:
                beats = ri > grp
            else:
                beats = jnp.where(jj8 + 8 * r > i, jnp.where(ri >= grp, 1.0, 0.0), jnp.where(ri > grp, 1.0, 0.0)) > 0.5
            cnts[r] = cnts[r] + jnp.where(beats, 1.0, 0.0)
    cnt = jnp.concatenate(cnts, axis=0)
    sel_t = jnp.where((cnt < float(min(SEL_TOPK, nb))) & valid_b, 0.0, NEG)

    per_tile = tq // SEL_BLOCK
    for t in range(pen_ref.shape[0]):
        pen_ref[t] = jnp.concatenate([sel_t[t * per_tile:(t + 1) * per_tile, :],
                                      jnp.zeros((8 - per_tile, tq), F32)], axis=0)
    blk8 = lax.broadcasted_iota(jnp.int32, (8, tq), 0)

    def tile_pen(t, hi):
        rows = jnp.where(t * per_tile + blk8 >= hi, NEG, pen_ref[t])
        return jnp.concatenate([jnp.broadcast_to(rows[u:u + 1, :], (SEL_BLOCK, tq)) for u in range(per_tile)],
                               axis=0)

    def sel_scores(t0, ntile, hi):
        off = pl.multiple_of(t0 * tq, tq)
        s = _dot(ks_ref[0, 0, pl.ds(off, ntile * tq), :], qt)
        spen = jnp.concatenate([tile_pen(t0 + u, hi) for u in range(ntile)], axis=0)
        return s + heads(lambda h: spen)

    FAR_TILES = FAR_KEYS // tq
    nk = FAR_KEYS
    n_far = jnp.maximum(qi - 1, 0)
    nblk = (n_far + FAR_TILES - 1) // FAR_TILES

    def far_t0(j):
        return jnp.maximum(n_far - FAR_TILES * (j + 1), 0)

    def far_scores(j):
        return sel_scores(far_t0(j), FAR_TILES, (n_far - FAR_TILES * j) * per_tile)

    t1 = jnp.maximum(qi - 1, 0)
    s = jnp.concatenate([sel_scores(t1, 1, qi * per_tile) + table(T_NEAR_REL),
                         sel_scores(qi, 1, (qi + 1) * per_tile) + table(T_DIAG_REL)], axis=0)
    vx = jnp.concatenate([vs_ref[0, 0, :, pl.ds(pl.multiple_of(t1 * tq, tq), tq)],
                          vs_ref[0, 0, :, pl.ds(pl.multiple_of(qi * tq, tq), tq)]], axis=1)
    carry = flash(init, s, vx)

    def far_v(j):
        return vs_ref[0, 0, :, pl.ds(pl.multiple_of(far_t0(jnp.maximum(j, 0)) * tq, tq), nk)]

    def far_body(j, carry):
        m, acc, alpha_prev = carry
        slot = j % 2
        s = far_scores(j)
        pv = _dot(far_v(j - 1), pbuf_ref[1 - slot])
        m_new = jnp.maximum(m, jnp.max(s, axis=0, keepdims=True))
        pbuf_ref[slot] = jnp.exp2(s - m_new).astype(BF16)
        return m_new, alpha_prev * acc + pv, jnp.exp2(m - m_new)

    pbuf_ref[1] = jnp.zeros(pbuf_ref.shape[1:], BF16)
    m, acc, alpha = lax.fori_loop(0, nblk, far_body, (carry[0], carry[1], jnp.ones((1, cols), F32)))
    last = jnp.where(nblk > 0, (nblk - 1) % 2, 1)
    o_s = finish((m, alpha * acc + _dot(far_v(nblk - 1), pbuf_ref[last])))

    zg = pltpu.roll(zg_ref[...].astype(F32), (LANE - 16 * g) % LANE, axis=1)
    gates = _sigmoid(zg).T
    out = (heads(lambda h: gates[h:h + 1, :]) * o_c + heads(lambda h: gates[4 + h:5 + h, :]) * o_s
           + heads(lambda h: gates[8 + h:9 + h, :]) * o_w)
    for hp in range(N_HG // 2):
        pair = jnp.concatenate([out[:, (2 * hp) * tq:(2 * hp + 1) * tq],
                                out[:, (2 * hp + 1) * tq:(2 * hp + 2) * tq]], axis=0)
        o_ref[:, hp * 2 * HEAD_DIM:(hp + 1) * 2 * HEAD_DIM] = pair.T.astype(o_ref.dtype)


def _nsa_attn(qn, kc, vc, ks, vs, kw, vw, z, pat, bw, *, batch, seq, tq, ncp):
    nt = seq // tq
    nb = seq // SEL_BLOCK
    n_cmp = (seq - CMP_BLOCK) // CMP_STRIDE + 1
    ci = np.arange(ncp)[None, :] * CMP_STRIDE
    sj = np.arange(nb)[:, None] * SEL_BLOCK
    msel = ((ci <= sj + SEL_BLOCK - 1) & (ci + CMP_BLOCK - 1 >= sj) & (np.arange(ncp)[None, :] < n_cmp))
    msel_j = jnp.asarray(msel.astype(np.float32), BF16)
    assert nb <= LANE and seq >= FAR_KEYS and FAR_KEYS % tq == 0 and WINDOW // tq >= 2
    ks_spec = pl.BlockSpec((1, 1, seq, ks.shape[3]), lambda b, g, i: (b, g, 0, 0))
    kw_spec = pl.BlockSpec((1, 1, seq, HEAD_DIM), lambda b, g, i: (b, g, 0, 0))
    v_spec = pl.BlockSpec((1, 1, vs.shape[2], seq), lambda b, g, i: (b, g, 0, 0))
    kc_spec = pl.BlockSpec((1, 1, ncp, HEAD_DIM), lambda b, g, i: (b, g, 0, 0))
    vc_spec = pl.BlockSpec((1, 1, HEAD_DIM, ncp), lambda b, g, i: (b, g, 0, 0))
    width = N_HEADS * HEAD_DIM
    return pl.pallas_call(
        functools.partial(_nsa_attn_kernel, tq=tq, ncp=ncp, nb=nb),
        out_shape=jax.ShapeDtypeStruct((batch * seq, width), BF16),
        grid=(batch, N_KV, nt),
        in_specs=[
            pl.BlockSpec((1, 1, 1, HEAD_DIM, N_HG * tq), lambda b, g, i: (b, g, i, 0, 0)),
            kc_spec, vc_spec, ks_spec, v_spec, kw_spec, v_spec,
            pl.BlockSpec((tq, LANE), lambda b, g, i: (b * nt + i, C_GATE // LANE)),
            pl.BlockSpec((N_HG, 2 * ncp, tq), lambda b, g, i: (g, 0, 0)),
            pl.BlockSpec((N_HG, N_TABLES, tq, tq), lambda b, g, i: (g, 0, 0, 0)),
            pl.BlockSpec(msel_j.shape, lambda b, g, i: (0, 0)),
        ],
        out_specs=pl.BlockSpec((tq, N_HG * HEAD_DIM), lambda b, g, i: (b * nt + i, g)),
        scratch_shapes=[pltpu.VMEM((nt, 8, tq), F32),
                        pltpu.VMEM((2, FAR_KEYS, N_HG * tq), BF16)],
        compiler_params=_cparams(("arbitrary", "arbitrary", "arbitrary")),
        name="nsa_attn",
    )(qn, kc, vc, ks, vs, kw, vw, z, pat, bw, msel_j)


def _merge_kernel(x_ref, yr_ref, yn_ref, zm_ref, wr_ref, wn_ref, wm_ref, o_ref):
    d = x_ref.shape[1]
    zm = zm_ref[...].astype(F32)
    br = _dot(yr_ref[...].astype(BF16), wr_ref[...])
    bn = _dot(yn_ref[...].astype(BF16), wn_ref[...])
    merged = _sigmoid(zm[:, 0:d]) * br + _sigmoid(zm[:, d:2 * d]) * bn
    o_ref[...] = x_ref[...] + _dot(merged.astype(BF16), wm_ref[...])


def _merge(x, yr, yn, z, wr, wn, wm, *, tm):
    m, d = x.shape
    row = lambda c: pl.BlockSpec((tm, c), lambda i: (i, 0))
    full = lambda a: pl.BlockSpec(a.shape, lambda i: (0, 0))
    return pl.pallas_call(
        _merge_kernel,
        out_shape=jax.ShapeDtypeStruct((m, d), F32),
        grid=(m // tm,),
        in_specs=[row(d), row(d), row(d), pl.BlockSpec((tm, 2 * d), lambda i: (i, C_M // (2 * d))),
                  full(wr), full(wn), full(wm)],
        out_specs=row(d),
        compiler_params=_cparams(("arbitrary",)),
        name="merge",
    )(x, yr, yn, z, wr, wn, wm)


def _ca_kv_kernel(mem_ref, g_ref, w_ref, kg_ref, k_ref, v_ref):
    d = mem_ref.shape[2]
    dh = d // CA_HEADS
    mn = (_rms(mem_ref[0], NORM_EPS) * g_ref[...]).astype(BF16)
    kv = _dot(mn, w_ref[...])
    ks = [(_rms(kv[:, h * dh:(h + 1) * dh], QK_EPS) * kg_ref[...]) for h in range(CA_HEADS)]
    k_ref[0] = jnp.concatenate(ks, axis=1).astype(k_ref.dtype)
    v_ref[0] = kv[:, d:2 * d].astype(v_ref.dtype)


def _ca_kv(mem, g, wkv, kg):
    batch, nm, d = mem.shape
    o_shape = jax.ShapeDtypeStruct((batch, nm, d), BF16)
    o_spec = pl.BlockSpec((1, nm, d), lambda b: (b, 0, 0))
    full = lambda a: pl.BlockSpec(a.shape, lambda b: (0, 0))
    return pl.pallas_call(
        _ca_kv_kernel,
        out_shape=(o_shape, o_shape),
        grid=(batch,),
        in_specs=[o_spec, full(g), full(wkv), full(kg)],
        out_specs=(o_spec, o_spec),
        compiler_params=_cparams(("arbitrary",)),
        name="ca_kv",
    )(mem, g, wkv, kg)


def _cross_kernel(x_ref, g_ref, wq_ref, qg_ref, k_ref, v_ref, wo_ref, o_ref):
    d = x_ref.shape[1]
    dh = d // CA_HEADS
    x = x_ref[...]
    xn = (_rms(x, NORM_EPS) * g_ref[...]).astype(BF16)
    qf = _dot(xn, wq_ref[...])
    k = k_ref[0]
    v = v_ref[0]
    outs = []
    for h in range(CA_HEADS):
        sl = slice(h * dh, (h + 1) * dh)
        qh = (_rms(qf[:, sl], QK_EPS) * qg_ref[...] * (dh ** -0.5)).astype(BF16)
        s = _dot_nt(qh, k[:, sl])
        s = s - jnp.max(s, axis=-1, keepdims=True)
        e = jnp.exp(s)
        p = e / jnp.sum(e, axis=-1, keepdims=True)
        outs.append(_dot(p.astype(BF16), v[:, sl]))
    o = jnp.concatenate(outs, axis=1).astype(BF16)
    o_ref[...] = x + _dot(o, wo_ref[...])


def _cross(h1, g, wq, qg, kn, vv, wo, *, batch, seq, tm):
    m, d = h1.shape
    nt = seq // tm
    nm = kn.shape[1]
    row = pl.BlockSpec((tm, d), lambda b, i: (b * nt + i, 0))
    full = lambda a: pl.BlockSpec(a.shape, lambda b, i: (0, 0))
    kv_spec = pl.BlockSpec((1, nm, d), lambda b, i: (b, 0, 0))
    return pl.pallas_call(
        _cross_kernel,
        out_shape=jax.ShapeDtypeStruct((m, d), F32),
        grid=(batch, nt),
        in_specs=[row, full(g), full(wq), full(qg), kv_spec, kv_spec, full(wo)],
        out_specs=row,
        compiler_params=_cparams(("arbitrary", "arbitrary")),
        name="cross",
    )(h1, g, wq, qg, kn, vv, wo)


def _ffn_kernel(x_ref, g_ref, wa_ref, wb_ref, cw_ref, cb_ref, wd_ref, o_ref, xn_ref, acc_ref, carry_ref,
                *, tiles_per_seq):
    i = pl.program_id(0)
    j = pl.program_id(1)
    nj = pl.num_programs(1)
    tm = x_ref.shape[0]

    @pl.when(j == 0)
    def _():
        xn_ref[...] = (_rms(x_ref[...], NORM_EPS) * g_ref[...]).astype(BF16)
        acc_ref[...] = jnp.zeros_like(acc_ref)

    @pl.when(i % tiles_per_seq == 0)
    def _():
        carry_ref[j] = jnp.zeros(carry_ref.shape[1:], F32)

    xn = xn_ref[...]
    a = _dot(xn, wa_ref[...])
    b = _dot(xn, wb_ref[...])
    car = carry_ref[j]
    rowi = lax.broadcasted_iota(jnp.int32, a.shape, 0)
    p1 = jnp.where(rowi == 0, car[7:8, :], pltpu.roll(a, 1, axis=0))
    p2 = jnp.where(rowi == 0, car[6:7, :], jnp.where(rowi == 1, car[7:8, :], pltpu.roll(a, 2, axis=0)))
    carry_ref[j] = a[tm - 8:tm, :]
    cw = cw_ref[0]
    conv = cw[0:1, :] * p2 + cw[1:2, :] * p1 + cw[2:3, :] * a + cb_ref[0]
    act = conv * _sigmoid(conv) * b
    acc_ref[...] += _dot(act.astype(BF16), wd_ref[...])

    @pl.when(j == nj - 1)
    def _():
        o_ref[...] = x_ref[...] + acc_ref[...]


def _ffn(h2, g, wup, cw, cb, wd, *, seq, tm, tn):
    m, d = h2.shape
    dff = wd.shape[0]
    nj = dff // tn
    cw3 = jnp.zeros((nj, 8, tn), F32).at[:, 0:CONV_W, :].set(cw.reshape(CONV_W, nj, tn).transpose(1, 0, 2))
    cb3 = cb.reshape(nj, 1, tn)
    return pl.pallas_call(
        functools.partial(_ffn_kernel, tiles_per_seq=seq // tm),
        out_shape=jax.ShapeDtypeStruct((m, d), F32),
        grid=(m // tm, nj),
        in_specs=[
            pl.BlockSpec((tm, d), lambda i, j: (i, 0)),
            pl.BlockSpec((1, d), lambda i, j: (0, 0)),
            pl.BlockSpec((d, tn), lambda i, j: (0, j)),
            pl.BlockSpec((d, tn), lambda i, j: (0, nj + j)),
            pl.BlockSpec((1, 8, tn), lambda i, j: (j, 0, 0)),
            pl.BlockSpec((1, 1, tn), lambda i, j: (j, 0, 0)),
            pl.BlockSpec((tn, d), lambda i, j: (j, 0)),
        ],
        out_specs=pl.BlockSpec((tm, d), lambda i, j: (i, 0)),
        scratch_shapes=[pltpu.VMEM((tm, d), BF16), pltpu.VMEM((tm, d), F32), pltpu.VMEM((nj, 8, tn), F32)],
        compiler_params=_cparams(("arbitrary", "arbitrary")),
        name="ffn",
    )(h2, g, wup, wup, cw3, cb3, wd)


def _pack_perm(width):
    o_zw = 3 * width
    o_za = o_zw + R_LORA_W
    o_zg = o_za + R_LORA_A
    o_q = o_zg + R_LORA_G
    nkv = N_KV * HEAD_DIM
    o_kc = o_q + N_HEADS * HEAD_DIM
    o_gate = o_kc + 6 * nkv
    o_m = o_gate + 3 * N_HEADS
    perm = np.full((Z_COLS,), -1, np.int64)
    perm[C_RKV:C_RKV + 3 * width] = np.arange(3 * width)
    perm[C_Q:C_Q + N_HEADS * HEAD_DIM] = o_q + np.arange(N_HEADS * HEAD_DIM)
    perm[C_KC:C_KC + 6 * nkv] = o_kc + np.arange(6 * nkv)
    perm[C_ZG:C_ZG + R_LORA_G] = o_zg + np.arange(R_LORA_G)
    perm[C_ZWA:C_ZWA + R_LORA_W + R_LORA_A] = o_zw + np.arange(R_LORA_W + R_LORA_A)
    for g in range(N_KV):
        for c in range(3):
            for h in range(N_HG):
                perm[C_GATE + 16 * g + 4 * c + h] = o_gate + (g * N_HG + h) * 3 + c
    perm[C_M:C_M + 2 * width] = o_m + np.arange(2 * width)
    return perm, o_m + 2 * width


def _pad_rows(w, rows, offset=0):
    out = jnp.zeros((rows, w.shape[1]), w.dtype)
    return out.at[offset:offset + w.shape[0]].set(w)


def kernel(x, mem, rel_bias, norm_mix, w_in, rwkv_mu, rwkv_w0, rwkv_w2, rwkv_a0, rwkv_a2, rwkv_g2,
           rwkv_kk, rwkv_ka, rwkv_rk, rwkv_lnx_w, rwkv_lnx_b, nsa_q_gain, nsa_k_gain, cmp_pe_k, cmp_pe_v,
           cmp_w1_k, cmp_w2_k, cmp_w1_v, cmp_w2_v, w_branch_rwkv, w_branch_nsa, w_mix_out,
           norm_cross, norm_mem, ca_wq, ca_wkv, ca_q_gain, ca_k_gain, ca_wo,
           norm_ffn, ffn_up, ffn_conv, ffn_conv_b, ffn_down):
    batch, seq, d = x.shape
    depth = w_in.shape[0]
    width = d
    assert d == N_HEADS * HEAD_DIM and 2 * (Z_COLS - C_M) == 4 * d
    perm, in_cols = _pack_perm(width)
    assert w_in.shape[2] == in_cols
    perm_j = jnp.asarray(np.maximum(perm, 0), jnp.int32)
    keep = jnp.asarray(perm >= 0)

    tq = 256
    assert seq % 256 == 0 and tq > REL_MAX_DIST - 1 and WINDOW % tq == 0
    ncp = -(-(seq // CMP_STRIDE) // LANE) * LANE
    n_cmp = (seq - CMP_BLOCK) // CMP_STRIDE + 1
    tm = min(1024, seq)
    row = lambda v: v.reshape(1, -1).astype(F32)
    nkv = N_KV * HEAD_DIM

    h = x.reshape(batch * seq, d)
    for l in range(depth):
        w_in_p = jnp.where(keep[None, :], jnp.take(w_in[l], perm_j, axis=1), 0.0).astype(BF16)
        mu = rwkv_mu[l]
        mu_r = row(mu[0:3 * width])
        mu_wa = row(mu[3 * width:3 * width + R_LORA_W + R_LORA_A])
        mu_g = row(jnp.zeros((256,), F32).at[0:R_LORA_G].set(mu[3 * width + R_LORA_W + R_LORA_A:]))
        w2p = _pad_rows(rwkv_w2[l], LANE, 0).astype(BF16)
        a2p = _pad_rows(rwkv_a2[l], LANE, R_LORA_W).astype(BF16)
        g2p = _pad_rows(rwkv_g2[l], 256, 0).astype(BF16)

        z = _norm_matmul(h, row(norm_mix[l]), w_in_p, tm=tm, tn=1024, out_dtype=BF16, name="in_proj")

        y_r = _rwkv(z, mu_r, mu_g, mu_wa, row(rwkv_w0[l]), w2p, row(rwkv_a0[l]), a2p, g2p,
                    row(rwkv_kk[l]), row(rwkv_ka[l]), row(rwkv_rk[l]), row(rwkv_lnx_w[l]),
                    row(rwkv_lnx_b[l]), batch=batch, seq=seq, width=width)

        qn, ks, vs, kw, vw = _nsa_prep(
            z, row(jnp.tile(nsa_q_gain[l], N_HEADS)), row(jnp.tile(nsa_k_gain[l, 1], N_KV)),
            row(jnp.tile(nsa_k_gain[l, 2], N_KV)), batch=batch, seq=seq, tq=tq)
        z3 = z.reshape(batch, seq // CMP_STRIDE, CMP_STRIDE, Z_COLS)

        def groups(c0):
            t = z3[:, :, :, c0:c0 + nkv].reshape(batch, seq // CMP_STRIDE, CMP_STRIDE, N_KV, HEAD_DIM)
            return t.transpose(0, 3, 1, 2, 4).reshape(batch, N_KV, seq // CMP_STRIDE, CMP_STRIDE * HEAD_DIM)

        kc, vc = _nsa_cmp(groups(C_KC), groups(C_KC + nkv), cmp_pe_k[l].reshape(1, -1), cmp_pe_v[l].reshape(1, -1),
                          cmp_w1_k[l].astype(BF16), cmp_w2_k[l].astype(BF16), cmp_w1_v[l].astype(BF16),
                          cmp_w2_v[l].astype(BF16), row(nsa_k_gain[l, 0]), ncp=ncp, n_cmp=n_cmp)
        bw, pat = _nsa_bias(rel_bias.astype(F32), tq=tq, ncp=ncp)
        y_n = _nsa_attn(qn, kc, vc, ks, vs, kw, vw, z, pat, bw,
                        batch=batch, seq=seq, tq=tq, ncp=ncp)

        h1 = _merge(h, y_r, y_n, z, w_branch_rwkv[l].astype(BF16), w_branch_nsa[l].astype(BF16),
                    w_mix_out[l].astype(BF16), tm=min(512, seq))

        kn, vv = _ca_kv(mem, row(norm_mem[l]), ca_wkv[l].astype(BF16), row(ca_k_gain[l]))
        h2 = _cross(h1, row(norm_cross[l]), ca_wq[l].astype(BF16), row(ca_q_gain[l]), kn, vv,
                    ca_wo[l].astype(BF16), batch=batch, seq=seq, tm=min(512, seq))

        h = _ffn(h2, row(norm_ffn[l]), ffn_up[l].astype(BF16), ffn_conv[l], ffn_conv_b[l],
                 ffn_down[l].astype(BF16), seq=seq, tm=tm, tn=256)
    return h.reshape(batch, seq, d)
```

```python
import functools
import math

import jax
import jax.numpy as jnp
import numpy as np
from jax import lax
from jax.experimental import pallas as pl
from jax.experimental.pallas import tpu as pltpu

F32 = jnp.float32
BF16 = jnp.bfloat16
NEG = -1e30
LOG2E = 1.4426950408889634
FAR_KEYS = 512
T_DIAG, T_NEAR, T_NONE, T_FAR, T_EDGE, T_DIAG_REL, T_NEAR_REL = range(7)
N_TABLES = 7
V_ROWS = 80

R_HEAD = 64
R_LORA_W = 64
R_LORA_A = 64
R_LORA_G = 160
LNX_EPS = 64e-5
N_HEADS = 16
N_KV = 4
N_HG = N_HEADS // N_KV
HEAD_DIM = 64
CMP_BLOCK = 32
CMP_STRIDE = 16
CMP_HIDDEN = 256
SEL_BLOCK = 64
SEL_TOPK = 16
WINDOW = 512
FORCE_SCORE = 1e4
REL_BUCKETS = 32
REL_MAX_DIST = 128
CA_HEADS = 4
CONV_W = 3
NORM_EPS = 1e-6
QK_EPS = 1e-6

LANE = 128
VMEM_LIMIT = 56 * 1024 * 1024

C_RKV = 0
C_Q = 3072
C_KC = 4096
C_ZG = 5632
C_ZWA = 5888
C_GATE = 6016
C_M = 6144
Z_COLS = 8192


def _dot(a, b):
    return jnp.dot(a, b, preferred_element_type=F32)


def _dot_nt(a, b):
    return lax.dot_general(a, b, (((1,), (1,)), ((), ())), preferred_element_type=F32)


def _split_dot(x, e, parts, mode="xe"):
    acc = None
    rem = x
    for i in range(parts):
        hi = rem.astype(BF16)
        t = _dot(hi, e) if mode == "xe" else (_dot(e, hi) if mode == "ex" else _dot_nt(e, hi))
        acc = t if acc is None else acc + t
        if i + 1 < parts:
            rem = rem - hi.astype(F32)
    return acc


def _rms(x, eps):
    return x * lax.rsqrt(jnp.mean(x * x, axis=-1, keepdims=True) + eps)


def _sigmoid(x):
    return 1.0 / (1.0 + jnp.exp(-x))


def _cparams(sem):
    return pltpu.CompilerParams(dimension_semantics=sem, vmem_limit_bytes=VMEM_LIMIT)


def _norm_matmul_kernel(x_ref, g_ref, w_ref, o_ref, xn_ref):
    @pl.when(pl.program_id(1) == 0)
    def _():
        x = x_ref[...]
        xn_ref[...] = (_rms(x, NORM_EPS) * g_ref[...]).astype(BF16)

    o_ref[...] = _dot(xn_ref[...], w_ref[...]).astype(o_ref.dtype)


def _norm_matmul(x, g, w, *, tm, tn, out_dtype, name):
    m, k = x.shape
    n = w.shape[1]
    return pl.pallas_call(
        _norm_matmul_kernel,
        out_shape=jax.ShapeDtypeStruct((m, n), out_dtype),
        grid=(m // tm, n // tn),
        in_specs=[
            pl.BlockSpec((tm, k), lambda i, j: (i, 0)),
            pl.BlockSpec((1, k), lambda i, j: (0, 0)),
            pl.BlockSpec((k, tn), lambda i, j: (0, j)),
        ],
        out_specs=pl.BlockSpec((tm, tn), lambda i, j: (i, j)),
        scratch_shapes=[pltpu.VMEM((tm, k), BF16)],
        compiler_params=_cparams(("arbitrary", "arbitrary")),
        name=name,
    )(x, g, w)


def _rwkv_kernel(zr_ref, zg_ref, zwa_ref, mur_ref, mug_ref, muwa_ref, w0_ref, w2_ref, a0_ref, a2_ref,
                 g2_ref, kk_ref, ka_ref, rk_ref, lnw_ref, lnb_ref, eh_ref, eht_ref, tri_ref,
                 o_ref, pr_ref, pg_ref, pwa_ref, st_ref, *, L, width):
    i = pl.program_id(1)
    npair = width // LANE
    tb = o_ref.shape[0]

    @pl.when(i == 0)
    def _():
        pr_ref[...] = jnp.zeros_like(pr_ref)
        pg_ref[...] = jnp.zeros_like(pg_ref)
        pwa_ref[...] = jnp.zeros_like(pwa_ref)
        st_ref[...] = jnp.zeros_like(st_ref)

    def shifted8(z, prev_ref, mu):
        rolled = pltpu.roll(z, 1, axis=0)
        row0 = lax.broadcasted_iota(jnp.int32, z.shape, 0) == 0
        prev = jnp.where(row0, prev_ref[7:8, :], rolled)
        prev_ref[...] = z[tb - 8:tb, :]
        return z + (prev - z) * mu

    zs = shifted8(zr_ref[...].astype(F32), pr_ref, mur_ref[...])
    zsg = shifted8(zg_ref[...].astype(F32), pg_ref, mug_ref[...])
    zswa = shifted8(zwa_ref[...].astype(F32), pwa_ref, muwa_ref[...])

    r = zs[:, 0:width]
    k = zs[:, width:2 * width]
    v = zs[:, 2 * width:3 * width]

    w_lin = w0_ref[...] + _dot(jnp.tanh(zswa).astype(BF16), w2_ref[...])
    a_lin = a0_ref[...] + _dot(zswa.astype(BF16), a2_ref[...])
    y = -w_lin
    softplus = jnp.maximum(y, 0.0) + jnp.log(1.0 + jnp.exp(-jnp.abs(y)))
    ld = -jnp.exp(-softplus - 0.5)
    a = _sigmoid(a_lin)
    g = _dot(_sigmoid(zsg).astype(BF16), g2_ref[...])

    eh = eh_ref[...]
    eht = eht_ref[...]

    def sum_heads(t):
        return _split_dot(t, eh, 2)

    def bcast_heads(t):
        return _split_dot(t, eht, 2)

    kkr = k * kk_ref[...]
    nrm = jnp.maximum(jnp.sqrt(sum_heads(kkr * kkr)), 1e-12)
    kkn = kkr * bcast_heads(1.0 / nrm)
    k2 = k * (1.0 + (a - 1.0) * ka_ref[...])
    av = -kkn
    bv = kkn * a
    bonus = bcast_heads(sum_heads(r * k2 * rk_ref[...])) * v

    nch = tb // L
    lg = _split_dot(ld, tri_ref[...], 3, mode="ex")
    lasts = [lg[(c + 1) * L - 1:(c + 1) * L, :] for c in range(nch)]
    lg_last = jnp.concatenate([jnp.broadcast_to(t, (L, width)) for t in lasts], axis=0)
    eg = jnp.exp(lg)
    eng = jnp.exp(-lg)
    egl = jnp.exp(lg_last - lg)
    rt = r * eg
    kt = k2 * eng
    bt = bv * eng
    at = av * jnp.exp(lg - ld)
    kgl = k2 * egl
    bgl = bv * egl
    gl = [jnp.exp(t) for t in lasts]

    lane = lax.broadcasted_iota(jnp.int32, (L, LANE), 1)
    lo = lane < R_HEAD

    def stack(t):
        return jnp.concatenate([jnp.where(lo, t, 0.0), jnp.where(lo, 0.0, t)], axis=0)

    row = lax.broadcasted_iota(jnp.int32, (2 * L, 2 * L), 0)
    col = lax.broadcasted_iota(jnp.int32, (2 * L, 2 * L), 1)
    strict = row > col
    incl = row >= col
    eye = (row == col).astype(F32)
    nsq = int(math.log2(L)) - 1

    pairs = range(npair)
    chains = [(c, p) for c in range(nch) for p in pairs]
    sub = lambda t, c, p: t[c * L:(c + 1) * L, p * LANE:(p + 1) * LANE]
    bf = lambda t: t.astype(BF16)
    ar = {cp: bf(jnp.concatenate([stack(sub(at, *cp)), stack(sub(rt, *cp))], axis=0)) for cp in chains}
    bk = {cp: bf(jnp.concatenate([stack(sub(bt, *cp)), stack(sub(kt, *cp))], axis=0)) for cp in chains}
    v_s = {cp: bf(stack(sub(v, *cp))) for cp in chains}
    gm = {cp: _dot_nt(ar[cp], bk[cp]) for cp in chains}
    a_ab = {cp: jnp.where(strict, gm[cp][0:2 * L, 0:2 * L], 0.0) for cp in chains}
    a_ak = {cp: bf(jnp.where(strict, gm[cp][0:2 * L, 2 * L:4 * L], 0.0)) for cp in chains}
    a_rb = {cp: bf(jnp.where(incl, gm[cp][2 * L:4 * L, 0:2 * L], 0.0)) for cp in chains}
    a_rk = {cp: bf(jnp.where(incl, gm[cp][2 * L:4 * L, 2 * L:4 * L], 0.0)) for cp in chains}
    tinv = {cp: eye + a_ab[cp] for cp in chains}
    pw = a_ab
    for _ in range(nsq):
        pwb = {cp: bf(pw[cp]) for cp in chains}
        pw = {cp: _dot(pwb[cp], pwb[cp]) for cp in chains}
        tinv = {cp: tinv[cp] + _dot(bf(pw[cp]), bf(tinv[cp])) for cp in chains}
    tinv = {cp: bf(tinv[cp]) for cp in chains}
    bkgt = {cp: bf(jnp.concatenate([stack(sub(bgl, *cp)), stack(sub(kgl, *cp))], axis=0).T) for cp in chains}
    glcol = {(c, p): jnp.sum(eye * gl[c][:, p * LANE:(p + 1) * LANE], axis=1, keepdims=True)
             for (c, p) in chains}
    h = [st_ref[p] for p in pairs]
    ys = []
    for c in range(nch):
        hb = [bf(h[p]) for p in pairs]
        wmat = [_dot(jnp.concatenate([ar[c, p][0:2 * L], a_ak[c, p]], axis=1),
                     jnp.concatenate([hb[p], v_s[c, p]], axis=0)) for p in pairs]
        ub = [bf(_dot(tinv[c, p], bf(wmat[p]))) for p in pairs]
        yy = [_dot(jnp.concatenate([ar[c, p][2 * L:4 * L], a_rb[c, p], a_rk[c, p]], axis=1),
                   jnp.concatenate([hb[p], ub[p], v_s[c, p]], axis=0)) for p in pairs]
        h = [h[p] * glcol[c, p] + _dot(bkgt[c, p], jnp.concatenate([ub[p], v_s[c, p]], axis=0))
             for p in pairs]
        ys.append(jnp.concatenate([yy[p][0:L] + yy[p][L:2 * L] for p in pairs], axis=1))
    for p in pairs:
        st_ref[p] = h[p]

    yv = jnp.concatenate(ys, axis=0)
    inv_n = 1.0 / R_HEAD
    mean = bcast_heads(sum_heads(yv) * inv_n)
    yc = yv - mean
    var = bcast_heads(sum_heads(yc * yc) * inv_n)
    yn = yc * lax.rsqrt(var + LNX_EPS) * lnw_ref[...] + lnb_ref[...]
    o_ref[...] = ((yn + bonus) * g).astype(o_ref.dtype)


def _rwkv(z, mu_r, mu_g, mu_wa, w0, w2p, a0, a2p, g2p, kk, ka, rk, lnw, lnb, *, batch, seq, width):
    L = 64
    tb = 4 * L
    nt = seq // tb
    nheads = width // R_HEAD
    eh = np.zeros((width, LANE), np.float32)
    eh[np.arange(width), np.arange(width) // R_HEAD] = 1.0
    ti = np.arange(tb)
    tri = ((ti[:, None] >= ti[None, :]) & (ti[:, None] // L == ti[None, :] // L)).astype(np.float32)
    vec = lambda c: pl.BlockSpec((1, c), lambda b, i: (0, 0))
    full = lambda a: pl.BlockSpec(a.shape, lambda b, i: (0,) * a.ndim)
    eh_j = jnp.asarray(eh, BF16)
    eht_j = jnp.asarray(eh.T, BF16)
    tri_j = jnp.asarray(tri, BF16)
    return pl.pallas_call(
        functools.partial(_rwkv_kernel, L=L, width=width),
        out_shape=jax.ShapeDtypeStruct((batch * seq, width), BF16),
        grid=(batch, nt),
        in_specs=[
            pl.BlockSpec((tb, 3 * width), lambda b, i: (b * nt + i, C_RKV // (3 * width))),
            pl.BlockSpec((tb, 256), lambda b, i: (b * nt + i, C_ZG // 256)),
            pl.BlockSpec((tb, LANE), lambda b, i: (b * nt + i, C_ZWA // LANE)),
            vec(3 * width), vec(256), vec(LANE),
            vec(width), full(w2p), vec(width), full(a2p), full(g2p),
            vec(width), vec(width), vec(width), vec(width), vec(width),
            full(eh_j), full(eht_j), full(tri_j),
        ],
        out_specs=pl.BlockSpec((tb, width), lambda b, i: (b * nt + i, 0)),
        scratch_shapes=[
            pltpu.VMEM((8, 3 * width), F32),
            pltpu.VMEM((8, 256), F32),
            pltpu.VMEM((8, LANE), F32),
            pltpu.VMEM((nheads // 2, 2 * R_HEAD, 2 * R_HEAD), F32),
        ],
        compiler_params=_cparams(("arbitrary", "arbitrary")),
        name="rwkv",
    )(z, z, z, mu_r, mu_g, mu_wa, w0, w2p, a0, a2p, g2p, kk, ka, rk, lnw, lnb, eh_j, eht_j, tri_j)


def _nsa_prep_kernel(zq_ref, zs_ref, zw_ref, qg_ref, kgs_ref, kgw_ref, ehq_ref, ehqt_ref,
                     q_ref, ks_ref, vs_ref, kw_ref, vw_ref):
    inv_d = 1.0 / HEAD_DIM

    def head_norm(x, gain, e, et):
        ss = _split_dot(x * x, e, 2)
        inv = lax.rsqrt(ss * inv_d + QK_EPS)
        return x * _split_dot(inv, et, 2) * gain

    nk = N_KV * HEAD_DIM
    ehq = ehq_ref[...]
    ehqt = ehqt_ref[...]
    q = head_norm(zq_ref[...].astype(F32), qg_ref[...], ehq, ehqt) * (HEAD_DIM ** -0.5 * LOG2E)
    qt = q.T
    for gq in range(N_KV):
        rows = [qt[(gq * N_HG + h) * HEAD_DIM:(gq * N_HG + h + 1) * HEAD_DIM, :] for h in range(N_HG)]
        q_ref[0, gq, 0] = jnp.concatenate(rows, axis=1).astype(q_ref.dtype)
    zs = zs_ref[...].astype(F32)
    zw = zw_ref[...].astype(F32)
    ehk = ehq[0:nk, :]
    ehkt = ehqt[:, 0:nk]
    ksn = head_norm(zs[:, 0:nk], kgs_ref[...], ehk, ehkt)
    kwn = head_norm(zw[:, 0:nk], kgw_ref[...], ehk, ehkt)
    vst = zs[:, nk:2 * nk].T
    vwt = zw[:, nk:2 * nk].T
    tb = zs.shape[0]
    vextra = jnp.where(lax.broadcasted_iota(jnp.int32, (V_ROWS - HEAD_DIM, tb), 0) == 0, 1.0, 0.0)
    for gq in range(N_KV):
        sl = slice(gq * HEAD_DIM, (gq + 1) * HEAD_DIM)
        ks_ref[0, gq] = ksn[:, sl].astype(ks_ref.dtype)
        vs_ref[0, gq] = jnp.concatenate([vst[sl, :], vextra], axis=0).astype(vs_ref.dtype)
        kw_ref[0, gq] = kwn[:, sl].astype(kw_ref.dtype)
        vw_ref[0, gq] = jnp.concatenate([vwt[sl, :], vextra], axis=0).astype(vw_ref.dtype)


def _nsa_prep(z, q_gain_t, kgs_t, kgw_t, *, batch, seq, tq):
    tb = tq
    nt = seq // tb
    nq = N_HEADS * HEAD_DIM
    nk = N_KV * HEAD_DIM
    eh = np.zeros((nq, LANE), np.float32)
    eh[np.arange(nq), np.arange(nq) // HEAD_DIM] = 1.0
    eh_j = jnp.asarray(eh, BF16)
    eht_j = jnp.asarray(eh.T, BF16)
    k_shape = jax.ShapeDtypeStruct((batch, N_KV, seq, HEAD_DIM), BF16)
    k_spec = pl.BlockSpec((1, N_KV, tb, HEAD_DIM), lambda b, i: (b, 0, i, 0))
    v_shape = jax.ShapeDtypeStruct((batch, N_KV, V_ROWS, seq), BF16)
    v_spec = pl.BlockSpec((1, N_KV, V_ROWS, tb), lambda b, i: (b, 0, 0, i))
    return pl.pallas_call(
        _nsa_prep_kernel,
        out_shape=(jax.ShapeDtypeStruct((batch, N_KV, nt, HEAD_DIM, N_HG * tq), BF16),
                   k_shape, v_shape, k_shape, v_shape),
        grid=(batch, nt),
        in_specs=[
            pl.BlockSpec((tb, nq), lambda b, i: (b * nt + i, C_Q // nq)),
            pl.BlockSpec((tb, 2 * nk), lambda b, i: (b * nt + i, C_KC // (2 * nk) + 1)),
            pl.BlockSpec((tb, 2 * nk), lambda b, i: (b * nt + i, C_KC // (2 * nk) + 2)),
            pl.BlockSpec((1, nq), lambda b, i: (0, 0)),
            pl.BlockSpec((1, nk), lambda b, i: (0, 0)),
            pl.BlockSpec((1, nk), lambda b, i: (0, 0)),
            pl.BlockSpec(eh_j.shape, lambda b, i: (0, 0)),
            pl.BlockSpec(eht_j.shape, lambda b, i: (0, 0)),
        ],
        out_specs=(pl.BlockSpec((1, N_KV, 1, HEAD_DIM, N_HG * tq), lambda b, i: (b, 0, i, 0, 0)),
                   k_spec, v_spec, k_spec, v_spec),
        compiler_params=_cparams(("arbitrary", "arbitrary")),
        name="nsa_prep",
    )(z, z, z, q_gain_t, kgs_t, kgw_t, eh_j, eht_j)


def _nsa_cmp_kernel(ak_ref, av_ref, pek_ref, pev_ref, w1k_ref, w2k_ref, w1v_ref, w2v_ref, kg_ref,
                    kc_ref, vc_ref, *, n_cmp):
    ncp = kc_ref.shape[2]
    half = w1k_ref.shape[0] // 2

    def compress(a_ref, pe_ref, w1_ref, w2_ref):
        rows = a_ref.shape[2]
        a = a_ref[0, 0].astype(BF16)
        p1 = _dot(a, w1_ref[0:half, :])
        p2 = _dot(a, w1_ref[half:2 * half, :])
        pe8 = jnp.broadcast_to(pe_ref[...], (8, pe_ref.shape[1])).astype(BF16)
        pe_bias = _dot(pe8, w1_ref[...])[0:1, :]
        hpre = p1 + pltpu.roll(p2, rows - 1, axis=0) + pe_bias
        hid = 0.5 * hpre * (1.0 + jnp.tanh(math.sqrt(2.0 / math.pi) * (hpre + 0.044715 * hpre * hpre * hpre)))
        out = _dot(hid.astype(BF16), w2_ref[...])
        if rows < ncp:
            out = jnp.concatenate([out, jnp.zeros((ncp - rows, out.shape[1]), F32)], axis=0)
        return out

    valid = lax.broadcasted_iota(jnp.int32, (ncp, HEAD_DIM), 0) < n_cmp
    kc = compress(ak_ref, pek_ref, w1k_ref, w2k_ref)
    kc = _rms(kc, QK_EPS) * kg_ref[...]
    kc_ref[0, 0] = jnp.where(valid, kc, 0.0).astype(kc_ref.dtype)
    vc = jnp.where(valid, compress(av_ref, pev_ref, w1v_ref, w2v_ref), 0.0)
    vct = jnp.concatenate([vc, jnp.zeros((ncp, LANE - HEAD_DIM), F32)], axis=1).T
    vc_ref[0, 0] = vct[0:HEAD_DIM, :].astype(vc_ref.dtype)


def _nsa_cmp(ak, av, pek, pev, w1k, w2k, w1v, w2v, kg, *, ncp, n_cmp):
    batch, ng, rows, wid = ak.shape
    a_spec = pl.BlockSpec((1, 1, rows, wid), lambda b, g: (b, g, 0, 0))
    full = lambda a: pl.BlockSpec(a.shape, lambda b, g: (0,) * a.ndim)
    k_shape = jax.ShapeDtypeStruct((batch, ng, ncp, HEAD_DIM), BF16)
    k_spec = pl.BlockSpec((1, 1, ncp, HEAD_DIM), lambda b, g: (b, g, 0, 0))
    v_shape = jax.ShapeDtypeStruct((batch, ng, HEAD_DIM, ncp), BF16)
    v_spec = pl.BlockSpec((1, 1, HEAD_DIM, ncp), lambda b, g: (b, g, 0, 0))
    return pl.pallas_call(
        functools.partial(_nsa_cmp_kernel, n_cmp=n_cmp),
        out_shape=(k_shape, v_shape),
        grid=(batch, ng),
        in_specs=[a_spec, a_spec, full(pek), full(pev), full(w1k), full(w2k), full(w1v), full(w2v), full(kg)],
        out_specs=(k_spec, v_spec),
        compiler_params=_cparams(("arbitrary", "arbitrary")),
        name="nsa_cmp",
    )(ak, av, pek, pev, w1k, w2k, w1v, w2v, kg)


def _bucket_thresholds():
    exact = REL_BUCKETS // 2
    n = np.arange(0, REL_MAX_DIST + 1)
    nf = np.maximum(n, 1).astype(np.float64)
    large = exact + (np.log(nf / exact) / math.log(REL_MAX_DIST / exact) * (REL_BUCKETS - exact)).astype(np.int64)
    large = np.minimum(large, REL_BUCKETS - 1)
    bucket = np.where(n < exact, n, large)
    assert bucket[-1] == REL_BUCKETS - 1
    return [int(np.argmax(bucket >= b)) for b in range(REL_BUCKETS)]


def _nsa_bias_kernel(rel_ref, bw_ref, pat_ref, *, tq, ncp):
    h = pl.program_id(0)
    thr = _bucket_thresholds()

    def bias_of(d):
        val = jnp.full(d.shape, rel_ref[0, h], F32)
        for b in range(1, REL_BUCKETS):
            val = jnp.where(d >= thr[b], rel_ref[b, h], val)
        return jnp.where(d < 0, NEG, val * LOG2E)

    kk = lax.broadcasted_iota(jnp.int32, (tq, tq), 0)
    qq = lax.broadcasted_iota(jnp.int32, (tq, tq), 1)
    far = rel_ref[REL_BUCKETS - 1, h] * LOG2E
    diag = bias_of(qq - kk)
    near = bias_of(qq - kk + tq)
    bw_ref[0, T_DIAG] = diag
    bw_ref[0, T_NEAR] = near
    bw_ref[0, T_NONE] = jnp.full((tq, tq), NEG, F32)
    bw_ref[0, T_FAR] = jnp.full((tq, tq), far, F32)
    bw_ref[0, T_EDGE] = jnp.where(kk > qq, far, NEG)
    bw_ref[0, T_DIAG_REL] = jnp.where(qq >= kk, diag - far, NEG)
    bw_ref[0, T_NEAR_REL] = near - far
    cc = lax.broadcasted_iota(jnp.int32, (2 * ncp, tq), 0)
    qc = lax.broadcasted_iota(jnp.int32, (2 * ncp, tq), 1)
    pat_ref[0] = bias_of(qc - CMP_STRIDE * (cc - ncp) - (CMP_BLOCK - 1))


def _nsa_bias(rel, *, tq, ncp):
    return pl.pallas_call(
        functools.partial(_nsa_bias_kernel, tq=tq, ncp=ncp),
        out_shape=(jax.ShapeDtypeStruct((N_HEADS, N_TABLES, tq, tq), F32),
                   jax.ShapeDtypeStruct((N_HEADS, 2 * ncp, tq), F32)),
        grid=(N_HEADS,),
        in_specs=[pl.BlockSpec(memory_space=pltpu.SMEM)],
        out_specs=(pl.BlockSpec((1, N_TABLES, tq, tq), lambda h: (h, 0, 0, 0)),
                   pl.BlockSpec((1, 2 * ncp, tq), lambda h: (h, 0, 0))),
        compiler_params=_cparams(("arbitrary",)),
        name="nsa_bias",
    )(rel)


def _nsa_attn_kernel(q_ref, kc_ref, vc_ref, ks_ref, vs_ref, kw_ref, vw_ref, zg_ref, pat_ref,
                     bw_ref, msel_ref, o_ref, pen_ref, *, tq, ncp, nb):
    g = pl.program_id(1)
    qi = pl.program_id(2)
    cols = N_HG * tq
    qt = q_ref[0, 0, 0]

    def heads(fn):
        return jnp.concatenate([fn(h) for h in range(N_HG)], axis=1)

    def table(which, ok=None):
        idx = which if ok is None else jnp.where(ok, which, T_NONE)
        return heads(lambda h: bw_ref[h, idx])

    vrows = vs_ref.shape[2]

    def flash(carry, s, vt):
        m, acc = carry
        m_new = jnp.maximum(m, jnp.max(s, axis=0, keepdims=True))
        p = jnp.exp2(s - m_new)
        acc = jnp.exp2(m - m_new) * acc + _dot(vt, p.astype(BF16))
        return m_new, acc

    init = (jnp.full((1, cols), NEG, F32), jnp.zeros((vrows, cols), F32))

    def finish(carry):
        acc = carry[1]
        return acc[0:HEAD_DIM] / acc[HEAD_DIM:HEAD_DIM + 1]

    s_c = _dot(kc_ref[0, 0], qt)
    start = pl.multiple_of(ncp - qi * (tq // CMP_STRIDE), tq // CMP_STRIDE)
    bias_c = heads(lambda h: pat_ref[h, pl.ds(start, ncp), :])

    nd = WINDOW // tq
    s_parts, v_parts = [], []
    for d in range(nd, -1, -1):
        off = pl.multiple_of(jnp.maximum(qi - d, 0) * tq, tq)
        s = _dot(kw_ref[0, 0, pl.ds(off, tq), :], qt)
        if d == 0:
            s = s + table(T_DIAG)
        elif d == nd:
            s = s + table(T_EDGE, qi >= d)
        elif d == 1:
            s = s + table(T_NEAR, qi >= d)
        else:
            s = s + table(T_FAR, qi >= d)
        s_parts.append(s)
        v_parts.append(vw_ref[0, 0, :, pl.ds(off, tq)])
    o_w = finish(flash(init, jnp.concatenate(s_parts, axis=0), jnp.concatenate(v_parts, axis=1)))

    valid_c = bias_c > 0.5 * NEG
    sc = s_c + bias_c
    m_c = jnp.max(sc, axis=0, keepdims=True)
    e_c = jnp.where(valid_c, jnp.exp2(sc - m_c), 0.0)
    l_c = jnp.sum(e_c, axis=0, keepdims=True)
    p_c = e_c / jnp.where(l_c > 0.0, l_c, 1.0)
    o_c = _dot(vc_ref[0, 0], p_c.astype(BF16))

    psum = p_c[:, 0:tq]
    for h in range(1, N_HG):
        psum = psum + p_c[:, h * tq:(h + 1) * tq]
    imp_t = _split_dot(psum, msel_ref[...], 3, mode="ex")
    jj = lax.broadcasted_iota(jnp.int32, (nb, tq), 0)
    tt = qi * tq + lax.broadcasted_iota(jnp.int32, (nb, tq), 1)
    cur = tt // SEL_BLOCK
    forced = (jj == 0) | (jj == cur) | (jj == cur - 1)
    valid_b = jj * SEL_BLOCK <= tt
    imp = jnp.where(valid_b, jnp.where(forced, FORCE_SCORE, imp_t), -jnp.inf)
    groups8 = [imp[r:r + 8, :] for r in range(0, nb, 8)]
    jj8 = lax.broadcasted_iota(jnp.int32, (8, tq), 0)
    cnts = [jnp.zeros((8, tq), F32) for _ in groups8]
    for i in range(nb):
        ri = jnp.broadcast_to(imp[i:i + 1, :], (8, tq))
        for r, grp in enumerate(groups8):
            if 8 * r > i:
                beats = ri >= grp
            elif 8 * r + 7 <= i:
                beats = ri > grp
            else:
                beats = jnp.where(jj8 + 8 * r > i, jnp.where(ri >= grp, 1.0, 0.0), jnp.where(ri > grp, 1.0, 0.0)) > 0.5
            cnts[r] = cnts[r] + jnp.where(beats, 1.0, 0.0)
    cnt = jnp.concatenate(cnts, axis=0)
    sel_t = jnp.where((cnt < float(min(SEL_TOPK, nb))) & valid_b, 0.0, NEG)

    per_tile = tq // SEL_BLOCK
    for t in range(pen_ref.shape[0]):
        pen_ref[t] = jnp.concatenate([sel_t[t * per_tile:(t + 1) * per_tile, :],
                                      jnp.zeros((8 - per_tile, tq), F32)], axis=0)
    blk8 = lax.broadcasted_iota(jnp.int32, (8, tq), 0)

    def tile_pen(t, hi):
        rows = jnp.where(t * per_tile + blk8 >= hi, NEG, pen_ref[t])
        return jnp.concatenate([jnp.broadcast_to(rows[u:u + 1, :], (SEL_BLOCK, tq)) for u in range(per_tile)],
                               axis=0)

    def sel_scores(t0, ntile, hi):
        off = pl.multiple_of(t0 * tq, tq)
        s = _dot(ks_ref[0, 0, pl.ds(off, ntile * tq), :], qt)
        spen = jnp.concatenate([tile_pen(t0 + u, hi) for u in range(ntile)], axis=0)
        return s + heads(lambda h: spen)

    FAR_TILES = FAR_KEYS // tq
    nk = FAR_KEYS
    n_far = jnp.maximum(qi - 1, 0)
    nblk = (n_far + FAR_TILES - 1) // FAR_TILES

    def far_t0(j):
        return jnp.maximum(n_far - FAR_TILES * (j + 1), 0)

    def far_scores(j):
        return sel_scores(far_t0(j), FAR_TILES, (n_far - FAR_TILES * j) * per_tile)

    t1 = jnp.maximum(qi - 1, 0)
    s = jnp.concatenate([sel_scores(t1, 1, qi * per_tile) + table(T_NEAR_REL),
                         sel_scores(qi, 1, (qi + 1) * per_tile) + table(T_DIAG_REL)], axis=0)
    vx = jnp.concatenate([vs_ref[0, 0, :, pl.ds(pl.multiple_of(t1 * tq, tq), tq)],
                          vs_ref[0, 0, :, pl.ds(pl.multiple_of(qi * tq, tq), tq)]], axis=1)
    carry = flash(init, s, vx)

    def far_body(j, carry):
        vt = vs_ref[0, 0, :, pl.ds(pl.multiple_of(far_t0(j) * tq, tq), nk)]
        return flash(carry, far_scores(j), vt)

    o_s = finish(lax.fori_loop(0, nblk, far_body, carry))

    zg = pltpu.roll(zg_ref[...].astype(F32), (LANE - 16 * g) % LANE, axis=1)
    gates = _sigmoid(zg).T
    out = (heads(lambda h: gates[h:h + 1, :]) * o_c + heads(lambda h: gates[4 + h:5 + h, :]) * o_s
           + heads(lambda h: gates[8 + h:9 + h, :]) * o_w)
    for hp in range(N_HG // 2):
        pair = jnp.concatenate([out[:, (2 * hp) * tq:(2 * hp + 1) * tq],
                                out[:, (2 * hp + 1) * tq:(2 * hp + 2) * tq]], axis=0)
        o_ref[:, hp * 2 * HEAD_DIM:(hp + 1) * 2 * HEAD_DIM] = pair.T.astype(o_ref.dtype)


def _nsa_attn(qn, kc, vc, ks, vs, kw, vw, z, pat, bw, *, batch, seq, tq, ncp):
    nt = seq // tq
    nb = seq // SEL_BLOCK
    n_cmp = (seq - CMP_BLOCK) // CMP_STRIDE + 1
    ci = np.arange(ncp)[None, :] * CMP_STRIDE
    sj = np.arange(nb)[:, None] * SEL_BLOCK
    msel = ((ci <= sj + SEL_BLOCK - 1) & (ci + CMP_BLOCK - 1 >= sj) & (np.arange(ncp)[None, :] < n_cmp))
    msel_j = jnp.asarray(msel.astype(np.float32), BF16)
    assert nb <= LANE and seq >= FAR_KEYS and FAR_KEYS % tq == 0 and WINDOW // tq >= 2
    ks_spec = pl.BlockSpec((1, 1, seq, ks.shape[3]), lambda b, g, i: (b, g, 0, 0))
    kw_spec = pl.BlockSpec((1, 1, seq, HEAD_DIM), lambda b, g, i: (b, g, 0, 0))
    v_spec = pl.BlockSpec((1, 1, vs.shape[2], seq), lambda b, g, i: (b, g, 0, 0))
    kc_spec = pl.BlockSpec((1, 1, ncp, HEAD_DIM), lambda b, g, i: (b, g, 0, 0))
    vc_spec = pl.BlockSpec((1, 1, HEAD_DIM, ncp), lambda b, g, i: (b, g, 0, 0))
    width = N_HEADS * HEAD_DIM
    return pl.pallas_call(
        functools.partial(_nsa_attn_kernel, tq=tq, ncp=ncp, nb=nb),
        out_shape=jax.ShapeDtypeStruct((batch * seq, width), BF16),
        grid=(batch, N_KV, nt),
        in_specs=[
            pl.BlockSpec((1, 1, 1, HEAD_DIM, N_HG * tq), lambda b, g, i: (b, g, i, 0, 0)),
            kc_spec, vc_spec, ks_spec, v_spec, kw_spec, v_spec,
            pl.BlockSpec((tq, LANE), lambda b, g, i: (b * nt + i, C_GATE // LANE)),
            pl.BlockSpec((N_HG, 2 * ncp, tq), lambda b, g, i: (g, 0, 0)),
            pl.BlockSpec((N_HG, N_TABLES, tq, tq), lambda b, g, i: (g, 0, 0, 0)),
            pl.BlockSpec(msel_j.shape, lambda b, g, i: (0, 0)),
        ],
        out_specs=pl.BlockSpec((tq, N_HG * HEAD_DIM), lambda b, g, i: (b * nt + i, g)),
        scratch_shapes=[pltpu.VMEM((nt, 8, tq), F32)],
        compiler_params=_cparams(("arbitrary", "arbitrary", "arbitrary")),
        name="nsa_attn",
    )(qn, kc, vc, ks, vs, kw, vw, z, pat, bw, msel_j)


def _merge_cross_kernel(x_ref, yr_ref, yn_ref, zm_ref, wr_ref, wn_ref, wm_ref,
                        g_ref, wq_ref, qg_ref, k_ref, v_ref, wo_ref, o_ref):
    d = x_ref.shape[1]
    dh = d // CA_HEADS
    zm = zm_ref[...].astype(F32)
    br = _dot(yr_ref[...], wr_ref[...])
    bn = _dot(yn_ref[...], wn_ref[...])
    merged = _sigmoid(zm[:, 0:d]) * br + _sigmoid(zm[:, d:2 * d]) * bn
    h1 = x_ref[...] + _dot(merged.astype(BF16), wm_ref[...])

    xn = (_rms(h1, NORM_EPS) * g_ref[...]).astype(BF16)
    qf = _dot(xn, wq_ref[...])
    k = k_ref[0]
    v = v_ref[0]
    outs = []
    for h in range(CA_HEADS):
        sl = slice(h * dh, (h + 1) * dh)
        qh = (_rms(qf[:, sl], QK_EPS) * qg_ref[...] * (dh ** -0.5)).astype(BF16)
        s = _dot_nt(qh, k[:, sl])
        s = s - jnp.max(s, axis=-1, keepdims=True)
        e = jnp.exp(s)
        p = e / jnp.sum(e, axis=-1, keepdims=True)
        outs.append(_dot(p.astype(BF16), v[:, sl]))
    o = jnp.concatenate(outs, axis=1).astype(BF16)
    o_ref[...] = h1 + _dot(o, wo_ref[...])


def _merge_cross(x, yr, yn, z, wr, wn, wm, g, wq, qg, kn, vv, wo, *, batch, seq, tm):
    m, d = x.shape
    nt = seq // tm
    nm = kn.shape[1]
    row = lambda c: pl.BlockSpec((tm, c), lambda b, i: (b * nt + i, 0))
    full = lambda a: pl.BlockSpec(a.shape, lambda b, i: (0, 0))
    kv_spec = pl.BlockSpec((1, nm, d), lambda b, i: (b, 0, 0))
    return pl.pallas_call(
        _merge_cross_kernel,
        out_shape=jax.ShapeDtypeStruct((m, d), F32),
        grid=(batch, nt),
        in_specs=[row(d), row(d), row(d),
                  pl.BlockSpec((tm, 2 * d), lambda b, i: (b * nt + i, C_M // (2 * d))),
                  full(wr), full(wn), full(wm), full(g), full(wq), full(qg), kv_spec, kv_spec, full(wo)],
        out_specs=row(d),
        compiler_params=_cparams(("arbitrary", "arbitrary")),
        name="merge_cross",
    )(x, yr, yn, z, wr, wn, wm, g, wq, qg, kn, vv, wo)


def _ca_kv_kernel(mem_ref, g_ref, w_ref, kg_ref, k_ref, v_ref):
    d = mem_ref.shape[2]
    dh = d // CA_HEADS
    mn = (_rms(mem_ref[0], NORM_EPS) * g_ref[...]).astype(BF16)
    kv = _dot(mn, w_ref[...])
    ks = [(_rms(kv[:, h * dh:(h + 1) * dh], QK_EPS) * kg_ref[...]) for h in range(CA_HEADS)]
    k_ref[0] = jnp.concatenate(ks, axis=1).astype(k_ref.dtype)
    v_ref[0] = kv[:, d:2 * d].astype(v_ref.dtype)


def _ca_kv(mem, g, wkv, kg):
    batch, nm, d = mem.shape
    o_shape = jax.ShapeDtypeStruct((batch, nm, d), BF16)
    o_spec = pl.BlockSpec((1, nm, d), lambda b: (b, 0, 0))
    full = lambda a: pl.BlockSpec(a.shape, lambda b: (0, 0))
    return pl.pallas_call(
        _ca_kv_kernel,
        out_shape=(o_shape, o_shape),
        grid=(batch,),
        in_specs=[o_spec, full(g), full(wkv), full(kg)],
        out_specs=(o_spec, o_spec),
        compiler_params=_cparams(("arbitrary",)),
        name="ca_kv",
    )(mem, g, wkv, kg)


def _ffn_kernel(x_ref, g_ref, wa_ref, wb_ref, cw_ref, cb_ref, wd_ref, o_ref, xn_ref, acc_ref, carry_ref,
                *, tiles_per_seq):
    i = pl.program_id(0)
    j = pl.program_id(1)
    nj = pl.num_programs(1)
    tm = x_ref.shape[0]

    @pl.when(j == 0)
    def _():
        xn_ref[...] = (_rms(x_ref[...], NORM_EPS) * g_ref[...]).astype(BF16)
        acc_ref[...] = jnp.zeros_like(acc_ref)

    @pl.when(i % tiles_per_seq == 0)
    def _():
        carry_ref[j] = jnp.zeros(carry_ref.shape[1:], F32)

    xn = xn_ref[...]
    a = _dot(xn, wa_ref[...])
    b = _dot(xn, wb_ref[...])
    car = carry_ref[j]
    rowi = lax.broadcasted_iota(jnp.int32, a.shape, 0)
    p1 = jnp.where(rowi == 0, car[7:8, :], pltpu.roll(a, 1, axis=0))
    p2 = jnp.where(rowi == 0, car[6:7, :], jnp.where(rowi == 1, car[7:8, :], pltpu.roll(a, 2, axis=0)))
    carry_ref[j] = a[tm - 8:tm, :]
    cw = cw_ref[0]
    conv = cw[0:1, :] * p2 + cw[1:2, :] * p1 + cw[2:3, :] * a + cb_ref[0]
    act = conv * _sigmoid(conv) * b
    acc_ref[...] += _dot(act.astype(BF16), wd_ref[...])

    @pl.when(j == nj - 1)
    def _():
        o_ref[...] = x_ref[...] + acc_ref[...]


def _ffn(h2, g, wup, cw, cb, wd, *, seq, tm, tn):
    m, d = h2.shape
    dff = wd.shape[0]
    nj = dff // tn
    cw3 = jnp.zeros((nj, 8, tn), F32).at[:, 0:CONV_W, :].set(cw.reshape(CONV_W, nj, tn).transpose(1, 0, 2))
    cb3 = cb.reshape(nj, 1, tn)
    return pl.pallas_call(
        functools.partial(_ffn_kernel, tiles_per_seq=seq // tm),
        out_shape=jax.ShapeDtypeStruct((m, d), F32),
        grid=(m // tm, nj),
        in_specs=[
            pl.BlockSpec((tm, d), lambda i, j: (i, 0)),
            pl.BlockSpec((1, d), lambda i, j: (0, 0)),
            pl.BlockSpec((d, tn), lambda i, j: (0, j)),
            pl.BlockSpec((d, tn), lambda i, j: (0, nj + j)),
            pl.BlockSpec((1, 8, tn), lambda i, j: (j, 0, 0)),
            pl.BlockSpec((1, 1, tn), lambda i, j: (j, 0, 0)),
            pl.BlockSpec((tn, d), lambda i, j: (j, 0)),
        ],
        out_specs=pl.BlockSpec((tm, d), lambda i, j: (i, 0)),
        scratch_shapes=[pltpu.VMEM((tm, d), BF16), pltpu.VMEM((tm, d), F32), pltpu.VMEM((nj, 8, tn), F32)],
        compiler_params=_cparams(("arbitrary", "arbitrary")),
        name="ffn",
    )(h2, g, wup, wup, cw3, cb3, wd)


def _pack_perm(width):
    o_zw = 3 * width
    o_za = o_zw + R_LORA_W
    o_zg = o_za + R_LORA_A
    o_q = o_zg + R_LORA_G
    nkv = N_KV * HEAD_DIM
    o_kc = o_q + N_HEADS * HEAD_DIM
    o_gate = o_kc + 6 * nkv
    o_m = o_gate + 3 * N_HEADS
    perm = np.full((Z_COLS,), -1, np.int64)
    perm[C_RKV:C_RKV + 3 * width] = np.arange(3 * width)
    perm[C_Q:C_Q + N_HEADS * HEAD_DIM] = o_q + np.arange(N_HEADS * HEAD_DIM)
    perm[C_KC:C_KC + 6 * nkv] = o_kc + np.arange(6 * nkv)
    perm[C_ZG:C_ZG + R_LORA_G] = o_zg + np.arange(R_LORA_G)
    perm[C_ZWA:C_ZWA + R_LORA_W + R_LORA_A] = o_zw + np.arange(R_LORA_W + R_LORA_A)
    for g in range(N_KV):
        for c in range(3):
            for h in range(N_HG):
                perm[C_GATE + 16 * g + 4 * c + h] = o_gate + (g * N_HG + h) * 3 + c
    perm[C_M:C_M + 2 * width] = o_m + np.arange(2 * width)
    return perm, o_m + 2 * width


def _pad_rows(w, rows, offset=0):
    out = jnp.zeros((rows, w.shape[1]), w.dtype)
    return out.at[offset:offset + w.shape[0]].set(w)


def kernel(x, mem, rel_bias, norm_mix, w_in, rwkv_mu, rwkv_w0, rwkv_w2, rwkv_a0, rwkv_a2, rwkv_g2,
           rwkv_kk, rwkv_ka, rwkv_rk, rwkv_lnx_w, rwkv_lnx_b, nsa_q_gain, nsa_k_gain, cmp_pe_k, cmp_pe_v,
           cmp_w1_k, cmp_w2_k, cmp_w1_v, cmp_w2_v, w_branch_rwkv, w_branch_nsa, w_mix_out,
           norm_cross, norm_mem, ca_wq, ca_wkv, ca_q_gain, ca_k_gain, ca_wo,
           norm_ffn, ffn_up, ffn_conv, ffn_conv_b, ffn_down):
    batch, seq, d = x.shape
    depth = w_in.shape[0]
    width = d
    assert d == N_HEADS * HEAD_DIM and 2 * (Z_COLS - C_M) == 4 * d
    perm, in_cols = _pack_perm(width)
    assert w_in.shape[2] == in_cols
    perm_j = jnp.asarray(np.maximum(perm, 0), jnp.int32)
    keep = jnp.asarray(perm >= 0)

    tq = 256
    assert seq % 256 == 0 and tq > REL_MAX_DIST - 1 and WINDOW % tq == 0
    ncp = -(-(seq // CMP_STRIDE) // LANE) * LANE
    n_cmp = (seq - CMP_BLOCK) // CMP_STRIDE + 1
    tm = min(1024, seq)
    row = lambda v: v.reshape(1, -1).astype(F32)
    nkv = N_KV * HEAD_DIM

    h = x.reshape(batch * seq, d)
    for l in range(depth):
        w_in_p = jnp.where(keep[None, :], jnp.take(w_in[l], perm_j, axis=1), 0.0).astype(BF16)
        mu = rwkv_mu[l]
        mu_r = row(mu[0:3 * width])
        mu_wa = row(mu[3 * width:3 * width + R_LORA_W + R_LORA_A])
        mu_g = row(jnp.zeros((256,), F32).at[0:R_LORA_G].set(mu[3 * width + R_LORA_W + R_LORA_A:]))
        w2p = _pad_rows(rwkv_w2[l], LANE, 0).astype(BF16)
        a2p = _pad_rows(rwkv_a2[l], LANE, R_LORA_W).astype(BF16)
        g2p = _pad_rows(rwkv_g2[l], 256, 0).astype(BF16)

        z = _norm_matmul(h, row(norm_mix[l]), w_in_p, tm=tm, tn=1024, out_dtype=BF16, name="in_proj")

        y_r = _rwkv(z, mu_r, mu_g, mu_wa, row(rwkv_w0[l]), w2p, row(rwkv_a0[l]), a2p, g2p,
                    row(rwkv_kk[l]), row(rwkv_ka[l]), row(rwkv_rk[l]), row(rwkv_lnx_w[l]),
                    row(rwkv_lnx_b[l]), batch=batch, seq=seq, width=width)

        qn, ks, vs, kw, vw = _nsa_prep(
            z, row(jnp.tile(nsa_q_gain[l], N_HEADS)), row(jnp.tile(nsa_k_gain[l, 1], N_KV)),
            row(jnp.tile(nsa_k_gain[l, 2], N_KV)), batch=batch, seq=seq, tq=tq)
        z3 = z.reshape(batch, seq // CMP_STRIDE, CMP_STRIDE, Z_COLS)

        def groups(c0):
            t = z3[:, :, :, c0:c0 + nkv].reshape(batch, seq // CMP_STRIDE, CMP_STRIDE, N_KV, HEAD_DIM)
            return t.transpose(0, 3, 1, 2, 4).reshape(batch, N_KV, seq // CMP_STRIDE, CMP_STRIDE * HEAD_DIM)

        kc, vc = _nsa_cmp(groups(C_KC), groups(C_KC + nkv), cmp_pe_k[l].reshape(1, -1), cmp_pe_v[l].reshape(1, -1),
                          cmp_w1_k[l].astype(BF16), cmp_w2_k[l].astype(BF16), cmp_w1_v[l].astype(BF16),
                          cmp_w2_v[l].astype(BF16), row(nsa_k_gain[l, 0]), ncp=ncp, n_cmp=n_cmp)
        bw, pat = _nsa_bias(rel_bias.astype(F32), tq=tq, ncp=ncp)
        y_n = _nsa_attn(qn, kc, vc, ks, vs, kw, vw, z, pat, bw,
                        batch=batch, seq=seq, tq=tq, ncp=ncp)

        kn, vv = _ca_kv(mem, row(norm_mem[l]), ca_wkv[l].astype(BF16), row(ca_k_gain[l]))
        h2 = _merge_cross(h, y_r, y_n, z, w_branch_rwkv[l].astype(BF16), w_branch_nsa[l].astype(BF16),
                          w_mix_out[l].astype(BF16), row(norm_cross[l]), ca_wq[l].astype(BF16),
                          row(ca_q_gain[l]), kn, vv, ca_wo[l].astype(BF16),
                          batch=batch, seq=seq, tm=min(512, seq))

        h = _ffn(h2, row(norm_ffn[l]), ffn_up[l].astype(BF16), ffn_conv[l], ffn_conv_b[l],
                 ffn_down[l].astype(BF16), seq=seq, tm=tm, tn=256)
    return h.reshape(batch, seq, d)
```

```python
import functools
import math

import jax
import jax.numpy as jnp
import numpy as np
from jax import lax
from jax.experimental import pallas as pl
from jax.experimental.pallas import tpu as pltpu

F32 = jnp.float32
BF16 = jnp.bfloat16
NEG = -1e30
LOG2E = 1.4426950408889634
FAR_KEYS = 512
T_DIAG, T_NEAR, T_NONE, T_FAR, T_EDGE, T_DIAG_REL, T_NEAR_REL = range(7)
N_TABLES = 7
V_ROWS = 80

R_HEAD = 64
R_LORA_W = 64
R_LORA_A = 64
R_LORA_G = 160
LNX_EPS = 64e-5
N_HEADS = 16
N_KV = 4
N_HG = N_HEADS // N_KV
HEAD_DIM = 64
CMP_BLOCK = 32
CMP_STRIDE = 16
CMP_HIDDEN = 256
SEL_BLOCK = 64
SEL_TOPK = 16
WINDOW = 512
FORCE_SCORE = 1e4
REL_BUCKETS = 32
REL_MAX_DIST = 128
CA_HEADS = 4
CONV_W = 3
NORM_EPS = 1e-6
QK_EPS = 1e-6

LANE = 128
VMEM_LIMIT = 56 * 1024 * 1024

C_RKV = 0
C_Q = 3072
C_KC = 4096
C_ZG = 5632
C_ZWA = 5888
C_GATE = 6016
C_M = 6144
Z_COLS = 8192


def _dot(a, b):
    return jnp.dot(a, b, preferred_element_type=F32)


def _dot_nt(a, b):
    return lax.dot_general(a, b, (((1,), (1,)), ((), ())), preferred_element_type=F32)


def _split_dot(x, e, parts, mode="xe"):
    acc = None
    rem = x
    for i in range(parts):
        hi = rem.astype(BF16)
        t = _dot(hi, e) if mode == "xe" else (_dot(e, hi) if mode == "ex" else _dot_nt(e, hi))
        acc = t if acc is None else acc + t
        if i + 1 < parts:
            rem = rem - hi.astype(F32)
    return acc


def _rms(x, eps):
    return x * lax.rsqrt(jnp.mean(x * x, axis=-1, keepdims=True) + eps)


def _sigmoid(x):
    return 1.0 / (1.0 + jnp.exp(-x))


def _cparams(sem):
    return pltpu.CompilerParams(dimension_semantics=sem, vmem_limit_bytes=VMEM_LIMIT)


def _norm_matmul_kernel(x_ref, g_ref, w_ref, o_ref, xn_ref):
    @pl.when(pl.program_id(1) == 0)
    def _():
        x = x_ref[...]
        xn_ref[...] = (_rms(x, NORM_EPS) * g_ref[...]).astype(BF16)

    o_ref[...] = _dot(xn_ref[...], w_ref[...]).astype(o_ref.dtype)


def _norm_matmul(x, g, w, *, tm, tn, out_dtype, name):
    m, k = x.shape
    n = w.shape[1]
    return pl.pallas_call(
        _norm_matmul_kernel,
        out_shape=jax.ShapeDtypeStruct((m, n), out_dtype),
        grid=(m // tm, n // tn),
        in_specs=[
            pl.BlockSpec((tm, k), lambda i, j: (i, 0)),
            pl.BlockSpec((1, k), lambda i, j: (0, 0)),
            pl.BlockSpec((k, tn), lambda i, j: (0, j)),
        ],
        out_specs=pl.BlockSpec((tm, tn), lambda i, j: (i, j)),
        scratch_shapes=[pltpu.VMEM((tm, k), BF16)],
        compiler_params=_cparams(("arbitrary", "arbitrary")),
        name=name,
    )(x, g, w)


def _rwkv_kernel(zr_ref, zg_ref, zwa_ref, mur_ref, mug_ref, muwa_ref, w0_ref, w2_ref, a0_ref, a2_ref,
                 g2_ref, kk_ref, ka_ref, rk_ref, lnw_ref, lnb_ref, eh_ref, eht_ref, tri_ref,
                 o_ref, pr_ref, pg_ref, pwa_ref, st_ref, *, L, width):
    i = pl.program_id(1)
    npair = width // LANE
    tb = o_ref.shape[0]

    @pl.when(i == 0)
    def _():
        pr_ref[...] = jnp.zeros_like(pr_ref)
        pg_ref[...] = jnp.zeros_like(pg_ref)
        pwa_ref[...] = jnp.zeros_like(pwa_ref)
        st_ref[...] = jnp.zeros_like(st_ref)

    def shifted8(z, prev_ref, mu):
        rolled = pltpu.roll(z, 1, axis=0)
        row0 = lax.broadcasted_iota(jnp.int32, z.shape, 0) == 0
        prev = jnp.where(row0, prev_ref[7:8, :], rolled)
        prev_ref[...] = z[tb - 8:tb, :]
        return z + (prev - z) * mu

    zs = shifted8(zr_ref[...].astype(F32), pr_ref, mur_ref[...])
    zsg = shifted8(zg_ref[...].astype(F32), pg_ref, mug_ref[...])
    zswa = shifted8(zwa_ref[...].astype(F32), pwa_ref, muwa_ref[...])

    r = zs[:, 0:width]
    k = zs[:, width:2 * width]
    v = zs[:, 2 * width:3 * width]

    w_lin = w0_ref[...] + _dot(jnp.tanh(zswa).astype(BF16), w2_ref[...])
    a_lin = a0_ref[...] + _dot(zswa.astype(BF16), a2_ref[...])
    y = -w_lin
    softplus = jnp.maximum(y, 0.0) + jnp.log(1.0 + jnp.exp(-jnp.abs(y)))
    ld = -jnp.exp(-softplus - 0.5)
    a = _sigmoid(a_lin)
    g = _dot(_sigmoid(zsg).astype(BF16), g2_ref[...])

    eh = eh_ref[...]
    eht = eht_ref[...]

    def sum_heads(t):
        return _split_dot(t, eh, 2)

    def bcast_heads(t):
        return _split_dot(t, eht, 2)

    kkr = k * kk_ref[...]
    nrm = jnp.maximum(jnp.sqrt(sum_heads(kkr * kkr)), 1e-12)
    kkn = kkr * bcast_heads(1.0 / nrm)
    k2 = k * (1.0 + (a - 1.0) * ka_ref[...])
    av = -kkn
    bv = kkn * a
    bonus = bcast_heads(sum_heads(r * k2 * rk_ref[...])) * v

    nch = tb // L
    lg = _split_dot(ld, tri_ref[...], 3, mode="ex")
    lasts = [lg[(c + 1) * L - 1:(c + 1) * L, :] for c in range(nch)]
    lg_last = jnp.concatenate([jnp.broadcast_to(t, (L, width)) for t in lasts], axis=0)
    eg = jnp.exp(lg)
    eng = jnp.exp(-lg)
    egl = jnp.exp(lg_last - lg)
    rt = r * eg
    kt = k2 * eng
    bt = bv * eng
    at = av * jnp.exp(lg - ld)
    kgl = k2 * egl
    bgl = bv * egl
    gl = [jnp.exp(t) for t in lasts]

    lane = lax.broadcasted_iota(jnp.int32, (L, LANE), 1)
    lo = lane < R_HEAD

    def stack(t):
        return jnp.concatenate([jnp.where(lo, t, 0.0), jnp.where(lo, 0.0, t)], axis=0)

    row = lax.broadcasted_iota(jnp.int32, (2 * L, 2 * L), 0)
    col = lax.broadcasted_iota(jnp.int32, (2 * L, 2 * L), 1)
    strict = row > col
    incl = row >= col
    eye = (row == col).astype(F32)
    nsq = int(math.log2(L)) - 1

    pairs = range(npair)
    chains = [(c, p) for c in range(nch) for p in pairs]
    sub = lambda t, c, p: t[c * L:(c + 1) * L, p * LANE:(p + 1) * LANE]
    bf = lambda t: t.astype(BF16)
    ar = {cp: bf(jnp.concatenate([stack(sub(at, *cp)), stack(sub(rt, *cp))], axis=0)) for cp in chains}
    bk = {cp: bf(jnp.concatenate([stack(sub(bt, *cp)), stack(sub(kt, *cp))], axis=0)) for cp in chains}
    v_s = {cp: bf(stack(sub(v, *cp))) for cp in chains}
    gm = {cp: _dot_nt(ar[cp], bk[cp]) for cp in chains}
    a_ab = {cp: jnp.where(strict, gm[cp][0:2 * L, 0:2 * L], 0.0) for cp in chains}
    a_ak = {cp: bf(jnp.where(strict, gm[cp][0:2 * L, 2 * L:4 * L], 0.0)) for cp in chains}
    a_rb = {cp: bf(jnp.where(incl, gm[cp][2 * L:4 * L, 0:2 * L], 0.0)) for cp in chains}
    a_rk = {cp: bf(jnp.where(incl, gm[cp][2 * L:4 * L, 2 * L:4 * L], 0.0)) for cp in chains}
    tinv = {cp: eye + a_ab[cp] for cp in chains}
    pw = a_ab
    for _ in range(nsq):
        pwb = {cp: bf(pw[cp]) for cp in chains}
        pw = {cp: _dot(pwb[cp], pwb[cp]) for cp in chains}
        tinv = {cp: tinv[cp] + _dot(bf(pw[cp]), bf(tinv[cp])) for cp in chains}
    tinv = {cp: bf(tinv[cp]) for cp in chains}
    bkgt = {cp: bf(jnp.concatenate([stack(sub(bgl, *cp)), stack(sub(kgl, *cp))], axis=0).T) for cp in chains}
    glcol = {(c, p): jnp.sum(eye * gl[c][:, p * LANE:(p + 1) * LANE], axis=1, keepdims=True)
             for (c, p) in chains}
    h = [st_ref[p] for p in pairs]
    ys = []
    for c in range(nch):
        hb = [bf(h[p]) for p in pairs]
        wmat = [_dot(jnp.concatenate([ar[c, p][0:2 * L], a_ak[c, p]], axis=1),
                     jnp.concatenate([hb[p], v_s[c, p]], axis=0)) for p in pairs]
        ub = [bf(_dot(tinv[c, p], bf(wmat[p]))) for p in pairs]
        yy = [_dot(jnp.concatenate([ar[c, p][2 * L:4 * L], a_rb[c, p], a_rk[c, p]], axis=1),
                   jnp.concatenate([hb[p], ub[p], v_s[c, p]], axis=0)) for p in pairs]
        h = [h[p] * glcol[c, p] + _dot(bkgt[c, p], jnp.concatenate([ub[p], v_s[c, p]], axis=0))
             for p in pairs]
        ys.append(jnp.concatenate([yy[p][0:L] + yy[p][L:2 * L] for p in pairs], axis=1))
    for p in pairs:
        st_ref[p] = h[p]

    yv = jnp.concatenate(ys, axis=0)
    inv_n = 1.0 / R_HEAD
    mean = bcast_heads(sum_heads(yv) * inv_n)
    yc = yv - mean
    var = bcast_heads(sum_heads(yc * yc) * inv_n)
    yn = yc * lax.rsqrt(var + LNX_EPS) * lnw_ref[...] + lnb_ref[...]
    o_ref[...] = ((yn + bonus) * g).astype(o_ref.dtype)


def _rwkv(z, mu_r, mu_g, mu_wa, w0, w2p, a0, a2p, g2p, kk, ka, rk, lnw, lnb, *, batch, seq, width):
    L = 64
    tb = 4 * L
    nt = seq // tb
    nheads = width // R_HEAD
    eh = np.zeros((width, LANE), np.float32)
    eh[np.arange(width), np.arange(width) // R_HEAD] = 1.0
    ti = np.arange(tb)
    tri = ((ti[:, None] >= ti[None, :]) & (ti[:, None] // L == ti[None, :] // L)).astype(np.float32)
    vec = lambda c: pl.BlockSpec((1, c), lambda b, i: (0, 0))
    full = lambda a: pl.BlockSpec(a.shape, lambda b, i: (0,) * a.ndim)
    eh_j = jnp.asarray(eh, BF16)
    eht_j = jnp.asarray(eh.T, BF16)
    tri_j = jnp.asarray(tri, BF16)
    return pl.pallas_call(
        functools.partial(_rwkv_kernel, L=L, width=width),
        out_shape=jax.ShapeDtypeStruct((batch * seq, width), BF16),
        grid=(batch, nt),
        in_specs=[
            pl.BlockSpec((tb, 3 * width), lambda b, i: (b * nt + i, C_RKV // (3 * width))),
            pl.BlockSpec((tb, 256), lambda b, i: (b * nt + i, C_ZG // 256)),
            pl.BlockSpec((tb, LANE), lambda b, i: (b * nt + i, C_ZWA // LANE)),
            vec(3 * width), vec(256), vec(LANE),
            vec(width), full(w2p), vec(width), full(a2p), full(g2p),
            vec(width), vec(width), vec(width), vec(width), vec(width),
            full(eh_j), full(eht_j), full(tri_j),
        ],
        out_specs=pl.BlockSpec((tb, width), lambda b, i: (b * nt + i, 0)),
        scratch_shapes=[
            pltpu.VMEM((8, 3 * width), F32),
            pltpu.VMEM((8, 256), F32),
            pltpu.VMEM((8, LANE), F32),
            pltpu.VMEM((nheads // 2, 2 * R_HEAD, 2 * R_HEAD), F32),
        ],
        compiler_params=_cparams(("arbitrary", "arbitrary")),
        name="rwkv",
    )(z, z, z, mu_r, mu_g, mu_wa, w0, w2p, a0, a2p, g2p, kk, ka, rk, lnw, lnb, eh_j, eht_j, tri_j)


def _nsa_prep_kernel(zq_ref, zs_ref, zw_ref, qg_ref, kgs_ref, kgw_ref, ehq_ref, ehqt_ref,
                     q_ref, ks_ref, vs_ref, kw_ref, vw_ref):
    inv_d = 1.0 / HEAD_DIM

    def head_norm(x, gain, e, et):
        ss = _split_dot(x * x, e, 2)
        inv = lax.rsqrt(ss * inv_d + QK_EPS)
        return x * _split_dot(inv, et, 2) * gain

    nk = N_KV * HEAD_DIM
    ehq = ehq_ref[...]
    ehqt = ehqt_ref[...]
    q = head_norm(zq_ref[...].astype(F32), qg_ref[...], ehq, ehqt) * (HEAD_DIM ** -0.5 * LOG2E)
    qt = q.T
    for gq in range(N_KV):
        rows = [qt[(gq * N_HG + h) * HEAD_DIM:(gq * N_HG + h + 1) * HEAD_DIM, :] for h in range(N_HG)]
        q_ref[0, gq, 0] = jnp.concatenate(rows, axis=1).astype(q_ref.dtype)
    zs = zs_ref[...].astype(F32)
    zw = zw_ref[...].astype(F32)
    ehk = ehq[0:nk, :]
    ehkt = ehqt[:, 0:nk]
    ksn = head_norm(zs[:, 0:nk], kgs_ref[...], ehk, ehkt)
    kwn = head_norm(zw[:, 0:nk], kgw_ref[...], ehk, ehkt)
    vst = zs[:, nk:2 * nk].T
    vwt = zw[:, nk:2 * nk].T
    tb = zs.shape[0]
    vextra = jnp.where(lax.broadcasted_iota(jnp.int32, (V_ROWS - HEAD_DIM, tb), 0) == 0, 1.0, 0.0)
    for gq in range(N_KV):
        sl = slice(gq * HEAD_DIM, (gq + 1) * HEAD_DIM)
        ks_ref[0, gq] = ksn[:, sl].astype(ks_ref.dtype)
        vs_ref[0, gq] = jnp.concatenate([vst[sl, :], vextra], axis=0).astype(vs_ref.dtype)
        kw_ref[0, gq] = kwn[:, sl].astype(kw_ref.dtype)
        vw_ref[0, gq] = jnp.concatenate([vwt[sl, :], vextra], axis=0).astype(vw_ref.dtype)


def _nsa_prep(z, q_gain_t, kgs_t, kgw_t, *, batch, seq, tq):
    tb = tq
    nt = seq // tb
    nq = N_HEADS * HEAD_DIM
    nk = N_KV * HEAD_DIM
    eh = np.zeros((nq, LANE), np.float32)
    eh[np.arange(nq), np.arange(nq) // HEAD_DIM] = 1.0
    eh_j = jnp.asarray(eh, BF16)
    eht_j = jnp.asarray(eh.T, BF16)
    k_shape = jax.ShapeDtypeStruct((batch, N_KV, seq, HEAD_DIM), BF16)
    k_spec = pl.BlockSpec((1, N_KV, tb, HEAD_DIM), lambda b, i: (b, 0, i, 0))
    v_shape = jax.ShapeDtypeStruct((batch, N_KV, V_ROWS, seq), BF16)
    v_spec = pl.BlockSpec((1, N_KV, V_ROWS, tb), lambda b, i: (b, 0, 0, i))
    return pl.pallas_call(
        _nsa_prep_kernel,
        out_shape=(jax.ShapeDtypeStruct((batch, N_KV, nt, HEAD_DIM, N_HG * tq), BF16),
                   k_shape, v_shape, k_shape, v_shape),
        grid=(batch, nt),
        in_specs=[
            pl.BlockSpec((tb, nq), lambda b, i: (b * nt + i, C_Q // nq)),
            pl.BlockSpec((tb, 2 * nk), lambda b, i: (b * nt + i, C_KC // (2 * nk) + 1)),
            pl.BlockSpec((tb, 2 * nk), lambda b, i: (b * nt + i, C_KC // (2 * nk) + 2)),
            pl.BlockSpec((1, nq), lambda b, i: (0, 0)),
            pl.BlockSpec((1, nk), lambda b, i: (0, 0)),
            pl.BlockSpec((1, nk), lambda b, i: (0, 0)),
            pl.BlockSpec(eh_j.shape, lambda b, i: (0, 0)),
            pl.BlockSpec(eht_j.shape, lambda b, i: (0, 0)),
        ],
        out_specs=(pl.BlockSpec((1, N_KV, 1, HEAD_DIM, N_HG * tq), lambda b, i: (b, 0, i, 0, 0)),
                   k_spec, v_spec, k_spec, v_spec),
        compiler_params=_cparams(("arbitrary", "arbitrary")),
        name="nsa_prep",
    )(z, z, z, q_gain_t, kgs_t, kgw_t, eh_j, eht_j)


def _nsa_cmp_kernel(ak_ref, av_ref, pek_ref, pev_ref, w1k_ref, w2k_ref, w1v_ref, w2v_ref, kg_ref,
                    kc_ref, vc_ref, *, n_cmp):
    ncp = kc_ref.shape[2]
    half = w1k_ref.shape[0] // 2

    def compress(a_ref, pe_ref, w1_ref, w2_ref):
        rows = a_ref.shape[2]
        a = a_ref[0, 0].astype(BF16)
        p1 = _dot(a, w1_ref[0:half, :])
        p2 = _dot(a, w1_ref[half:2 * half, :])
        pe8 = jnp.broadcast_to(pe_ref[...], (8, pe_ref.shape[1])).astype(BF16)
        pe_bias = _dot(pe8, w1_ref[...])[0:1, :]
        hpre = p1 + pltpu.roll(p2, rows - 1, axis=0) + pe_bias
        hid = 0.5 * hpre * (1.0 + jnp.tanh(math.sqrt(2.0 / math.pi) * (hpre + 0.044715 * hpre * hpre * hpre)))
        out = _dot(hid.astype(BF16), w2_ref[...])
        if rows < ncp:
            out = jnp.concatenate([out, jnp.zeros((ncp - rows, out.shape[1]), F32)], axis=0)
        return out

    valid = lax.broadcasted_iota(jnp.int32, (ncp, HEAD_DIM), 0) < n_cmp
    kc = compress(ak_ref, pek_ref, w1k_ref, w2k_ref)
    kc = _rms(kc, QK_EPS) * kg_ref[...]
    kc_ref[0, 0] = jnp.where(valid, kc, 0.0).astype(kc_ref.dtype)
    vc = jnp.where(valid, compress(av_ref, pev_ref, w1v_ref, w2v_ref), 0.0)
    vct = jnp.concatenate([vc, jnp.zeros((ncp, LANE - HEAD_DIM), F32)], axis=1).T
    vc_ref[0, 0] = vct[0:HEAD_DIM, :].astype(vc_ref.dtype)


def _nsa_cmp(ak, av, pek, pev, w1k, w2k, w1v, w2v, kg, *, ncp, n_cmp):
    batch, ng, rows, wid = ak.shape
    a_spec = pl.BlockSpec((1, 1, rows, wid), lambda b, g: (b, g, 0, 0))
    full = lambda a: pl.BlockSpec(a.shape, lambda b, g: (0,) * a.ndim)
    k_shape = jax.ShapeDtypeStruct((batch, ng, ncp, HEAD_DIM), BF16)
    k_spec = pl.BlockSpec((1, 1, ncp, HEAD_DIM), lambda b, g: (b, g, 0, 0))
    v_shape = jax.ShapeDtypeStruct((batch, ng, HEAD_DIM, ncp), BF16)
    v_spec = pl.BlockSpec((1, 1, HEAD_DIM, ncp), lambda b, g: (b, g, 0, 0))
    return pl.pallas_call(
        functools.partial(_nsa_cmp_kernel, n_cmp=n_cmp),
        out_shape=(k_shape, v_shape),
        grid=(batch, ng),
        in_specs=[a_spec, a_spec, full(pek), full(pev), full(w1k), full(w2k), full(w1v), full(w2v), full(kg)],
        out_specs=(k_spec, v_spec),
        compiler_params=_cparams(("arbitrary", "arbitrary")),
        name="nsa_cmp",
    )(ak, av, pek, pev, w1k, w2k, w1v, w2v, kg)


def _bucket_thresholds():
    exact = REL_BUCKETS // 2
    n = np.arange(0, REL_MAX_DIST + 1)
    nf = np.maximum(n, 1).astype(np.float64)
    large = exact + (np.log(nf / exact) / math.log(REL_MAX_DIST / exact) * (REL_BUCKETS - exact)).astype(np.int64)
    large = np.minimum(large, REL_BUCKETS - 1)
    bucket = np.where(n < exact, n, large)
    assert bucket[-1] == REL_BUCKETS - 1
    return [int(np.argmax(bucket >= b)) for b in range(REL_BUCKETS)]


def _nsa_bias_kernel(rel_ref, bw_ref, pat_ref, *, tq, ncp):
    h = pl.program_id(0)
    thr = _bucket_thresholds()

    def bias_of(d):
        val = jnp.full(d.shape, rel_ref[0, h], F32)
        for b in range(1, REL_BUCKETS):
            val = jnp.where(d >= thr[b], rel_ref[b, h], val)
        return jnp.where(d < 0, NEG, val * LOG2E)

    kk = lax.broadcasted_iota(jnp.int32, (tq, tq), 0)
    qq = lax.broadcasted_iota(jnp.int32, (tq, tq), 1)
    far = rel_ref[REL_BUCKETS - 1, h] * LOG2E
    diag = bias_of(qq - kk)
    near = bias_of(qq - kk + tq)
    bw_ref[0, T_DIAG] = diag
    bw_ref[0, T_NEAR] = near
    bw_ref[0, T_NONE] = jnp.full((tq, tq), NEG, F32)
    bw_ref[0, T_FAR] = jnp.full((tq, tq), far, F32)
    bw_ref[0, T_EDGE] = jnp.where(kk > qq, far, NEG)
    bw_ref[0, T_DIAG_REL] = jnp.where(qq >= kk, diag - far, NEG)
    bw_ref[0, T_NEAR_REL] = near - far
    cc = lax.broadcasted_iota(jnp.int32, (2 * ncp, tq), 0)
    qc = lax.broadcasted_iota(jnp.int32, (2 * ncp, tq), 1)
    pat_ref[0] = bias_of(qc - CMP_STRIDE * (cc - ncp) - (CMP_BLOCK - 1))


def _nsa_bias(rel, *, tq, ncp):
    return pl.pallas_call(
        functools.partial(_nsa_bias_kernel, tq=tq, ncp=ncp),
        out_shape=(jax.ShapeDtypeStruct((N_HEADS, N_TABLES, tq, tq), F32),
                   jax.ShapeDtypeStruct((N_HEADS, 2 * ncp, tq), F32)),
        grid=(N_HEADS,),
        in_specs=[pl.BlockSpec(memory_space=pltpu.SMEM)],
        out_specs=(pl.BlockSpec((1, N_TABLES, tq, tq), lambda h: (h, 0, 0, 0)),
                   pl.BlockSpec((1, 2 * ncp, tq), lambda h: (h, 0, 0))),
        compiler_params=_cparams(("arbitrary",)),
        name="nsa_bias",
    )(rel)


def _nsa_attn_kernel(q_ref, kc_ref, vc_ref, ks_ref, vs_ref, kw_ref, vw_ref, zg_ref, pat_ref,
                     bw_ref, msel_ref, o_ref, pen_ref, *, tq, ncp, nb):
    g = pl.program_id(1)
    qi = pl.program_id(2)
    cols = N_HG * tq
    qt = q_ref[0, 0, 0]

    def heads(fn):
        return jnp.concatenate([fn(h) for h in range(N_HG)], axis=1)

    def table(which, ok=None):
        idx = which if ok is None else jnp.where(ok, which, T_NONE)
        return heads(lambda h: bw_ref[h, idx])

    vrows = vs_ref.shape[2]

    def flash(carry, s, vt):
        m, acc = carry
        m_new = jnp.maximum(m, jnp.max(s, axis=0, keepdims=True))
        p = jnp.exp2(s - m_new)
        acc = jnp.exp2(m - m_new) * acc + _dot(vt, p.astype(BF16))
        return m_new, acc

    init = (jnp.full((1, cols), NEG, F32), jnp.zeros((vrows, cols), F32))

    def finish(carry):
        acc = carry[1]
        return acc[0:HEAD_DIM] / acc[HEAD_DIM:HEAD_DIM + 1]

    s_c = _dot(kc_ref[0, 0], qt)
    start = pl.multiple_of(ncp - qi * (tq // CMP_STRIDE), tq // CMP_STRIDE)
    bias_c = heads(lambda h: pat_ref[h, pl.ds(start, ncp), :])

    nd = WINDOW // tq
    s_parts, v_parts = [], []
    for d in range(nd, -1, -1):
        off = pl.multiple_of(jnp.maximum(qi - d, 0) * tq, tq)
        s = _dot(kw_ref[0, 0, pl.ds(off, tq), :], qt)
        if d == 0:
            s = s + table(T_DIAG)
        elif d == nd:
            s = s + table(T_EDGE, qi >= d)
        elif d == 1:
            s = s + table(T_NEAR, qi >= d)
        else:
            s = s + table(T_FAR, qi >= d)
        s_parts.append(s)
        v_parts.append(vw_ref[0, 0, :, pl.ds(off, tq)])
    o_w = finish(flash(init, jnp.concatenate(s_parts, axis=0), jnp.concatenate(v_parts, axis=1)))

    valid_c = bias_c > 0.5 * NEG
    sc = s_c + bias_c
    m_c = jnp.max(sc, axis=0, keepdims=True)
    e_c = jnp.where(valid_c, jnp.exp2(sc - m_c), 0.0)
    l_c = jnp.sum(e_c, axis=0, keepdims=True)
    p_c = e_c / jnp.where(l_c > 0.0, l_c, 1.0)
    o_c = _dot(vc_ref[0, 0], p_c.astype(BF16))

    psum = p_c[:, 0:tq]
    for h in range(1, N_HG):
        psum = psum + p_c[:, h * tq:(h + 1) * tq]
    imp_t = _split_dot(psum, msel_ref[...], 3, mode="ex")
    jj = lax.broadcasted_iota(jnp.int32, (nb, tq), 0)
    tt = qi * tq + lax.broadcasted_iota(jnp.int32, (nb, tq), 1)
    cur = tt // SEL_BLOCK
    forced = (jj == 0) | (jj == cur) | (jj == cur - 1)
    valid_b = jj * SEL_BLOCK <= tt
    imp = jnp.where(valid_b, jnp.where(forced, FORCE_SCORE, imp_t), -jnp.inf)
    groups8 = [imp[r:r + 8, :] for r in range(0, nb, 8)]
    jj8 = lax.broadcasted_iota(jnp.int32, (8, tq), 0)
    cnts = [jnp.zeros((8, tq), F32) for _ in groups8]
    for i in range(nb):
        ri = jnp.broadcast_to(imp[i:i + 1, :], (8, tq))
        for r, grp in enumerate(groups8):
            if 8 * r > i:
                beats = ri >= grp
            elif 8 * r + 7 <= i:
                beats = ri > grp
            else:
                beats = jnp.where(jj8 + 8 * r > i, jnp.where(ri >= grp, 1.0, 0.0), jnp.where(ri > grp, 1.0, 0.0)) > 0.5
            cnts[r] = cnts[r] + jnp.where(beats, 1.0, 0.0)
    cnt = jnp.concatenate(cnts, axis=0)
    sel_t = jnp.where((cnt < float(min(SEL_TOPK, nb))) & valid_b, 0.0, NEG)

    per_tile = tq // SEL_BLOCK
    for t in range(pen_ref.shape[0]):
        pen_ref[t] = jnp.concatenate([sel_t[t * per_tile:(t + 1) * per_tile, :],
                                      jnp.zeros((8 - per_tile, tq), F32)], axis=0)
    blk8 = lax.broadcasted_iota(jnp.int32, (8, tq), 0)

    def tile_pen(t, hi):
        rows = jnp.where(t * per_tile + blk8 >= hi, NEG, pen_ref[t])
        return jnp.concatenate([jnp.broadcast_to(rows[u:u + 1, :], (SEL_BLOCK, tq)) for u in range(per_tile)],
                               axis=0)

    def sel_scores(t0, ntile, hi):
        off = pl.multiple_of(t0 * tq, tq)
        s = _dot(ks_ref[0, 0, pl.ds(off, ntile * tq), :], qt)
        spen = jnp.concatenate([tile_pen(t0 + u, hi) for u in range(ntile)], axis=0)
        return s + heads(lambda h: spen)

    FAR_TILES = FAR_KEYS // tq
    nk = FAR_KEYS
    n_far = jnp.maximum(qi - 1, 0)
    nblk = (n_far + FAR_TILES - 1) // FAR_TILES

    def far_t0(j):
        return jnp.maximum(n_far - FAR_TILES * (j + 1), 0)

    def far_scores(j):
        return sel_scores(far_t0(j), FAR_TILES, (n_far - FAR_TILES * j) * per_tile)

    t1 = jnp.maximum(qi - 1, 0)
    s = jnp.concatenate([sel_scores(t1, 1, qi * per_tile) + table(T_NEAR_REL),
                         sel_scores(qi, 1, (qi + 1) * per_tile) + table(T_DIAG_REL)], axis=0)
    vx = jnp.concatenate([vs_ref[0, 0, :, pl.ds(pl.multiple_of(t1 * tq, tq), tq)],
                          vs_ref[0, 0, :, pl.ds(pl.multiple_of(qi * tq, tq), tq)]], axis=1)
    carry = flash(init, s, vx)

    def far_body(j, carry):
        vt = vs_ref[0, 0, :, pl.ds(pl.multiple_of(far_t0(j) * tq, tq), nk)]
        return flash(carry, far_scores(j), vt)

    o_s = finish(lax.fori_loop(0, nblk, far_body, carry))

    zg = pltpu.roll(zg_ref[...].astype(F32), (LANE - 16 * g) % LANE, axis=1)
    gates = _sigmoid(zg).T
    out = (heads(lambda h: gates[h:h + 1, :]) * o_c + heads(lambda h: gates[4 + h:5 + h, :]) * o_s
           + heads(lambda h: gates[8 + h:9 + h, :]) * o_w)
    for hp in range(N_HG // 2):
        pair = jnp.concatenate([out[:, (2 * hp) * tq:(2 * hp + 1) * tq],
                                out[:, (2 * hp + 1) * tq:(2 * hp + 2) * tq]], axis=0)
        o_ref[:, hp * 2 * HEAD_DIM:(hp + 1) * 2 * HEAD_DIM] = pair.T.astype(o_ref.dtype)


def _nsa_attn(qn, kc, vc, ks, vs, kw, vw, z, pat, bw, *, batch, seq, tq, ncp):
    nt = seq // tq
    nb = seq // SEL_BLOCK
    n_cmp = (seq - CMP_BLOCK) // CMP_STRIDE + 1
    ci = np.arange(ncp)[None, :] * CMP_STRIDE
    sj = np.arange(nb)[:, None] * SEL_BLOCK
    msel = ((ci <= sj + SEL_BLOCK - 1) & (ci + CMP_BLOCK - 1 >= sj) & (np.arange(ncp)[None, :] < n_cmp))
    msel_j = jnp.asarray(msel.astype(np.float32), BF16)
    assert nb <= LANE and seq >= FAR_KEYS and FAR_KEYS % tq == 0 and WINDOW // tq >= 2
    ks_spec = pl.BlockSpec((1, 1, seq, ks.shape[3]), lambda b, g, i: (b, g, 0, 0))
    kw_spec = pl.BlockSpec((1, 1, seq, HEAD_DIM), lambda b, g, i: (b, g, 0, 0))
    v_spec = pl.BlockSpec((1, 1, vs.shape[2], seq), lambda b, g, i: (b, g, 0, 0))
    kc_spec = pl.BlockSpec((1, 1, ncp, HEAD_DIM), lambda b, g, i: (b, g, 0, 0))
    vc_spec = pl.BlockSpec((1, 1, HEAD_DIM, ncp), lambda b, g, i: (b, g, 0, 0))
    width = N_HEADS * HEAD_DIM
    return pl.pallas_call(
        functools.partial(_nsa_attn_kernel, tq=tq, ncp=ncp, nb=nb),
        out_shape=jax.ShapeDtypeStruct((batch * seq, width), BF16),
        grid=(batch, N_KV, nt),
        in_specs=[
            pl.BlockSpec((1, 1, 1, HEAD_DIM, N_HG * tq), lambda b, g, i: (b, g, i, 0, 0)),
            kc_spec, vc_spec, ks_spec, v_spec, kw_spec, v_spec,
            pl.BlockSpec((tq, LANE), lambda b, g, i: (b * nt + i, C_GATE // LANE)),
            pl.BlockSpec((N_HG, 2 * ncp, tq), lambda b, g, i: (g, 0, 0)),
            pl.BlockSpec((N_HG, N_TABLES, tq, tq), lambda b, g, i: (g, 0, 0, 0)),
            pl.BlockSpec(msel_j.shape, lambda b, g, i: (0, 0)),
        ],
        out_specs=pl.BlockSpec((tq, N_HG * HEAD_DIM), lambda b, g, i: (b * nt + i, g)),
        scratch_shapes=[pltpu.VMEM((nt, 8, tq), F32)],
        compiler_params=_cparams(("arbitrary", "arbitrary", "arbitrary")),
        name="nsa_attn",
    )(qn, kc, vc, ks, vs, kw, vw, z, pat, bw, msel_j)


def _merge_cross_kernel(x_ref, yr_ref, yn_ref, zm_ref, wr_ref, wn_ref, wm_ref,
                        g_ref, wq_ref, qg_ref, k_ref, v_ref, wo_ref, o_ref):
    d = x_ref.shape[1]
    dh = d // CA_HEADS
    zm = zm_ref[...].astype(F32)
    br = _dot(yr_ref[...], wr_ref[...])
    bn = _dot(yn_ref[...], wn_ref[...])
    merged = _sigmoid(zm[:, 0:d]) * br + _sigmoid(zm[:, d:2 * d]) * bn
    h1 = x_ref[...] + _dot(merged.astype(BF16), wm_ref[...])

    xn = (_rms(h1, NORM_EPS) * g_ref[...]).astype(BF16)
    qf = _dot(xn, wq_ref[...])
    k = k_ref[0]
    v = v_ref[0]
    outs = []
    for h in range(CA_HEADS):
        sl = slice(h * dh, (h + 1) * dh)
        qh = (_rms(qf[:, sl], QK_EPS) * qg_ref[...] * (dh ** -0.5)).astype(BF16)
        s = _dot_nt(qh, k[:, sl])
        s = s - jnp.max(s, axis=-1, keepdims=True)
        e = jnp.exp(s)
        p = e / jnp.sum(e, axis=-1, keepdims=True)
        outs.append(_dot(p.astype(BF16), v[:, sl]))
    o = jnp.concatenate(outs, axis=1).astype(BF16)
    o_ref[...] = h1 + _dot(o, wo_ref[...])


def _merge_cross(x, yr, yn, z, wr, wn, wm, g, wq, qg, kn, vv, wo, *, batch, seq, tm):
    m, d = x.shape
    nt = seq // tm
    nm = kn.shape[1]
    row = lambda c: pl.BlockSpec((tm, c), lambda b, i: (b * nt + i, 0))
    full = lambda a: pl.BlockSpec(a.shape, lambda b, i: (0, 0))
    kv_spec = pl.BlockSpec((1, nm, d), lambda b, i: (b, 0, 0))
    return pl.pallas_call(
        _merge_cross_kernel,
        out_shape=jax.ShapeDtypeStruct((m, d), F32),
        grid=(batch, nt),
        in_specs=[row(d), row(d), row(d),
                  pl.BlockSpec((tm, 2 * d), lambda b, i: (b * nt + i, C_M // (2 * d))),
                  full(wr), full(wn), full(wm), full(g), full(wq), full(qg), kv_spec, kv_spec, full(wo)],
        out_specs=row(d),
        compiler_params=_cparams(("arbitrary", "arbitrary")),
        name="merge_cross",
    )(x, yr, yn, z, wr, wn, wm, g, wq, qg, kn, vv, wo)


def _ca_kv_kernel(mem_ref, g_ref, w_ref, kg_ref, k_ref, v_ref):
    d = mem_ref.shape[2]
    dh = d // CA_HEADS
    mn = (_rms(mem_ref[0], NORM_EPS) * g_ref[...]).astype(BF16)
    kv = _dot(mn, w_ref[...])
    ks = [(_rms(kv[:, h * dh:(h + 1) * dh], QK_EPS) * kg_ref[...]) for h in range(CA_HEADS)]
    k_ref[0] = jnp.concatenate(ks, axis=1).astype(k_ref.dtype)
    v_ref[0] = kv[:, d:2 * d].astype(v_ref.dtype)


def _ca_kv(mem, g, wkv, kg):
    batch, nm, d = mem.shape
    o_shape = jax.ShapeDtypeStruct((batch, nm, d), BF16)
    o_spec = pl.BlockSpec((1, nm, d), lambda b: (b, 0, 0))
    full = lambda a: pl.BlockSpec(a.shape, lambda b: (0, 0))
    return pl.pallas_call(
        _ca_kv_kernel,
        out_shape=(o_shape, o_shape),
        grid=(batch,),
        in_specs=[o_spec, full(g), full(wkv), full(kg)],
        out_specs=(o_spec, o_spec),
        compiler_params=_cparams(("arbitrary",)),
        name="ca_kv",
    )(mem, g, wkv, kg)


def _ffn_kernel(x_ref, g_ref, wup_ref, cw_ref, cb_ref, wd_ref, o_ref, act_ref, carry_ref,
                *, tiles_per_seq, tn):
    i = pl.program_id(0)
    tm = x_ref.shape[0]
    dff = wd_ref.shape[0]

    @pl.when(i % tiles_per_seq == 0)
    def _():
        carry_ref[...] = jnp.zeros_like(carry_ref)

    x = x_ref[...]
    xn = (_rms(x, NORM_EPS) * g_ref[...]).astype(BF16)
    rowi = lax.broadcasted_iota(jnp.int32, (tm, tn), 0)
    for j in range(dff // tn):
        cs = slice(j * tn, (j + 1) * tn)
        a = _dot(xn, wup_ref[:, cs])
        b = _dot(xn, wup_ref[:, dff + j * tn:dff + (j + 1) * tn])
        car = carry_ref[:, cs]
        p1 = jnp.where(rowi == 0, car[7:8, :], pltpu.roll(a, 1, axis=0))
        p2 = jnp.where(rowi == 0, car[6:7, :], jnp.where(rowi == 1, car[7:8, :], pltpu.roll(a, 2, axis=0)))
        carry_ref[:, cs] = a[tm - 8:tm, :]
        conv = cw_ref[0:1, cs] * p2 + cw_ref[1:2, cs] * p1 + cw_ref[2:3, cs] * a + cb_ref[:, cs]
        act_ref[:, cs] = (conv * _sigmoid(conv) * b).astype(BF16)
    o_ref[...] = x + _dot(act_ref[...], wd_ref[...])


def _ffn(h2, g, wup, cw, cb, wd, *, seq, tm, tn):
    m, d = h2.shape
    dff = wd.shape[0]
    cw8 = jnp.zeros((8, dff), F32).at[0:CONV_W, :].set(cw)
    full = lambda a: pl.BlockSpec(a.shape, lambda i: (0, 0))
    return pl.pallas_call(
        functools.partial(_ffn_kernel, tiles_per_seq=seq // tm, tn=tn),
        out_shape=jax.ShapeDtypeStruct((m, d), F32),
        grid=(m // tm,),
        in_specs=[pl.BlockSpec((tm, d), lambda i: (i, 0)), full(g), full(wup), full(cw8),
                  pl.BlockSpec((1, dff), lambda i: (0, 0)), full(wd)],
        out_specs=pl.BlockSpec((tm, d), lambda i: (i, 0)),
        scratch_shapes=[pltpu.VMEM((tm, dff), BF16), pltpu.VMEM((8, dff), F32)],
        compiler_params=_cparams(("arbitrary",)),
        name="ffn",
    )(h2, g, wup, cw8, cb.reshape(1, dff), wd)


def _pack_perm(width):
    o_zw = 3 * width
    o_za = o_zw + R_LORA_W
    o_zg = o_za + R_LORA_A
    o_q = o_zg + R_LORA_G
    nkv = N_KV * HEAD_DIM
    o_kc = o_q + N_HEADS * HEAD_DIM
    o_gate = o_kc + 6 * nkv
    o_m = o_gate + 3 * N_HEADS
    perm = np.full((Z_COLS,), -1, np.int64)
    perm[C_RKV:C_RKV + 3 * width] = np.arange(3 * width)
    perm[C_Q:C_Q + N_HEADS * HEAD_DIM] = o_q + np.arange(N_HEADS * HEAD_DIM)
    perm[C_KC:C_KC + 6 * nkv] = o_kc + np.arange(6 * nkv)
    perm[C_ZG:C_ZG + R_LORA_G] = o_zg + np.arange(R_LORA_G)
    perm[C_ZWA:C_ZWA + R_LORA_W + R_LORA_A] = o_zw + np.arange(R_LORA_W + R_LORA_A)
    for g in range(N_KV):
        for c in range(3):
            for h in range(N_HG):
                perm[C_GATE + 16 * g + 4 * c + h] = o_gate + (g * N_HG + h) * 3 + c
    perm[C_M:C_M + 2 * width] = o_m + np.arange(2 * width)
    return perm, o_m + 2 * width


def _pad_rows(w, rows, offset=0):
    out = jnp.zeros((rows, w.shape[1]), w.dtype)
    return out.at[offset:offset + w.shape[0]].set(w)


def kernel(x, mem, rel_bias, norm_mix, w_in, rwkv_mu, rwkv_w0, rwkv_w2, rwkv_a0, rwkv_a2, rwkv_g2,
           rwkv_kk, rwkv_ka, rwkv_rk, rwkv_lnx_w, rwkv_lnx_b, nsa_q_gain, nsa_k_gain, cmp_pe_k, cmp_pe_v,
           cmp_w1_k, cmp_w2_k, cmp_w1_v, cmp_w2_v, w_branch_rwkv, w_branch_nsa, w_mix_out,
           norm_cross, norm_mem, ca_wq, ca_wkv, ca_q_gain, ca_k_gain, ca_wo,
           norm_ffn, ffn_up, ffn_conv, ffn_conv_b, ffn_down):
    batch, seq, d = x.shape
    depth = w_in.shape[0]
    width = d
    assert d == N_HEADS * HEAD_DIM and 2 * (Z_COLS - C_M) == 4 * d
    perm, in_cols = _pack_perm(width)
    assert w_in.shape[2] == in_cols
    perm_j = jnp.asarray(np.maximum(perm, 0), jnp.int32)
    keep = jnp.asarray(perm >= 0)

    tq = 256
    assert seq % 256 == 0 and tq > REL_MAX_DIST - 1 and WINDOW % tq == 0
    ncp = -(-(seq // CMP_STRIDE) // LANE) * LANE
    n_cmp = (seq - CMP_BLOCK) // CMP_STRIDE + 1
    tm = min(1024, seq)
    row = lambda v: v.reshape(1, -1).astype(F32)
    nkv = N_KV * HEAD_DIM

    h = x.reshape(batch * seq, d)
    for l in range(depth):
        w_in_p = jnp.where(keep[None, :], jnp.take(w_in[l], perm_j, axis=1), 0.0).astype(BF16)
        mu = rwkv_mu[l]
        mu_r = row(mu[0:3 * width])
        mu_wa = row(mu[3 * width:3 * width + R_LORA_W + R_LORA_A])
        mu_g = row(jnp.zeros((256,), F32).at[0:R_LORA_G].set(mu[3 * width + R_LORA_W + R_LORA_A:]))
        w2p = _pad_rows(rwkv_w2[l], LANE, 0).astype(BF16)
        a2p = _pad_rows(rwkv_a2[l], LANE, R_LORA_W).astype(BF16)
        g2p = _pad_rows(rwkv_g2[l], 256, 0).astype(BF16)

        z = _norm_matmul(h, row(norm_mix[l]), w_in_p, tm=tm, tn=1024, out_dtype=BF16, name="in_proj")

        y_r = _rwkv(z, mu_r, mu_g, mu_wa, row(rwkv_w0[l]), w2p, row(rwkv_a0[l]), a2p, g2p,
                    row(rwkv_kk[l]), row(rwkv_ka[l]), row(rwkv_rk[l]), row(rwkv_lnx_w[l]),
                    row(rwkv_lnx_b[l]), batch=batch, seq=seq, width=width)

        qn, ks, vs, kw, vw = _nsa_prep(
            z, row(jnp.tile(nsa_q_gain[l], N_HEADS)), row(jnp.tile(nsa_k_gain[l, 1], N_KV)),
            row(jnp.tile(nsa_k_gain[l, 2], N_KV)), batch=batch, seq=seq, tq=tq)
        z3 = z.reshape(batch, seq // CMP_STRIDE, CMP_STRIDE, Z_COLS)

        def groups(c0):
            t = z3[:, :, :, c0:c0 + nkv].reshape(batch, seq // CMP_STRIDE, CMP_STRIDE, N_KV, HEAD_DIM)
            return t.transpose(0, 3, 1, 2, 4).reshape(batch, N_KV, seq // CMP_STRIDE, CMP_STRIDE * HEAD_DIM)

        kc, vc = _nsa_cmp(groups(C_KC), groups(C_KC + nkv), cmp_pe_k[l].reshape(1, -1), cmp_pe_v[l].reshape(1, -1),
                          cmp_w1_k[l].astype(BF16), cmp_w2_k[l].astype(BF16), cmp_w1_v[l].astype(BF16),
                          cmp_w2_v[l].astype(BF16), row(nsa_k_gain[l, 0]), ncp=ncp, n_cmp=n_cmp)
        bw, pat = _nsa_bias(rel_bias.astype(F32), tq=tq, ncp=ncp)
        y_n = _nsa_attn(qn, kc, vc, ks, vs, kw, vw, z, pat, bw,
                        batch=batch, seq=seq, tq=tq, ncp=ncp)

        kn, vv = _ca_kv(mem, row(norm_mem[l]), ca_wkv[l].astype(BF16), row(ca_k_gain[l]))
        h2 = _merge_cross(h, y_r, y_n, z, w_branch_rwkv[l].astype(BF16), w_branch_nsa[l].astype(BF16),
                          w_mix_out[l].astype(BF16), row(norm_cross[l]), ca_wq[l].astype(BF16),
                          row(ca_q_gain[l]), kn, vv, ca_wo[l].astype(BF16),
                          batch=batch, seq=seq, tm=min(512, seq))

        h = _ffn(h2, row(norm_ffn[l]), ffn_up[l].astype(BF16), ffn_conv[l], ffn_conv_b[l],
                 ffn_down[l].astype(BF16), seq=seq, tm=min(512, seq), tn=256)
    return h.reshape(batch, seq, d)
```

```python
import functools
import math

import jax
import jax.numpy as jnp
import numpy as np
from jax import lax
from jax.experimental import pallas as pl
from jax.experimental.pallas import tpu as pltpu

F32 = jnp.float32
BF16 = jnp.bfloat16
NEG = -1e30
LOG2E = 1.4426950408889634
FAR_KEYS = 512
T_DIAG, T_NEAR, T_NONE, T_FAR, T_EDGE, T_DIAG_REL, T_NEAR_REL = range(7)
N_TABLES = 7
V_ROWS = 80

R_HEAD = 64
R_LORA_W = 64
R_LORA_A = 64
R_LORA_G = 160
LNX_EPS = 64e-5
N_HEADS = 16
N_KV = 4
N_HG = N_HEADS // N_KV
HEAD_DIM = 64
CMP_BLOCK = 32
CMP_STRIDE = 16
CMP_HIDDEN = 256
SEL_BLOCK = 64
SEL_TOPK = 16
WINDOW = 512
FORCE_SCORE = 1e4
REL_BUCKETS = 32
REL_MAX_DIST = 128
CA_HEADS = 4
CONV_W = 3
NORM_EPS = 1e-6
QK_EPS = 1e-6

LANE = 128
VMEM_LIMIT = 56 * 1024 * 1024

C_RKV = 0
C_Q = 3072
C_KC = 4096
C_ZG = 5632
C_ZWA = 5888
C_GATE = 6016
C_M = 6144
Z_COLS = 8192


def _dot(a, b):
    return jnp.dot(a, b, preferred_element_type=F32)


def _dot_nt(a, b):
    return lax.dot_general(a, b, (((1,), (1,)), ((), ())), preferred_element_type=F32)


def _split_dot(x, e, parts, mode="xe"):
    acc = None
    rem = x
    for i in range(parts):
        hi = rem.astype(BF16)
        t = _dot(hi, e) if mode == "xe" else (_dot(e, hi) if mode == "ex" else _dot_nt(e, hi))
        acc = t if acc is None else acc + t
        if i + 1 < parts:
            rem = rem - hi.astype(F32)
    return acc


def _rms(x, eps):
    return x * lax.rsqrt(jnp.mean(x * x, axis=-1, keepdims=True) + eps)


def _sigmoid(x):
    return 1.0 / (1.0 + jnp.exp(-x))


def _cparams(sem):
    return pltpu.CompilerParams(dimension_semantics=sem, vmem_limit_bytes=VMEM_LIMIT)


def _norm_matmul_kernel(x_ref, g_ref, w_ref, o_ref, xn_ref):
    @pl.when(pl.program_id(1) == 0)
    def _():
        x = x_ref[...]
        xn_ref[...] = (_rms(x, NORM_EPS) * g_ref[...]).astype(BF16)

    o_ref[...] = _dot(xn_ref[...], w_ref[...]).astype(o_ref.dtype)


def _norm_matmul(x, g, w, *, tm, tn, out_dtype, name):
    m, k = x.shape
    n = w.shape[1]
    return pl.pallas_call(
        _norm_matmul_kernel,
        out_shape=jax.ShapeDtypeStruct((m, n), out_dtype),
        grid=(m // tm, n // tn),
        in_specs=[
            pl.BlockSpec((tm, k), lambda i, j: (i, 0)),
            pl.BlockSpec((1, k), lambda i, j: (0, 0)),
            pl.BlockSpec((k, tn), lambda i, j: (0, j)),
        ],
        out_specs=pl.BlockSpec((tm, tn), lambda i, j: (i, j)),
        scratch_shapes=[pltpu.VMEM((tm, k), BF16)],
        compiler_params=_cparams(("arbitrary", "arbitrary")),
        name=name,
    )(x, g, w)


def _rwkv_kernel(zr_ref, zg_ref, zwa_ref, mur_ref, mug_ref, muwa_ref, w0_ref, w2_ref, a0_ref, a2_ref,
                 g2_ref, kk_ref, ka_ref, rk_ref, lnw_ref, lnb_ref, eh_ref, eht_ref, tri_ref,
                 o_ref, pr_ref, pg_ref, pwa_ref, st_ref, *, L, width):
    i = pl.program_id(1)
    npair = width // LANE
    tb = o_ref.shape[0]

    @pl.when(i == 0)
    def _():
        pr_ref[...] = jnp.zeros_like(pr_ref)
        pg_ref[...] = jnp.zeros_like(pg_ref)
        pwa_ref[...] = jnp.zeros_like(pwa_ref)
        st_ref[...] = jnp.zeros_like(st_ref)

    def shifted8(z, prev_ref, mu):
        rolled = pltpu.roll(z, 1, axis=0)
        row0 = lax.broadcasted_iota(jnp.int32, z.shape, 0) == 0
        prev = jnp.where(row0, prev_ref[7:8, :], rolled)
        prev_ref[...] = z[tb - 8:tb, :]
        return z + (prev - z) * mu

    zs = shifted8(zr_ref[...].astype(F32), pr_ref, mur_ref[...])
    zsg = shifted8(zg_ref[...].astype(F32), pg_ref, mug_ref[...])
    zswa = shifted8(zwa_ref[...].astype(F32), pwa_ref, muwa_ref[...])

    r = zs[:, 0:width]
    k = zs[:, width:2 * width]
    v = zs[:, 2 * width:3 * width]

    w_lin = w0_ref[...] + _dot(jnp.tanh(zswa).astype(BF16), w2_ref[...])
    a_lin = a0_ref[...] + _dot(zswa.astype(BF16), a2_ref[...])
    y = -w_lin
    softplus = jnp.maximum(y, 0.0) + jnp.log(1.0 + jnp.exp(-jnp.abs(y)))
    ld = -jnp.exp(-softplus - 0.5)
    a = _sigmoid(a_lin)
    g = _dot(_sigmoid(zsg).astype(BF16), g2_ref[...])

    eh = eh_ref[...]
    eht = eht_ref[...]

    def sum_heads(t):
        return _split_dot(t, eh, 2)

    def bcast_heads(t):
        return _split_dot(t, eht, 2)

    kkr = k * kk_ref[...]
    nrm = jnp.maximum(jnp.sqrt(sum_heads(kkr * kkr)), 1e-12)
    kkn = kkr * bcast_heads(1.0 / nrm)
    k2 = k * (1.0 + (a - 1.0) * ka_ref[...])
    av = -kkn
    bv = kkn * a
    bonus = bcast_heads(sum_heads(r * k2 * rk_ref[...])) * v

    nch = tb // L
    lg = _split_dot(ld, tri_ref[...], 3, mode="ex")
    lasts = [lg[(c + 1) * L - 1:(c + 1) * L, :] for c in range(nch)]
    lg_last = jnp.concatenate([jnp.broadcast_to(t, (L, width)) for t in lasts], axis=0)
    eg = jnp.exp(lg)
    eng = jnp.exp(-lg)
    egl = jnp.exp(lg_last - lg)
    rt = r * eg
    kt = k2 * eng
    bt = bv * eng
    at = av * jnp.exp(lg - ld)
    kgl = k2 * egl
    bgl = bv * egl
    gl = [jnp.exp(t) for t in lasts]

    lane = lax.broadcasted_iota(jnp.int32, (L, LANE), 1)
    lo = lane < R_HEAD

    def stack(t):
        return jnp.concatenate([jnp.where(lo, t, 0.0), jnp.where(lo, 0.0, t)], axis=0)

    row = lax.broadcasted_iota(jnp.int32, (2 * L, 2 * L), 0)
    col = lax.broadcasted_iota(jnp.int32, (2 * L, 2 * L), 1)
    strict = row > col
    incl = row >= col
    eye = (row == col).astype(F32)
    nsq = int(math.log2(L)) - 1

    pairs = range(npair)
    chains = [(c, p) for c in range(nch) for p in pairs]
    sub = lambda t, c, p: t[c * L:(c + 1) * L, p * LANE:(p + 1) * LANE]
    bf = lambda t: t.astype(BF16)
    ar = {cp: bf(jnp.concatenate([stack(sub(at, *cp)), stack(sub(rt, *cp))], axis=0)) for cp in chains}
    bk = {cp: bf(jnp.concatenate([stack(sub(bt, *cp)), stack(sub(kt, *cp))], axis=0)) for cp in chains}
    v_s = {cp: bf(stack(sub(v, *cp))) for cp in chains}
    gm = {cp: _dot_nt(ar[cp], bk[cp]) for cp in chains}
    a_ab = {cp: jnp.where(strict, gm[cp][0:2 * L, 0:2 * L], 0.0) for cp in chains}
    a_ak = {cp: bf(jnp.where(strict, gm[cp][0:2 * L, 2 * L:4 * L], 0.0)) for cp in chains}
    a_rb = {cp: bf(jnp.where(incl, gm[cp][2 * L:4 * L, 0:2 * L], 0.0)) for cp in chains}
    a_rk = {cp: bf(jnp.where(incl, gm[cp][2 * L:4 * L, 2 * L:4 * L], 0.0)) for cp in chains}
    tinv = {cp: eye + a_ab[cp] for cp in chains}
    pw = a_ab
    for _ in range(nsq):
        pwb = {cp: bf(pw[cp]) for cp in chains}
        pw = {cp: _dot(pwb[cp], pwb[cp]) for cp in chains}
        tinv = {cp: tinv[cp] + _dot(bf(pw[cp]), bf(tinv[cp])) for cp in chains}
    tinv = {cp: bf(tinv[cp]) for cp in chains}
    bkgt = {cp: bf(jnp.concatenate([stack(sub(bgl, *cp)), stack(sub(kgl, *cp))], axis=0).T) for cp in chains}
    glcol = {(c, p): jnp.sum(eye * gl[c][:, p * LANE:(p + 1) * LANE], axis=1, keepdims=True)
             for (c, p) in chains}
    h = [st_ref[p] for p in pairs]
    ys = []
    for c in range(nch):
        hb = [bf(h[p]) for p in pairs]
        wmat = [_dot(jnp.concatenate([ar[c, p][0:2 * L], a_ak[c, p]], axis=1),
                     jnp.concatenate([hb[p], v_s[c, p]], axis=0)) for p in pairs]
        ub = [bf(_dot(tinv[c, p], bf(wmat[p]))) for p in pairs]
        yy = [_dot(jnp.concatenate([ar[c, p][2 * L:4 * L], a_rb[c, p], a_rk[c, p]], axis=1),
                   jnp.concatenate([hb[p], ub[p], v_s[c, p]], axis=0)) for p in pairs]
        h = [h[p] * glcol[c, p] + _dot(bkgt[c, p], jnp.concatenate([ub[p], v_s[c, p]], axis=0))
             for p in pairs]
        ys.append(jnp.concatenate([yy[p][0:L] + yy[p][L:2 * L] for p in pairs], axis=1))
    for p in pairs:
        st_ref[p] = h[p]

    yv = jnp.concatenate(ys, axis=0)
    inv_n = 1.0 / R_HEAD
    mean = bcast_heads(sum_heads(yv) * inv_n)
    yc = yv - mean
    var = bcast_heads(sum_heads(yc * yc) * inv_n)
    yn = yc * lax.rsqrt(var + LNX_EPS) * lnw_ref[...] + lnb_ref[...]
    o_ref[...] = ((yn + bonus) * g).astype(o_ref.dtype)


def _rwkv(z, mu_r, mu_g, mu_wa, w0, w2p, a0, a2p, g2p, kk, ka, rk, lnw, lnb, *, batch, seq, width):
    L = 64
    tb = 4 * L
    nt = seq // tb
    nheads = width // R_HEAD
    eh = np.zeros((width, LANE), np.float32)
    eh[np.arange(width), np.arange(width) // R_HEAD] = 1.0
    ti = np.arange(tb)
    tri = ((ti[:, None] >= ti[None, :]) & (ti[:, None] // L == ti[None, :] // L)).astype(np.float32)
    vec = lambda c: pl.BlockSpec((1, c), lambda b, i: (0, 0))
    full = lambda a: pl.BlockSpec(a.shape, lambda b, i: (0,) * a.ndim)
    eh_j = jnp.asarray(eh, BF16)
    eht_j = jnp.asarray(eh.T, BF16)
    tri_j = jnp.asarray(tri, BF16)
    return pl.pallas_call(
        functools.partial(_rwkv_kernel, L=L, width=width),
        out_shape=jax.ShapeDtypeStruct((batch * seq, width), BF16),
        grid=(batch, nt),
        in_specs=[
            pl.BlockSpec((tb, 3 * width), lambda b, i: (b * nt + i, C_RKV // (3 * width))),
            pl.BlockSpec((tb, 256), lambda b, i: (b * nt + i, C_ZG // 256)),
            pl.BlockSpec((tb, LANE), lambda b, i: (b * nt + i, C_ZWA // LANE)),
            vec(3 * width), vec(256), vec(LANE),
            vec(width), full(w2p), vec(width), full(a2p), full(g2p),
            vec(width), vec(width), vec(width), vec(width), vec(width),
            full(eh_j), full(eht_j), full(tri_j),
        ],
        out_specs=pl.BlockSpec((tb, width), lambda b, i: (b * nt + i, 0)),
        scratch_shapes=[
            pltpu.VMEM((8, 3 * width), F32),
            pltpu.VMEM((8, 256), F32),
            pltpu.VMEM((8, LANE), F32),
            pltpu.VMEM((nheads // 2, 2 * R_HEAD, 2 * R_HEAD), F32),
        ],
        compiler_params=_cparams(("arbitrary", "arbitrary")),
        name="rwkv",
    )(z, z, z, mu_r, mu_g, mu_wa, w0, w2p, a0, a2p, g2p, kk, ka, rk, lnw, lnb, eh_j, eht_j, tri_j)


def _nsa_prep_kernel(zq_ref, zs_ref, zw_ref, qg_ref, kgs_ref, kgw_ref, ehq_ref, ehqt_ref,
                     q_ref, ks_ref, vs_ref, kw_ref, vw_ref):
    inv_d = 1.0 / HEAD_DIM

    def head_norm(x, gain, e, et):
        ss = _split_dot(x * x, e, 2)
        inv = lax.rsqrt(ss * inv_d + QK_EPS)
        return x * _split_dot(inv, et, 2) * gain

    nk = N_KV * HEAD_DIM
    ehq = ehq_ref[...]
    ehqt = ehqt_ref[...]
    q = head_norm(zq_ref[...].astype(F32), qg_ref[...], ehq, ehqt) * (HEAD_DIM ** -0.5 * LOG2E)
    qt = q.T
    for gq in range(N_KV):
        rows = [qt[(gq * N_HG + h) * HEAD_DIM:(gq * N_HG + h + 1) * HEAD_DIM, :] for h in range(N_HG)]
        q_ref[0, gq, 0] = jnp.concatenate(rows, axis=1).astype(q_ref.dtype)
    zs = zs_ref[...].astype(F32)
    zw = zw_ref[...].astype(F32)
    ehk = ehq[0:nk, :]
    ehkt = ehqt[:, 0:nk]
    ksn = head_norm(zs[:, 0:nk], kgs_ref[...], ehk, ehkt)
    kwn = head_norm(zw[:, 0:nk], kgw_ref[...], ehk, ehkt)
    vst = zs[:, nk:2 * nk].T
    vwt = zw[:, nk:2 * nk].T
    tb = zs.shape[0]
    vextra = jnp.where(lax.broadcasted_iota(jnp.int32, (V_ROWS - HEAD_DIM, tb), 0) == 0, 1.0, 0.0)
    for gq in range(N_KV):
        sl = slice(gq * HEAD_DIM, (gq + 1) * HEAD_DIM)
        ks_ref[0, gq] = ksn[:, sl].astype(ks_ref.dtype)
        vs_ref[0, gq] = jnp.concatenate([vst[sl, :], vextra], axis=0).astype(vs_ref.dtype)
        kw_ref[0, gq] = kwn[:, sl].astype(kw_ref.dtype)
        vw_ref[0, gq] = jnp.concatenate([vwt[sl, :], vextra], axis=0).astype(vw_ref.dtype)


def _nsa_prep(z, q_gain_t, kgs_t, kgw_t, *, batch, seq, tq):
    tb = tq
    nt = seq // tb
    nq = N_HEADS * HEAD_DIM
    nk = N_KV * HEAD_DIM
    eh = np.zeros((nq, LANE), np.float32)
    eh[np.arange(nq), np.arange(nq) // HEAD_DIM] = 1.0
    eh_j = jnp.asarray(eh, BF16)
    eht_j = jnp.asarray(eh.T, BF16)
    k_shape = jax.ShapeDtypeStruct((batch, N_KV, seq, HEAD_DIM), BF16)
    k_spec = pl.BlockSpec((1, N_KV, tb, HEAD_DIM), lambda b, i: (b, 0, i, 0))
    v_shape = jax.ShapeDtypeStruct((batch, N_KV, V_ROWS, seq), BF16)
    v_spec = pl.BlockSpec((1, N_KV, V_ROWS, tb), lambda b, i: (b, 0, 0, i))
    return pl.pallas_call(
        _nsa_prep_kernel,
        out_shape=(jax.ShapeDtypeStruct((batch, N_KV, nt, HEAD_DIM, N_HG * tq), BF16),
                   k_shape, v_shape, k_shape, v_shape),
        grid=(batch, nt),
        in_specs=[
            pl.BlockSpec((tb, nq), lambda b, i: (b * nt + i, C_Q // nq)),
            pl.BlockSpec((tb, 2 * nk), lambda b, i: (b * nt + i, C_KC // (2 * nk) + 1)),
            pl.BlockSpec((tb, 2 * nk), lambda b, i: (b * nt + i, C_KC // (2 * nk) + 2)),
            pl.BlockSpec((1, nq), lambda b, i: (0, 0)),
            pl.BlockSpec((1, nk), lambda b, i: (0, 0)),
            pl.BlockSpec((1, nk), lambda b, i: (0, 0)),
            pl.BlockSpec(eh_j.shape, lambda b, i: (0, 0)),
            pl.BlockSpec(eht_j.shape, lambda b, i: (0, 0)),
        ],
        out_specs=(pl.BlockSpec((1, N_KV, 1, HEAD_DIM, N_HG * tq), lambda b, i: (b, 0, i, 0, 0)),
                   k_spec, v_spec, k_spec, v_spec),
        compiler_params=_cparams(("arbitrary", "arbitrary")),
        name="nsa_prep",
    )(z, z, z, q_gain_t, kgs_t, kgw_t, eh_j, eht_j)


def _nsa_cmp_kernel(ak_ref, av_ref, pek_ref, pev_ref, w1k_ref, w2k_ref, w1v_ref, w2v_ref, kg_ref,
                    kc_ref, vc_ref, *, n_cmp):
    ncp = kc_ref.shape[2]
    half = w1k_ref.shape[0] // 2

    def compress(a_ref, pe_ref, w1_ref, w2_ref):
        rows = a_ref.shape[2]
        a = a_ref[0, 0].astype(BF16)
        p1 = _dot(a, w1_ref[0:half, :])
        p2 = _dot(a, w1_ref[half:2 * half, :])
        pe8 = jnp.broadcast_to(pe_ref[...], (8, pe_ref.shape[1])).astype(BF16)
        pe_bias = _dot(pe8, w1_ref[...])[0:1, :]
        hpre = p1 + pltpu.roll(p2, rows - 1, axis=0) + pe_bias
        hid = 0.5 * hpre * (1.0 + jnp.tanh(math.sqrt(2.0 / math.pi) * (hpre + 0.044715 * hpre * hpre * hpre)))
        out = _dot(hid.astype(BF16), w2_ref[...])
        if rows < ncp:
            out = jnp.concatenate([out, jnp.zeros((ncp - rows, out.shape[1]), F32)], axis=0)
        return out

    valid = lax.broadcasted_iota(jnp.int32, (ncp, HEAD_DIM), 0) < n_cmp
    kc = compress(ak_ref, pek_ref, w1k_ref, w2k_ref)
    kc = _rms(kc, QK_EPS) * kg_ref[...]
    kc_ref[0, 0] = jnp.where(valid, kc, 0.0).astype(kc_ref.dtype)
    vc = jnp.where(valid, compress(av_ref, pev_ref, w1v_ref, w2v_ref), 0.0)
    vct = jnp.concatenate([vc, jnp.zeros((ncp, LANE - HEAD_DIM), F32)], axis=1).T
    vc_ref[0, 0] = vct[0:HEAD_DIM, :].astype(vc_ref.dtype)


def _nsa_cmp(ak, av, pek, pev, w1k, w2k, w1v, w2v, kg, *, ncp, n_cmp):
    batch, ng, rows, wid = ak.shape
    a_spec = pl.BlockSpec((1, 1, rows, wid), lambda b, g: (b, g, 0, 0))
    full = lambda a: pl.BlockSpec(a.shape, lambda b, g: (0,) * a.ndim)
    k_shape = jax.ShapeDtypeStruct((batch, ng, ncp, HEAD_DIM), BF16)
    k_spec = pl.BlockSpec((1, 1, ncp, HEAD_DIM), lambda b, g: (b, g, 0, 0))
    v_shape = jax.ShapeDtypeStruct((batch, ng, HEAD_DIM, ncp), BF16)
    v_spec = pl.BlockSpec((1, 1, HEAD_DIM, ncp), lambda b, g: (b, g, 0, 0))
    return pl.pallas_call(
        functools.partial(_nsa_cmp_kernel, n_cmp=n_cmp),
        out_shape=(k_shape, v_shape),
        grid=(batch, ng),
        in_specs=[a_spec, a_spec, full(pek), full(pev), full(w1k), full(w2k), full(w1v), full(w2v), full(kg)],
        out_specs=(k_spec, v_spec),
        compiler_params=_cparams(("arbitrary", "arbitrary")),
        name="nsa_cmp",
    )(ak, av, pek, pev, w1k, w2k, w1v, w2v, kg)


def _bucket_thresholds():
    exact = REL_BUCKETS // 2
    n = np.arange(0, REL_MAX_DIST + 1)
    nf = np.maximum(n, 1).astype(np.float64)
    large = exact + (np.log(nf / exact) / math.log(REL_MAX_DIST / exact) * (REL_BUCKETS - exact)).astype(np.int64)
    large = np.minimum(large, REL_BUCKETS - 1)
    bucket = np.where(n < exact, n, large)
    assert bucket[-1] == REL_BUCKETS - 1
    return [int(np.argmax(bucket >= b)) for b in range(REL_BUCKETS)]


def _nsa_bias_kernel(rel_ref, bw_ref, pat_ref, *, tq, ncp):
    h = pl.program_id(0)
    thr = _bucket_thresholds()

    def bias_of(d):
        val = jnp.full(d.shape, rel_ref[0, h], F32)
        for b in range(1, REL_BUCKETS):
            val = jnp.where(d >= thr[b], rel_ref[b, h], val)
        return jnp.where(d < 0, NEG, val * LOG2E)

    kk = lax.broadcasted_iota(jnp.int32, (tq, tq), 0)
    qq = lax.broadcasted_iota(jnp.int32, (tq, tq), 1)
    far = rel_ref[REL_BUCKETS - 1, h] * LOG2E
    diag = bias_of(qq - kk)
    near = bias_of(qq - kk + tq)
    bw_ref[0, T_DIAG] = diag
    bw_ref[0, T_NEAR] = near
    bw_ref[0, T_NONE] = jnp.full((tq, tq), NEG, F32)
    bw_ref[0, T_FAR] = jnp.full((tq, tq), far, F32)
    bw_ref[0, T_EDGE] = jnp.where(kk > qq, far, NEG)
    bw_ref[0, T_DIAG_REL] = jnp.where(qq >= kk, diag - far, NEG)
    bw_ref[0, T_NEAR_REL] = near - far
    cc = lax.broadcasted_iota(jnp.int32, (2 * ncp, tq), 0)
    qc = lax.broadcasted_iota(jnp.int32, (2 * ncp, tq), 1)
    pat_ref[0] = bias_of(qc - CMP_STRIDE * (cc - ncp) - (CMP_BLOCK - 1))


def _nsa_bias(rel, *, tq, ncp):
    return pl.pallas_call(
        functools.partial(_nsa_bias_kernel, tq=tq, ncp=ncp),
        out_shape=(jax.ShapeDtypeStruct((N_HEADS, N_TABLES, tq, tq), F32),
                   jax.ShapeDtypeStruct((N_HEADS, 2 * ncp, tq), F32)),
        grid=(N_HEADS,),
        in_specs=[pl.BlockSpec(memory_space=pltpu.SMEM)],
        out_specs=(pl.BlockSpec((1, N_TABLES, tq, tq), lambda h: (h, 0, 0, 0)),
                   pl.BlockSpec((1, 2 * ncp, tq), lambda h: (h, 0, 0))),
        compiler_params=_cparams(("arbitrary",)),
        name="nsa_bias",
    )(rel)


def _nsa_attn_kernel(q_ref, kc_ref, vc_ref, ks_ref, vs_ref, kw_ref, vw_ref, zg_ref, pat_ref,
                     bw_ref, msel_ref, o_ref, pen_ref, sbuf_ref, *, tq, ncp, nb):
    g = pl.program_id(1)
    qi = pl.program_id(2)
    cols = N_HG * tq
    qt = q_ref[0, 0, 0]

    def heads(fn):
        return jnp.concatenate([fn(h) for h in range(N_HG)], axis=1)

    def table(which, ok=None):
        idx = which if ok is None else jnp.where(ok, which, T_NONE)
        return heads(lambda h: bw_ref[h, idx])

    vrows = vs_ref.shape[2]

    def flash(carry, s, vt):
        m, acc = carry
        if s.dtype == BF16:
            m_new = jnp.maximum(m.astype(BF16), jnp.max(s, axis=0, keepdims=True))
        else:
            m_new = jnp.maximum(m, jnp.max(s, axis=0, keepdims=True)).astype(BF16)
        p = jnp.exp2(s.astype(BF16) - m_new)
        m_new = m_new.astype(F32)
        acc = jnp.exp2(m - m_new) * acc + _dot(vt, p)
        return m_new, acc

    init = (jnp.full((1, cols), NEG, F32), jnp.zeros((vrows, cols), F32))

    def finish(carry):
        acc = carry[1]
        return acc[0:HEAD_DIM] / acc[HEAD_DIM:HEAD_DIM + 1]

    s_c = _dot(kc_ref[0, 0], qt)
    start = pl.multiple_of(ncp - qi * (tq // CMP_STRIDE), tq // CMP_STRIDE)
    bias_c = heads(lambda h: pat_ref[h, pl.ds(start, ncp), :])

    nd = WINDOW // tq
    s_parts, v_parts = [], []
    for d in range(nd, -1, -1):
        off = pl.multiple_of(jnp.maximum(qi - d, 0) * tq, tq)
        s = _dot(kw_ref[0, 0, pl.ds(off, tq), :], qt)
        if d == 0:
            s = s + table(T_DIAG)
        elif d == nd:
            s = s + table(T_EDGE, qi >= d)
        elif d == 1:
            s = s + table(T_NEAR, qi >= d)
        else:
            s = s + table(T_FAR, qi >= d)
        s_parts.append(s)
        v_parts.append(vw_ref[0, 0, :, pl.ds(off, tq)])
    o_w = finish(flash(init, jnp.concatenate(s_parts, axis=0), jnp.concatenate(v_parts, axis=1)))

    valid_c = bias_c > 0.5 * NEG
    sc = s_c + bias_c
    m_c = jnp.max(sc, axis=0, keepdims=True)
    e_c = jnp.where(valid_c, jnp.exp2(sc - m_c), 0.0)
    l_c = jnp.sum(e_c, axis=0, keepdims=True)
    p_c = e_c / jnp.where(l_c > 0.0, l_c, 1.0)
    o_c = _dot(vc_ref[0, 0], p_c.astype(BF16))

    psum = p_c[:, 0:tq]
    for h in range(1, N_HG):
        psum = psum + p_c[:, h * tq:(h + 1) * tq]
    imp_t = _split_dot(psum, msel_ref[...], 3, mode="ex")
    jj = lax.broadcasted_iota(jnp.int32, (nb, tq), 0)
    tt = qi * tq + lax.broadcasted_iota(jnp.int32, (nb, tq), 1)
    cur = tt // SEL_BLOCK
    forced = (jj == 0) | (jj == cur) | (jj == cur - 1)
    valid_b = jj * SEL_BLOCK <= tt
    imp = jnp.where(valid_b, jnp.where(forced, FORCE_SCORE, imp_t), -jnp.inf)
    groups8 = [imp[r:r + 8, :] for r in range(0, nb, 8)]
    jj8 = lax.broadcasted_iota(jnp.int32, (8, tq), 0)
    cnts = [jnp.zeros((8, tq), F32) for _ in groups8]
    for i in range(nb):
        ri = jnp.broadcast_to(imp[i:i + 1, :], (8, tq))
        for r, grp in enumerate(groups8):
            if 8 * r > i:
                beats = ri >= grp
            elif 8 * r + 7 <= i:
                beats = ri > grp
            else:
                beats = jnp.where(jj8 + 8 * r > i, jnp.where(ri >= grp, 1.0, 0.0), jnp.where(ri > grp, 1.0, 0.0)) > 0.5
            cnts[r] = cnts[r] + jnp.where(beats, 1.0, 0.0)
    cnt = jnp.concatenate(cnts, axis=0)
    sel_t = jnp.where((cnt < float(min(SEL_TOPK, nb))) & valid_b, 0.0, NEG)

    per_tile = tq // SEL_BLOCK
    for t in range(pen_ref.shape[0]):
        pen_ref[t] = jnp.concatenate([sel_t[t * per_tile:(t + 1) * per_tile, :],
                                      jnp.zeros((8 - per_tile, tq), F32)], axis=0)
    blk8 = lax.broadcasted_iota(jnp.int32, (8, tq), 0)

    def tile_pen(t, hi):
        rows = jnp.where(t * per_tile + blk8 >= hi, NEG, pen_ref[t])
        return jnp.concatenate([jnp.broadcast_to(rows[u:u + 1, :], (SEL_BLOCK, tq)) for u in range(per_tile)],
                               axis=0)

    def sel_scores(t0, ntile, hi):
        off = pl.multiple_of(t0 * tq, tq)
        s = _dot(ks_ref[0, 0, pl.ds(off, ntile * tq), :], qt)
        spen = jnp.concatenate([tile_pen(t0 + u, hi) for u in range(ntile)], axis=0)
        return s + heads(lambda h: spen)

    FAR_TILES = FAR_KEYS // tq
    nk = FAR_KEYS
    n_far = jnp.maximum(qi - 1, 0)
    nblk = (n_far + FAR_TILES - 1) // FAR_TILES

    def far_t0(j):
        return jnp.maximum(n_far - FAR_TILES * (j + 1), 0)

    def far_scores(j):
        return sel_scores(far_t0(j), FAR_TILES, (n_far - FAR_TILES * j) * per_tile)

    sbuf_ref[0] = far_scores(0).astype(BF16)
    t1 = jnp.maximum(qi - 1, 0)
    s = jnp.concatenate([sel_scores(t1, 1, qi * per_tile) + table(T_NEAR_REL),
                         sel_scores(qi, 1, (qi + 1) * per_tile) + table(T_DIAG_REL)], axis=0)
    vx = jnp.concatenate([vs_ref[0, 0, :, pl.ds(pl.multiple_of(t1 * tq, tq), tq)],
                          vs_ref[0, 0, :, pl.ds(pl.multiple_of(qi * tq, tq), tq)]], axis=1)
    carry = flash(init, s, vx)

    def far_v(j):
        return vs_ref[0, 0, :, pl.ds(pl.multiple_of(far_t0(j) * tq, tq), nk)]

    def far_body(j, carry):
        slot = j % 2
        sb = sbuf_ref[slot]
        sbuf_ref[1 - slot] = far_scores(j + 1).astype(BF16)
        return flash(carry, sb, far_v(j))

    carry = lax.fori_loop(0, jnp.maximum(nblk - 1, 0), far_body, carry)
    jl = jnp.maximum(nblk - 1, 0)
    carry = lax.cond(nblk > 0, lambda c: flash(c, sbuf_ref[jl % 2], far_v(jl)), lambda c: c, carry)
    o_s = finish(carry)

    zg = pltpu.roll(zg_ref[...].astype(F32), (LANE - 16 * g) % LANE, axis=1)
    gates = _sigmoid(zg).T
    out = (heads(lambda h: gates[h:h + 1, :]) * o_c + heads(lambda h: gates[4 + h:5 + h, :]) * o_s
           + heads(lambda h: gates[8 + h:9 + h, :]) * o_w)
    for hp in range(N_HG // 2):
        pair = jnp.concatenate([out[:, (2 * hp) * tq:(2 * hp + 1) * tq],
                                out[:, (2 * hp + 1) * tq:(2 * hp + 2) * tq]], axis=0)
        o_ref[:, hp * 2 * HEAD_DIM:(hp + 1) * 2 * HEAD_DIM] = pair.T.astype(o_ref.dtype)


def _nsa_attn(qn, kc, vc, ks, vs, kw, vw, z, pat, bw, *, batch, seq, tq, ncp):
    nt = seq // tq
    nb = seq // SEL_BLOCK
    n_cmp = (seq - CMP_BLOCK) // CMP_STRIDE + 1
    ci = np.arange(ncp)[None, :] * CMP_STRIDE
    sj = np.arange(nb)[:, None] * SEL_BLOCK
    msel = ((ci <= sj + SEL_BLOCK - 1) & (ci + CMP_BLOCK - 1 >= sj) & (np.arange(ncp)[None, :] < n_cmp))
    msel_j = jnp.asarray(msel.astype(np.float32), BF16)
    assert nb <= LANE and seq >= FAR_KEYS and FAR_KEYS % tq == 0 and WINDOW // tq >= 2
    ks_spec = pl.BlockSpec((1, 1, seq, ks.shape[3]), lambda b, g, i: (b, g, 0, 0))
    kw_spec = pl.BlockSpec((1, 1, seq, HEAD_DIM), lambda b, g, i: (b, g, 0, 0))
    v_spec = pl.BlockSpec((1, 1, vs.shape[2], seq), lambda b, g, i: (b, g, 0, 0))
    kc_spec = pl.BlockSpec((1, 1, ncp, HEAD_DIM), lambda b, g, i: (b, g, 0, 0))
    vc_spec = pl.BlockSpec((1, 1, HEAD_DIM, ncp), lambda b, g, i: (b, g, 0, 0))
    width = N_HEADS * HEAD_DIM
    return pl.pallas_call(
        functools.partial(_nsa_attn_kernel, tq=tq, ncp=ncp, nb=nb),
        out_shape=jax.ShapeDtypeStruct((batch * seq, width), BF16),
        grid=(batch, N_KV, nt),
        in_specs=[
            pl.BlockSpec((1, 1, 1, HEAD_DIM, N_HG * tq), lambda b, g, i: (b, g, i, 0, 0)),
            kc_spec, vc_spec, ks_spec, v_spec, kw_spec, v_spec,
            pl.BlockSpec((tq, LANE), lambda b, g, i: (b * nt + i, C_GATE // LANE)),
            pl.BlockSpec((N_HG, 2 * ncp, tq), lambda b, g, i: (g, 0, 0)),
            pl.BlockSpec((N_HG, N_TABLES, tq, tq), lambda b, g, i: (g, 0, 0, 0)),
            pl.BlockSpec(msel_j.shape, lambda b, g, i: (0, 0)),
        ],
        out_specs=pl.BlockSpec((tq, N_HG * HEAD_DIM), lambda b, g, i: (b * nt + i, g)),
        scratch_shapes=[pltpu.VMEM((nt, 8, tq), F32), pltpu.VMEM((2, FAR_KEYS, N_HG * tq), BF16)],
        compiler_params=_cparams(("arbitrary", "arbitrary", "arbitrary")),
        name="nsa_attn",
    )(qn, kc, vc, ks, vs, kw, vw, z, pat, bw, msel_j)


def _merge_cross_kernel(x_ref, yr_ref, yn_ref, zm_ref, wr_ref, wn_ref, wm_ref,
                        g_ref, wq_ref, qg_ref, k_ref, v_ref, wo_ref, o_ref):
    d = x_ref.shape[1]
    dh = d // CA_HEADS
    zm = zm_ref[...].astype(F32)
    br = _dot(yr_ref[...], wr_ref[...])
    bn = _dot(yn_ref[...], wn_ref[...])
    merged = _sigmoid(zm[:, 0:d]) * br + _sigmoid(zm[:, d:2 * d]) * bn
    h1 = x_ref[...] + _dot(merged.astype(BF16), wm_ref[...])

    xn = (_rms(h1, NORM_EPS) * g_ref[...]).astype(BF16)
    qf = _dot(xn, wq_ref[...])
    k = k_ref[0]
    v = v_ref[0]
    outs = []
    for h in range(CA_HEADS):
        sl = slice(h * dh, (h + 1) * dh)
        qh = (_rms(qf[:, sl], QK_EPS) * qg_ref[...] * (dh ** -0.5)).astype(BF16)
        s = _dot_nt(qh, k[:, sl])
        s = s - jnp.max(s, axis=-1, keepdims=True)
        e = jnp.exp(s)
        p = e / jnp.sum(e, axis=-1, keepdims=True)
        outs.append(_dot(p.astype(BF16), v[:, sl]))
    o = jnp.concatenate(outs, axis=1).astype(BF16)
    o_ref[...] = h1 + _dot(o, wo_ref[...])


def _merge_cross(x, yr, yn, z, wr, wn, wm, g, wq, qg, kn, vv, wo, *, batch, seq, tm):
    m, d = x.shape
    nt = seq // tm
    nm = kn.shape[1]
    row = lambda c: pl.BlockSpec((tm, c), lambda b, i: (b * nt + i, 0))
    full = lambda a: pl.BlockSpec(a.shape, lambda b, i: (0, 0))
    kv_spec = pl.BlockSpec((1, nm, d), lambda b, i: (b, 0, 0))
    return pl.pallas_call(
        _merge_cross_kernel,
        out_shape=jax.ShapeDtypeStruct((m, d), F32),
        grid=(batch, nt),
        in_specs=[row(d), row(d), row(d),
                  pl.BlockSpec((tm, 2 * d), lambda b, i: (b * nt + i, C_M // (2 * d))),
                  full(wr), full(wn), full(wm), full(g), full(wq), full(qg), kv_spec, kv_spec, full(wo)],
        out_specs=row(d),
        compiler_params=_cparams(("arbitrary", "arbitrary")),
        name="merge_cross",
    )(x, yr, yn, z, wr, wn, wm, g, wq, qg, kn, vv, wo)


def _ca_kv_kernel(mem_ref, g_ref, w_ref, kg_ref, k_ref, v_ref):
    d = mem_ref.shape[2]
    dh = d // CA_HEADS
    mn = (_rms(mem_ref[0], NORM_EPS) * g_ref[...]).astype(BF16)
    kv = _dot(mn, w_ref[...])
    ks = [(_rms(kv[:, h * dh:(h + 1) * dh], QK_EPS) * kg_ref[...]) for h in range(CA_HEADS)]
    k_ref[0] = jnp.concatenate(ks, axis=1).astype(k_ref.dtype)
    v_ref[0] = kv[:, d:2 * d].astype(v_ref.dtype)


def _ca_kv(mem, g, wkv, kg):
    batch, nm, d = mem.shape
    o_shape = jax.ShapeDtypeStruct((batch, nm, d), BF16)
    o_spec = pl.BlockSpec((1, nm, d), lambda b: (b, 0, 0))
    full = lambda a: pl.BlockSpec(a.shape, lambda b: (0, 0))
    return pl.pallas_call(
        _ca_kv_kernel,
        out_shape=(o_shape, o_shape),
        grid=(batch,),
        in_specs=[o_spec, full(g), full(wkv), full(kg)],
        out_specs=(o_spec, o_spec),
        compiler_params=_cparams(("arbitrary",)),
        name="ca_kv",
    )(mem, g, wkv, kg)


def _ffn_kernel(x_ref, g_ref, wup_ref, cw_ref, cb_ref, wd_ref, o_ref, act_ref, carry_ref,
                *, tiles_per_seq, tn):
    i = pl.program_id(0)
    tm = x_ref.shape[0]
    dff = wd_ref.shape[0]

    @pl.when(i % tiles_per_seq == 0)
    def _():
        carry_ref[...] = jnp.zeros_like(carry_ref)

    x = x_ref[...]
    xn = (_rms(x, NORM_EPS) * g_ref[...]).astype(BF16)
    rowi = lax.broadcasted_iota(jnp.int32, (tm, tn), 0)
    for j in range(dff // tn):
        cs = slice(j * tn, (j + 1) * tn)
        a = _dot(xn, wup_ref[:, cs])
        b = _dot(xn, wup_ref[:, dff + j * tn:dff + (j + 1) * tn])
        car = carry_ref[:, cs]
        p1 = jnp.where(rowi == 0, car[7:8, :], pltpu.roll(a, 1, axis=0))
        p2 = jnp.where(rowi == 0, car[6:7, :], jnp.where(rowi == 1, car[7:8, :], pltpu.roll(a, 2, axis=0)))
        carry_ref[:, cs] = a[tm - 8:tm, :]
        conv = cw_ref[0:1, cs] * p2 + cw_ref[1:2, cs] * p1 + cw_ref[2:3, cs] * a + cb_ref[:, cs]
        act_ref[:, cs] = (conv * _sigmoid(conv) * b).astype(BF16)
    o_ref[...] = x + _dot(act_ref[...], wd_ref[...])


def _ffn(h2, g, wup, cw, cb, wd, *, seq, tm, tn):
    m, d = h2.shape
    dff = wd.shape[0]
    cw8 = jnp.zeros((8, dff), F32).at[0:CONV_W, :].set(cw)
    full = lambda a: pl.BlockSpec(a.shape, lambda i: (0, 0))
    return pl.pallas_call(
        functools.partial(_ffn_kernel, tiles_per_seq=seq // tm, tn=tn),
        out_shape=jax.ShapeDtypeStruct((m, d), F32),
        grid=(m // tm,),
        in_specs=[pl.BlockSpec((tm, d), lambda i: (i, 0)), full(g), full(wup), full(cw8),
                  pl.BlockSpec((1, dff), lambda i: (0, 0)), full(wd)],
        out_specs=pl.BlockSpec((tm, d), lambda i: (i, 0)),
        scratch_shapes=[pltpu.VMEM((tm, dff), BF16), pltpu.VMEM((8, dff), F32)],
        compiler_params=_cparams(("arbitrary",)),
        name="ffn",
    )(h2, g, wup, cw8, cb.reshape(1, dff), wd)


def _pack_perm(width):
    o_zw = 3 * width
    o_za = o_zw + R_LORA_W
    o_zg = o_za + R_LORA_A
    o_q = o_zg + R_LORA_G
    nkv = N_KV * HEAD_DIM
    o_kc = o_q + N_HEADS * HEAD_DIM
    o_gate = o_kc + 6 * nkv
    o_m = o_gate + 3 * N_HEADS
    perm = np.full((Z_COLS,), -1, np.int64)
    perm[C_RKV:C_RKV + 3 * width] = np.arange(3 * width)
    perm[C_Q:C_Q + N_HEADS * HEAD_DIM] = o_q + np.arange(N_HEADS * HEAD_DIM)
    perm[C_KC:C_KC + 6 * nkv] = o_kc + np.arange(6 * nkv)
    perm[C_ZG:C_ZG + R_LORA_G] = o_zg + np.arange(R_LORA_G)
    perm[C_ZWA:C_ZWA + R_LORA_W + R_LORA_A] = o_zw + np.arange(R_LORA_W + R_LORA_A)
    for g in range(N_KV):
        for c in range(3):
            for h in range(N_HG):
                perm[C_GATE + 16 * g + 4 * c + h] = o_gate + (g * N_HG + h) * 3 + c
    perm[C_M:C_M + 2 * width] = o_m + np.arange(2 * width)
    return perm, o_m + 2 * width


def _pad_rows(w, rows, offset=0):
    out = jnp.zeros((rows, w.shape[1]), w.dtype)
    return out.at[offset:offset + w.shape[0]].set(w)


def kernel(x, mem, rel_bias, norm_mix, w_in, rwkv_mu, rwkv_w0, rwkv_w2, rwkv_a0, rwkv_a2, rwkv_g2,
           rwkv_kk, rwkv_ka, rwkv_rk, rwkv_lnx_w, rwkv_lnx_b, nsa_q_gain, nsa_k_gain, cmp_pe_k, cmp_pe_v,
           cmp_w1_k, cmp_w2_k, cmp_w1_v, cmp_w2_v, w_branch_rwkv, w_branch_nsa, w_mix_out,
           norm_cross, norm_mem, ca_wq, ca_wkv, ca_q_gain, ca_k_gain, ca_wo,
           norm_ffn, ffn_up, ffn_conv, ffn_conv_b, ffn_down):
    batch, seq, d = x.shape
    depth = w_in.shape[0]
    width = d
    assert d == N_HEADS * HEAD_DIM and 2 * (Z_COLS - C_M) == 4 * d
    perm, in_cols = _pack_perm(width)
    assert w_in.shape[2] == in_cols
    perm_j = jnp.asarray(np.maximum(perm, 0), jnp.int32)
    keep = jnp.asarray(perm >= 0)

    tq = 256
    assert seq % 256 == 0 and tq > REL_MAX_DIST - 1 and WINDOW % tq == 0
    ncp = -(-(seq // CMP_STRIDE) // LANE) * LANE
    n_cmp = (seq - CMP_BLOCK) // CMP_STRIDE + 1
    tm = min(1024, seq)
    row = lambda v: v.reshape(1, -1).astype(F32)
    nkv = N_KV * HEAD_DIM

    h = x.reshape(batch * seq, d)
    for l in range(depth):
        w_in_p = jnp.where(keep[None, :], jnp.take(w_in[l], perm_j, axis=1), 0.0).astype(BF16)
        mu = rwkv_mu[l]
        mu_r = row(mu[0:3 * width])
        mu_wa = row(mu[3 * width:3 * width + R_LORA_W + R_LORA_A])
        mu_g = row(jnp.zeros((256,), F32).at[0:R_LORA_G].set(mu[3 * width + R_LORA_W + R_LORA_A:]))
        w2p = _pad_rows(rwkv_w2[l], LANE, 0).astype(BF16)
        a2p = _pad_rows(rwkv_a2[l], LANE, R_LORA_W).astype(BF16)
        g2p = _pad_rows(rwkv_g2[l], 256, 0).astype(BF16)

        z = _norm_matmul(h, row(norm_mix[l]), w_in_p, tm=tm, tn=4096, out_dtype=BF16, name="in_proj")

        y_r = _rwkv(z, mu_r, mu_g, mu_wa, row(rwkv_w0[l]), w2p, row(rwkv_a0[l]), a2p, g2p,
                    row(rwkv_kk[l]), row(rwkv_ka[l]), row(rwkv_rk[l]), row(rwkv_lnx_w[l]),
                    row(rwkv_lnx_b[l]), batch=batch, seq=seq, width=width)

        qn, ks, vs, kw, vw = _nsa_prep(
            z, row(jnp.tile(nsa_q_gain[l], N_HEADS)), row(jnp.tile(nsa_k_gain[l, 1], N_KV)),
            row(jnp.tile(nsa_k_gain[l, 2], N_KV)), batch=batch, seq=seq, tq=tq)
        z3 = z.reshape(batch, seq // CMP_STRIDE, CMP_STRIDE, Z_COLS)

        def groups(c0):
            t = z3[:, :, :, c0:c0 + nkv].reshape(batch, seq // CMP_STRIDE, CMP_STRIDE, N_KV, HEAD_DIM)
            return t.transpose(0, 3, 1, 2, 4).reshape(batch, N_KV, seq // CMP_STRIDE, CMP_STRIDE * HEAD_DIM)

        kc, vc = _nsa_cmp(groups(C_KC), groups(C_KC + nkv), cmp_pe_k[l].reshape(1, -1), cmp_pe_v[l].reshape(1, -1),
                          cmp_w1_k[l].astype(BF16), cmp_w2_k[l].astype(BF16), cmp_w1_v[l].astype(BF16),
                          cmp_w2_v[l].astype(BF16), row(nsa_k_gain[l, 0]), ncp=ncp, n_cmp=n_cmp)
        bw, pat = _nsa_bias(rel_bias.astype(F32), tq=tq, ncp=ncp)
        y_n = _nsa_attn(qn, kc, vc, ks, vs, kw, vw, z, pat, bw,
                        batch=batch, seq=seq, tq=tq, ncp=ncp)

        kn, vv = _ca_kv(mem, row(norm_mem[l]), ca_wkv[l].astype(BF16), row(ca_k_gain[l]))
        h2 = _merge_cross(h, y_r, y_n, z, w_branch_rwkv[l].astype(BF16), w_branch_nsa[l].astype(BF16),
                          w_mix_out[l].astype(BF16), row(norm_cross[l]), ca_wq[l].astype(BF16),
                          row(ca_q_gain[l]), kn, vv, ca_wo[l].astype(BF16),
                          batch=batch, seq=seq, tm=min(512, seq))

        h = _ffn(h2, row(norm_ffn[l]), ffn_up[l].astype(BF16), ffn_conv[l], ffn_conv_b[l],
                 ffn_down[l].astype(BF16), seq=seq, tm=min(512, seq), tn=256)
    return h.reshape(batch, seq, d)
```

```python
import functools
import math

import jax
import jax.numpy as jnp
import numpy as np
from jax import lax
from jax.experimental import pallas as pl
from jax.experimental.pallas import tpu as pltpu

F32 = jnp.float32
BF16 = jnp.bfloat16
NEG = -1e30
LOG2E = 1.4426950408889634
FAR_KEYS = 512
T_DIAG, T_NEAR, T_NONE, T_FAR, T_EDGE, T_DIAG_REL, T_NEAR_REL = range(7)
N_TABLES = 7
V_ROWS = 80

R_HEAD = 64
R_LORA_W = 64
R_LORA_A = 64
R_LORA_G = 160
LNX_EPS = 64e-5
N_HEADS = 16
N_KV = 4
N_HG = N_HEADS // N_KV
HEAD_DIM = 64
CMP_BLOCK = 32
CMP_STRIDE = 16
CMP_HIDDEN = 256
SEL_BLOCK = 64
SEL_TOPK = 16
WINDOW = 512
FORCE_SCORE = 1e4
REL_BUCKETS = 32
REL_MAX_DIST = 128
CA_HEADS = 4
CONV_W = 3
NORM_EPS = 1e-6
QK_EPS = 1e-6

LANE = 128
VMEM_LIMIT = 56 * 1024 * 1024

C_RKV = 0
C_Q = 3072
C_KC = 4096
C_ZG = 5632
C_ZWA = 5888
C_GATE = 6016
C_M = 6144
Z_COLS = 8192


def _dot(a, b):
    return jnp.dot(a, b, preferred_element_type=F32)


def _dot_nt(a, b):
    return lax.dot_general(a, b, (((1,), (1,)), ((), ())), preferred_element_type=F32)


def _split_dot(x, e, parts, mode="xe"):
    acc = None
    rem = x
    for i in range(parts):
        hi = rem.astype(BF16)
        t = _dot(hi, e) if mode == "xe" else (_dot(e, hi) if mode == "ex" else _dot_nt(e, hi))
        acc = t if acc is None else acc + t
        if i + 1 < parts:
            rem = rem - hi.astype(F32)
    return acc


def _rms(x, eps):
    return x * lax.rsqrt(jnp.mean(x * x, axis=-1, keepdims=True) + eps)


def _sigmoid(x):
    return 1.0 / (1.0 + jnp.exp(-x))


def _cparams(sem):
    return pltpu.CompilerParams(dimension_semantics=sem, vmem_limit_bytes=VMEM_LIMIT)


def _norm_matmul_kernel(x_ref, g_ref, w_ref, o_ref, xn_ref):
    @pl.when(pl.program_id(1) == 0)
    def _():
        x = x_ref[...]
        xn_ref[...] = (_rms(x, NORM_EPS) * g_ref[...]).astype(BF16)

    o_ref[...] = _dot(xn_ref[...], w_ref[...]).astype(o_ref.dtype)


def _norm_matmul(x, g, w, *, tm, tn, out_dtype, name):
    m, k = x.shape
    n = w.shape[1]
    return pl.pallas_call(
        _norm_matmul_kernel,
        out_shape=jax.ShapeDtypeStruct((m, n), out_dtype),
        grid=(m // tm, n // tn),
        in_specs=[
            pl.BlockSpec((tm, k), lambda i, j: (i, 0)),
            pl.BlockSpec((1, k), lambda i, j: (0, 0)),
            pl.BlockSpec((k, tn), lambda i, j: (0, j)),
        ],
        out_specs=pl.BlockSpec((tm, tn), lambda i, j: (i, j)),
        scratch_shapes=[pltpu.VMEM((tm, k), BF16)],
        compiler_params=_cparams(("arbitrary", "arbitrary")),
        name=name,
    )(x, g, w)


def _rwkv_kernel(zr_ref, zg_ref, zwa_ref, mur_ref, mug_ref, muwa_ref, w0_ref, w2_ref, a0_ref, a2_ref,
                 g2_ref, kk_ref, ka_ref, rk_ref, lnw_ref, lnb_ref, eh_ref, eht_ref, tri_ref,
                 o_ref, pr_ref, pg_ref, pwa_ref, st_ref, *, L, width):
    i = pl.program_id(1)
    npair = width // LANE
    tb = o_ref.shape[0]

    @pl.when(i == 0)
    def _():
        pr_ref[...] = jnp.zeros_like(pr_ref)
        pg_ref[...] = jnp.zeros_like(pg_ref)
        pwa_ref[...] = jnp.zeros_like(pwa_ref)
        st_ref[...] = jnp.zeros_like(st_ref)

    def shifted8(z, prev_ref, mu):
        rolled = pltpu.roll(z, 1, axis=0)
        row0 = lax.broadcasted_iota(jnp.int32, z.shape, 0) == 0
        prev = jnp.where(row0, prev_ref[7:8, :], rolled)
        prev_ref[...] = z[tb - 8:tb, :]
        return z + (prev - z) * mu

    zs = shifted8(zr_ref[...].astype(F32), pr_ref, mur_ref[...])
    zsg = shifted8(zg_ref[...].astype(F32), pg_ref, mug_ref[...])
    zswa = shifted8(zwa_ref[...].astype(F32), pwa_ref, muwa_ref[...])

    r = zs[:, 0:width]
    k = zs[:, width:2 * width]
    v = zs[:, 2 * width:3 * width]

    w_lin = w0_ref[...] + _dot(jnp.tanh(zswa).astype(BF16), w2_ref[...])
    a_lin = a0_ref[...] + _dot(zswa.astype(BF16), a2_ref[...])
    y = -w_lin
    softplus = jnp.maximum(y, 0.0) + jnp.log(1.0 + jnp.exp(-jnp.abs(y)))
    ld = -jnp.exp(-softplus - 0.5)
    a = _sigmoid(a_lin)
    g = _dot(_sigmoid(zsg).astype(BF16), g2_ref[...])

    eh = eh_ref[...]
    eht = eht_ref[...]

    def sum_heads(t):
        return _split_dot(t, eh, 2)

    def bcast_heads(t):
        return _split_dot(t, eht, 2)

    kkr = k * kk_ref[...]
    nrm = jnp.maximum(jnp.sqrt(sum_heads(kkr * kkr)), 1e-12)
    kkn = kkr * bcast_heads(1.0 / nrm)
    k2 = k * (1.0 + (a - 1.0) * ka_ref[...])
    av = -kkn
    bv = kkn * a
    bonus = bcast_heads(sum_heads(r * k2 * rk_ref[...])) * v

    nch = tb // L
    lg = _split_dot(ld, tri_ref[...], 3, mode="ex")
    lasts = [lg[(c + 1) * L - 1:(c + 1) * L, :] for c in range(nch)]
    lg_last = jnp.concatenate([jnp.broadcast_to(t, (L, width)) for t in lasts], axis=0)
    eg = jnp.exp(lg)
    eng = jnp.exp(-lg)
    egl = jnp.exp(lg_last - lg)
    rt = r * eg
    kt = k2 * eng
    bt = bv * eng
    at = av * jnp.exp(lg - ld)
    kgl = k2 * egl
    bgl = bv * egl
    gl = [jnp.exp(t) for t in lasts]

    lane = lax.broadcasted_iota(jnp.int32, (L, LANE), 1)
    lo = lane < R_HEAD

    def stack(t):
        return jnp.concatenate([jnp.where(lo, t, 0.0), jnp.where(lo, 0.0, t)], axis=0)

    row = lax.broadcasted_iota(jnp.int32, (2 * L, 2 * L), 0)
    col = lax.broadcasted_iota(jnp.int32, (2 * L, 2 * L), 1)
    strict = row > col
    incl = row >= col
    eye = (row == col).astype(F32)
    nsq = int(math.log2(L)) - 1

    pairs = range(npair)
    chains = [(c, p) for c in range(nch) for p in pairs]
    sub = lambda t, c, p: t[c * L:(c + 1) * L, p * LANE:(p + 1) * LANE]
    bf = lambda t: t.astype(BF16)
    ar = {cp: bf(jnp.concatenate([stack(sub(at, *cp)), stack(sub(rt, *cp))], axis=0)) for cp in chains}
    bk = {cp: bf(jnp.concatenate([stack(sub(bt, *cp)), stack(sub(kt, *cp))], axis=0)) for cp in chains}
    v_s = {cp: bf(stack(sub(v, *cp))) for cp in chains}
    gm = {cp: _dot_nt(ar[cp], bk[cp]) for cp in chains}
    a_ab = {cp: jnp.where(strict, gm[cp][0:2 * L, 0:2 * L], 0.0) for cp in chains}
    a_ak = {cp: bf(jnp.where(strict, gm[cp][0:2 * L, 2 * L:4 * L], 0.0)) for cp in chains}
    a_rb = {cp: bf(jnp.where(incl, gm[cp][2 * L:4 * L, 0:2 * L], 0.0)) for cp in chains}
    a_rk = {cp: bf(jnp.where(incl, gm[cp][2 * L:4 * L, 2 * L:4 * L], 0.0)) for cp in chains}
    tinv = {cp: eye + a_ab[cp] for cp in chains}
    pw = a_ab
    for _ in range(nsq):
        pwb = {cp: bf(pw[cp]) for cp in chains}
        pw = {cp: _dot(pwb[cp], pwb[cp]) for cp in chains}
        tinv = {cp: tinv[cp] + _dot(bf(pw[cp]), bf(tinv[cp])) for cp in chains}
    tinv = {cp: bf(tinv[cp]) for cp in chains}
    bkgt = {cp: bf(jnp.concatenate([stack(sub(bgl, *cp)), stack(sub(kgl, *cp))], axis=0).T) for cp in chains}
    glcol = {(c, p): jnp.sum(eye * gl[c][:, p * LANE:(p + 1) * LANE], axis=1, keepdims=True)
             for (c, p) in chains}
    h = [st_ref[p] for p in pairs]
    ys = []
    for c in range(nch):
        hb = [bf(h[p]) for p in pairs]
        wmat = [_dot(jnp.concatenate([ar[c, p][0:2 * L], a_ak[c, p]], axis=1),
                     jnp.concatenate([hb[p], v_s[c, p]], axis=0)) for p in pairs]
        ub = [bf(_dot(tinv[c, p], bf(wmat[p]))) for p in pairs]
        yy = [_dot(jnp.concatenate([ar[c, p][2 * L:4 * L], a_rb[c, p], a_rk[c, p]], axis=1),
                   jnp.concatenate([hb[p], ub[p], v_s[c, p]], axis=0)) for p in pairs]
        h = [h[p] * glcol[c, p] + _dot(bkgt[c, p], jnp.concatenate([ub[p], v_s[c, p]], axis=0))
             for p in pairs]
        ys.append(jnp.concatenate([yy[p][0:L] + yy[p][L:2 * L] for p in pairs], axis=1))
    for p in pairs:
        st_ref[p] = h[p]

    yv = jnp.concatenate(ys, axis=0)
    inv_n = 1.0 / R_HEAD
    mean = bcast_heads(sum_heads(yv) * inv_n)
    yc = yv - mean
    var = bcast_heads(sum_heads(yc * yc) * inv_n)
    yn = yc * lax.rsqrt(var + LNX_EPS) * lnw_ref[...] + lnb_ref[...]
    o_ref[...] = ((yn + bonus) * g).astype(o_ref.dtype)


def _rwkv(z, mu_r, mu_g, mu_wa, w0, w2p, a0, a2p, g2p, kk, ka, rk, lnw, lnb, *, batch, seq, width):
    L = 64
    tb = 4 * L
    nt = seq // tb
    nheads = width // R_HEAD
    eh = np.zeros((width, LANE), np.float32)
    eh[np.arange(width), np.arange(width) // R_HEAD] = 1.0
    ti = np.arange(tb)
    tri = ((ti[:, None] >= ti[None, :]) & (ti[:, None] // L == ti[None, :] // L)).astype(np.float32)
    vec = lambda c: pl.BlockSpec((1, c), lambda b, i: (0, 0))
    full = lambda a: pl.BlockSpec(a.shape, lambda b, i: (0,) * a.ndim)
    eh_j = jnp.asarray(eh, BF16)
    eht_j = jnp.asarray(eh.T, BF16)
    tri_j = jnp.asarray(tri, BF16)
    return pl.pallas_call(
        functools.partial(_rwkv_kernel, L=L, width=width),
        out_shape=jax.ShapeDtypeStruct((batch * seq, width), BF16),
        grid=(batch, nt),
        in_specs=[
            pl.BlockSpec((tb, 3 * width), lambda b, i: (b * nt + i, C_RKV // (3 * width))),
            pl.BlockSpec((tb, 256), lambda b, i: (b * nt + i, C_ZG // 256)),
            pl.BlockSpec((tb, LANE), lambda b, i: (b * nt + i, C_ZWA // LANE)),
            vec(3 * width), vec(256), vec(LANE),
            vec(width), full(w2p), vec(width), full(a2p), full(g2p),
            vec(width), vec(width), vec(width), vec(width), vec(width),
            full(eh_j), full(eht_j), full(tri_j),
        ],
        out_specs=pl.BlockSpec((tb, width), lambda b, i: (b * nt + i, 0)),
        scratch_shapes=[
            pltpu.VMEM((8, 3 * width), F32),
            pltpu.VMEM((8, 256), F32),
            pltpu.VMEM((8, LANE), F32),
            pltpu.VMEM((nheads // 2, 2 * R_HEAD, 2 * R_HEAD), F32),
        ],
        compiler_params=_cparams(("arbitrary", "arbitrary")),
        name="rwkv",
    )(z, z, z, mu_r, mu_g, mu_wa, w0, w2p, a0, a2p, g2p, kk, ka, rk, lnw, lnb, eh_j, eht_j, tri_j)


def _nsa_prep_kernel(zq_ref, zs_ref, zw_ref, qg_ref, kgs_ref, kgw_ref, ehq_ref, ehqt_ref,
                     q_ref, ks_ref, vs_ref, kw_ref, vw_ref):
    inv_d = 1.0 / HEAD_DIM

    def head_norm(x, gain, e, et):
        ss = _split_dot(x * x, e, 2)
        inv = lax.rsqrt(ss * inv_d + QK_EPS)
        return x * _split_dot(inv, et, 2) * gain

    nk = N_KV * HEAD_DIM
    ehq = ehq_ref[...]
    ehqt = ehqt_ref[...]
    q = head_norm(zq_ref[...].astype(F32), qg_ref[...], ehq, ehqt) * (HEAD_DIM ** -0.5 * LOG2E)
    qt = q.T
    for gq in range(N_KV):
        rows = [qt[(gq * N_HG + h) * HEAD_DIM:(gq * N_HG + h + 1) * HEAD_DIM, :] for h in range(N_HG)]
        q_ref[0, gq, 0] = jnp.concatenate(rows, axis=1).astype(q_ref.dtype)
    zs = zs_ref[...].astype(F32)
    zw = zw_ref[...].astype(F32)
    ehk = ehq[0:nk, :]
    ehkt = ehqt[:, 0:nk]
    ksn = head_norm(zs[:, 0:nk], kgs_ref[...], ehk, ehkt)
    kwn = head_norm(zw[:, 0:nk], kgw_ref[...], ehk, ehkt)
    vst = zs[:, nk:2 * nk].T
    vwt = zw[:, nk:2 * nk].T
    tb = zs.shape[0]
    vextra = jnp.where(lax.broadcasted_iota(jnp.int32, (V_ROWS - HEAD_DIM, tb), 0) == 0, 1.0, 0.0)
    for gq in range(N_KV):
        sl = slice(gq * HEAD_DIM, (gq + 1) * HEAD_DIM)
        ks_ref[0, gq] = ksn[:, sl].astype(ks_ref.dtype)
        vs_ref[0, gq] = jnp.concatenate([vst[sl, :], vextra], axis=0).astype(vs_ref.dtype)
        kw_ref[0, gq] = kwn[:, sl].astype(kw_ref.dtype)
        vw_ref[0, gq] = jnp.concatenate([vwt[sl, :], vextra], axis=0).astype(vw_ref.dtype)


def _nsa_prep(z, q_gain_t, kgs_t, kgw_t, *, batch, seq, tq):
    tb = tq
    nt = seq // tb
    nq = N_HEADS * HEAD_DIM
    nk = N_KV * HEAD_DIM
    eh = np.zeros((nq, LANE), np.float32)
    eh[np.arange(nq), np.arange(nq) // HEAD_DIM] = 1.0
    eh_j = jnp.asarray(eh, BF16)
    eht_j = jnp.asarray(eh.T, BF16)
    k_shape = jax.ShapeDtypeStruct((batch, N_KV, seq, HEAD_DIM), BF16)
    k_spec = pl.BlockSpec((1, N_KV, tb, HEAD_DIM), lambda b, i: (b, 0, i, 0))
    v_shape = jax.ShapeDtypeStruct((batch, N_KV, V_ROWS, seq), BF16)
    v_spec = pl.BlockSpec((1, N_KV, V_ROWS, tb), lambda b, i: (b, 0, 0, i))
    return pl.pallas_call(
        _nsa_prep_kernel,
        out_shape=(jax.ShapeDtypeStruct((batch, N_KV, nt, HEAD_DIM, N_HG * tq), BF16),
                   k_shape, v_shape, k_shape, v_shape),
        grid=(batch, nt),
        in_specs=[
            pl.BlockSpec((tb, nq), lambda b, i: (b * nt + i, C_Q // nq)),
            pl.BlockSpec((tb, 2 * nk), lambda b, i: (b * nt + i, C_KC // (2 * nk) + 1)),
            pl.BlockSpec((tb, 2 * nk), lambda b, i: (b * nt + i, C_KC // (2 * nk) + 2)),
            pl.BlockSpec((1, nq), lambda b, i: (0, 0)),
            pl.BlockSpec((1, nk), lambda b, i: (0, 0)),
            pl.BlockSpec((1, nk), lambda b, i: (0, 0)),
            pl.BlockSpec(eh_j.shape, lambda b, i: (0, 0)),
            pl.BlockSpec(eht_j.shape, lambda b, i: (0, 0)),
        ],
        out_specs=(pl.BlockSpec((1, N_KV, 1, HEAD_DIM, N_HG * tq), lambda b, i: (b, 0, i, 0, 0)),
                   k_spec, v_spec, k_spec, v_spec),
        compiler_params=_cparams(("arbitrary", "arbitrary")),
        name="nsa_prep",
    )(z, z, z, q_gain_t, kgs_t, kgw_t, eh_j, eht_j)


def _nsa_cmp_kernel(ak_ref, av_ref, pek_ref, pev_ref, w1k_ref, w2k_ref, w1v_ref, w2v_ref, kg_ref,
                    kc_ref, vc_ref, *, n_cmp):
    ncp = kc_ref.shape[2]
    half = w1k_ref.shape[0] // 2

    def compress(a_ref, pe_ref, w1_ref, w2_ref):
        rows = a_ref.shape[2]
        a = a_ref[0, 0].astype(BF16)
        p1 = _dot(a, w1_ref[0:half, :])
        p2 = _dot(a, w1_ref[half:2 * half, :])
        pe8 = jnp.broadcast_to(pe_ref[...], (8, pe_ref.shape[1])).astype(BF16)
        pe_bias = _dot(pe8, w1_ref[...])[0:1, :]
        hpre = p1 + pltpu.roll(p2, rows - 1, axis=0) + pe_bias
        hid = 0.5 * hpre * (1.0 + jnp.tanh(math.sqrt(2.0 / math.pi) * (hpre + 0.044715 * hpre * hpre * hpre)))
        out = _dot(hid.astype(BF16), w2_ref[...])
        if rows < ncp:
            out = jnp.concatenate([out, jnp.zeros((ncp - rows, out.shape[1]), F32)], axis=0)
        return out

    valid = lax.broadcasted_iota(jnp.int32, (ncp, HEAD_DIM), 0) < n_cmp
    kc = compress(ak_ref, pek_ref, w1k_ref, w2k_ref)
    kc = _rms(kc, QK_EPS) * kg_ref[...]
    kc_ref[0, 0] = jnp.where(valid, kc, 0.0).astype(kc_ref.dtype)
    vc = jnp.where(valid, compress(av_ref, pev_ref, w1v_ref, w2v_ref), 0.0)
    vct = jnp.concatenate([vc, jnp.zeros((ncp, LANE - HEAD_DIM), F32)], axis=1).T
    vc_ref[0, 0] = vct[0:HEAD_DIM, :].astype(vc_ref.dtype)


def _nsa_cmp(ak, av, pek, pev, w1k, w2k, w1v, w2v, kg, *, ncp, n_cmp):
    batch, ng, rows, wid = ak.shape
    a_spec = pl.BlockSpec((1, 1, rows, wid), lambda b, g: (b, g, 0, 0))
    full = lambda a: pl.BlockSpec(a.shape, lambda b, g: (0,) * a.ndim)
    k_shape = jax.ShapeDtypeStruct((batch, ng, ncp, HEAD_DIM), BF16)
    k_spec = pl.BlockSpec((1, 1, ncp, HEAD_DIM), lambda b, g: (b, g, 0, 0))
    v_shape = jax.ShapeDtypeStruct((batch, ng, HEAD_DIM, ncp), BF16)
    v_spec = pl.BlockSpec((1, 1, HEAD_DIM, ncp), lambda b, g: (b, g, 0, 0))
    return pl.pallas_call(
        functools.partial(_nsa_cmp_kernel, n_cmp=n_cmp),
        out_shape=(k_shape, v_shape),
        grid=(batch, ng),
        in_specs=[a_spec, a_spec, full(pek), full(pev), full(w1k), full(w2k), full(w1v), full(w2v), full(kg)],
        out_specs=(k_spec, v_spec),
        compiler_params=_cparams(("arbitrary", "arbitrary")),
        name="nsa_cmp",
    )(ak, av, pek, pev, w1k, w2k, w1v, w2v, kg)


def _bucket_thresholds():
    exact = REL_BUCKETS // 2
    n = np.arange(0, REL_MAX_DIST + 1)
    nf = np.maximum(n, 1).astype(np.float64)
    large = exact + (np.log(nf / exact) / math.log(REL_MAX_DIST / exact) * (REL_BUCKETS - exact)).astype(np.int64)
    large = np.minimum(large, REL_BUCKETS - 1)
    bucket = np.where(n < exact, n, large)
    assert bucket[-1] == REL_BUCKETS - 1
    return [int(np.argmax(bucket >= b)) for b in range(REL_BUCKETS)]


def _nsa_bias_kernel(rel_ref, bw_ref, pat_ref, *, tq, ncp):
    h = pl.program_id(0)
    thr = _bucket_thresholds()

    def bias_of(d):
        val = jnp.full(d.shape, rel_ref[0, h], F32)
        for b in range(1, REL_BUCKETS):
            val = jnp.where(d >= thr[b], rel_ref[b, h], val)
        return jnp.where(d < 0, NEG, val * LOG2E)

    kk = lax.broadcasted_iota(jnp.int32, (tq, tq), 0)
    qq = lax.broadcasted_iota(jnp.int32, (tq, tq), 1)
    far = rel_ref[REL_BUCKETS - 1, h] * LOG2E
    diag = bias_of(qq - kk)
    near = bias_of(qq - kk + tq)
    bw_ref[0, T_DIAG] = diag
    bw_ref[0, T_NEAR] = near
    bw_ref[0, T_NONE] = jnp.full((tq, tq), NEG, F32)
    bw_ref[0, T_FAR] = jnp.full((tq, tq), far, F32)
    bw_ref[0, T_EDGE] = jnp.where(kk > qq, far, NEG)
    bw_ref[0, T_DIAG_REL] = jnp.where(qq >= kk, diag - far, NEG)
    bw_ref[0, T_NEAR_REL] = near - far
    cc = lax.broadcasted_iota(jnp.int32, (2 * ncp, tq), 0)
    qc = lax.broadcasted_iota(jnp.int32, (2 * ncp, tq), 1)
    pat_ref[0] = bias_of(qc - CMP_STRIDE * (cc - ncp) - (CMP_BLOCK - 1))


def _nsa_bias(rel, *, tq, ncp):
    return pl.pallas_call(
        functools.partial(_nsa_bias_kernel, tq=tq, ncp=ncp),
        out_shape=(jax.ShapeDtypeStruct((N_HEADS, N_TABLES, tq, tq), F32),
                   jax.ShapeDtypeStruct((N_HEADS, 2 * ncp, tq), F32)),
        grid=(N_HEADS,),
        in_specs=[pl.BlockSpec(memory_space=pltpu.SMEM)],
        out_specs=(pl.BlockSpec((1, N_TABLES, tq, tq), lambda h: (h, 0, 0, 0)),
                   pl.BlockSpec((1, 2 * ncp, tq), lambda h: (h, 0, 0))),
        compiler_params=_cparams(("arbitrary",)),
        name="nsa_bias",
    )(rel)


def _nsa_attn_kernel(q_ref, kc_ref, vc_ref, ks_ref, vs_ref, kw_ref, vw_ref, zg_ref, pat_ref,
                     bw_ref, msel_ref, o_ref, pen_ref, sbuf_ref, *, tq, ncp, nb):
    g = pl.program_id(0)
    qi = pl.program_id(2)
    cols = N_HG * tq
    qt = q_ref[0, 0, 0]

    def heads(fn):
        return jnp.concatenate([fn(h) for h in range(N_HG)], axis=1)

    def table(which, ok=None):
        idx = which if ok is None else jnp.where(ok, which, T_NONE)
        return heads(lambda h: bw_ref[h, idx])

    vrows = vs_ref.shape[2]

    def flash(carry, s, vt):
        m, acc = carry
        if s.dtype == BF16:
            m_new = jnp.maximum(m.astype(BF16), jnp.max(s, axis=0, keepdims=True))
        else:
            m_new = jnp.maximum(m, jnp.max(s, axis=0, keepdims=True)).astype(BF16)
        p = jnp.exp2(s.astype(BF16) - m_new)
        m_new = m_new.astype(F32)
        acc = jnp.exp2(m - m_new) * acc + _dot(vt, p)
        return m_new, acc

    init = (jnp.full((1, cols), NEG, F32), jnp.zeros((vrows, cols), F32))

    def finish(carry):
        acc = carry[1]
        return acc[0:HEAD_DIM] / acc[HEAD_DIM:HEAD_DIM + 1]

    s_c = _dot(kc_ref[0, 0], qt)
    start = pl.multiple_of(ncp - qi * (tq // CMP_STRIDE), tq // CMP_STRIDE)
    bias_c = heads(lambda h: pat_ref[h, pl.ds(start, ncp), :])

    nd = WINDOW // tq
    s_parts, v_parts = [], []
    for d in range(nd, -1, -1):
        off = pl.multiple_of(jnp.maximum(qi - d, 0) * tq, tq)
        s = _dot(kw_ref[0, 0, pl.ds(off, tq), :], qt)
        if d == 0:
            s = s + table(T_DIAG)
        elif d == nd:
            s = s + table(T_EDGE, qi >= d)
        elif d == 1:
            s = s + table(T_NEAR, qi >= d)
        else:
            s = s + table(T_FAR, qi >= d)
        s_parts.append(s)
        v_parts.append(vw_ref[0, 0, :, pl.ds(off, tq)])
    o_w = finish(flash(init, jnp.concatenate(s_parts, axis=0), jnp.concatenate(v_parts, axis=1)))

    sc = s_c + bias_c
    m_c = jnp.max(sc, axis=0, keepdims=True)
    e_c = jnp.exp2(sc - m_c)
    l_c = jnp.sum(e_c, axis=0, keepdims=True)
    p_c = e_c * jnp.where(m_c > 0.5 * NEG, 1.0 / l_c, 0.0)
    o_c = _dot(vc_ref[0, 0], p_c.astype(BF16))

    psum = p_c[:, 0:tq]
    for h in range(1, N_HG):
        psum = psum + p_c[:, h * tq:(h + 1) * tq]
    imp_t = _split_dot(psum, msel_ref[...], 3, mode="ex")
    jj = lax.broadcasted_iota(jnp.int32, (nb, tq), 0)
    tt = qi * tq + lax.broadcasted_iota(jnp.int32, (nb, tq), 1)
    cur = tt // SEL_BLOCK
    forced = (jj == 0) | (jj == cur) | (jj == cur - 1)
    valid_b = jj * SEL_BLOCK <= tt
    imp = jnp.where(valid_b, jnp.where(forced, FORCE_SCORE, imp_t), -jnp.inf)
    groups8 = [imp[r:r + 8, :] for r in range(0, nb, 8)]
    jj8 = lax.broadcasted_iota(jnp.int32, (8, tq), 0)
    cnts = [jnp.zeros((8, tq), F32) for _ in groups8]
    for i in range(nb):
        ri = jnp.broadcast_to(imp[i:i + 1, :], (8, tq))
        for r, grp in enumerate(groups8):
            if 8 * r > i:
                beats = ri >= grp
            elif 8 * r + 7 <= i:
                beats = ri > grp
            else:
                beats = jnp.where(jj8 + 8 * r > i, jnp.where(ri >= grp, 1.0, 0.0), jnp.where(ri > grp, 1.0, 0.0)) > 0.5
            cnts[r] = cnts[r] + jnp.where(beats, 1.0, 0.0)
    cnt = jnp.concatenate(cnts, axis=0)
    sel_t = jnp.where((cnt < float(min(SEL_TOPK, nb))) & valid_b, 0.0, NEG)

    per_tile = tq // SEL_BLOCK
    for t in range(pen_ref.shape[0]):
        pen_ref[t] = jnp.concatenate([sel_t[t * per_tile:(t + 1) * per_tile, :],
                                      jnp.zeros((8 - per_tile, tq), F32)], axis=0)
    blk8 = lax.broadcasted_iota(jnp.int32, (8, tq), 0)

    def tile_pen(t, hi):
        rows = jnp.where(t * per_tile + blk8 >= hi, NEG, pen_ref[t])
        return jnp.concatenate([jnp.broadcast_to(rows[u:u + 1, :], (SEL_BLOCK, tq)) for u in range(per_tile)],
                               axis=0)

    def sel_scores(t0, ntile, hi):
        off = pl.multiple_of(t0 * tq, tq)
        s = _dot(ks_ref[0, 0, pl.ds(off, ntile * tq), :], qt)
        spen = jnp.concatenate([tile_pen(t0 + u, hi) for u in range(ntile)], axis=0)
        return s + heads(lambda h: spen)

    FAR_TILES = FAR_KEYS // tq
    nk = FAR_KEYS
    n_far = jnp.maximum(qi - 1, 0)
    nblk = (n_far + FAR_TILES - 1) // FAR_TILES

    def far_t0(j):
        return jnp.maximum(n_far - FAR_TILES * (j + 1), 0)

    def far_scores(j):
        return sel_scores(far_t0(j), FAR_TILES, (n_far - FAR_TILES * j) * per_tile)

    sbuf_ref[0] = far_scores(0).astype(BF16)
    t1 = jnp.maximum(qi - 1, 0)
    s = jnp.concatenate([sel_scores(t1, 1, qi * per_tile) + table(T_NEAR_REL),
                         sel_scores(qi, 1, (qi + 1) * per_tile) + table(T_DIAG_REL)], axis=0)
    vx = jnp.concatenate([vs_ref[0, 0, :, pl.ds(pl.multiple_of(t1 * tq, tq), tq)],
                          vs_ref[0, 0, :, pl.ds(pl.multiple_of(qi * tq, tq), tq)]], axis=1)
    carry = flash(init, s, vx)

    def far_v(j):
        return vs_ref[0, 0, :, pl.ds(pl.multiple_of(far_t0(j) * tq, tq), nk)]

    def far_body(j, carry):
        slot = j % 2
        sb = sbuf_ref[slot]
        sbuf_ref[1 - slot] = far_scores(j + 1).astype(BF16)
        return flash(carry, sb, far_v(j))

    carry = lax.fori_loop(0, jnp.maximum(nblk - 1, 0), far_body, carry)
    jl = jnp.maximum(nblk - 1, 0)
    carry = lax.cond(nblk > 0, lambda c: flash(c, sbuf_ref[jl % 2], far_v(jl)), lambda c: c, carry)
    o_s = finish(carry)

    zg = pltpu.roll(zg_ref[...].astype(F32), (LANE - 16 * g) % LANE, axis=1)
    gates = _sigmoid(zg).T
    out = (heads(lambda h: gates[h:h + 1, :]) * o_c + heads(lambda h: gates[4 + h:5 + h, :]) * o_s
           + heads(lambda h: gates[8 + h:9 + h, :]) * o_w)
    for hp in range(N_HG // 2):
        pair = jnp.concatenate([out[:, (2 * hp) * tq:(2 * hp + 1) * tq],
                                out[:, (2 * hp + 1) * tq:(2 * hp + 2) * tq]], axis=0)
        o_ref[:, hp * 2 * HEAD_DIM:(hp + 1) * 2 * HEAD_DIM] = pair.T.astype(o_ref.dtype)


def _nsa_attn(qn, kc, vc, ks, vs, kw, vw, z, pat, bw, *, batch, seq, tq, ncp):
    nt = seq // tq
    nb = seq // SEL_BLOCK
    n_cmp = (seq - CMP_BLOCK) // CMP_STRIDE + 1
    ci = np.arange(ncp)[None, :] * CMP_STRIDE
    sj = np.arange(nb)[:, None] * SEL_BLOCK
    msel = ((ci <= sj + SEL_BLOCK - 1) & (ci + CMP_BLOCK - 1 >= sj) & (np.arange(ncp)[None, :] < n_cmp))
    msel_j = jnp.asarray(msel.astype(np.float32), BF16)
    assert nb <= LANE and seq >= FAR_KEYS and FAR_KEYS % tq == 0 and WINDOW // tq >= 2
    ks_spec = pl.BlockSpec((1, 1, seq, ks.shape[3]), lambda g, b, i: (b, g, 0, 0))
    kw_spec = pl.BlockSpec((1, 1, seq, HEAD_DIM), lambda g, b, i: (b, g, 0, 0))
    v_spec = pl.BlockSpec((1, 1, vs.shape[2], seq), lambda g, b, i: (b, g, 0, 0))
    kc_spec = pl.BlockSpec((1, 1, ncp, HEAD_DIM), lambda g, b, i: (b, g, 0, 0))
    vc_spec = pl.BlockSpec((1, 1, HEAD_DIM, ncp), lambda g, b, i: (b, g, 0, 0))
    width = N_HEADS * HEAD_DIM
    return pl.pallas_call(
        functools.partial(_nsa_attn_kernel, tq=tq, ncp=ncp, nb=nb),
        out_shape=jax.ShapeDtypeStruct((batch * seq, width), BF16),
        grid=(N_KV, batch, nt),
        in_specs=[
            pl.BlockSpec((1, 1, 1, HEAD_DIM, N_HG * tq), lambda g, b, i: (b, g, i, 0, 0)),
            kc_spec, vc_spec, ks_spec, v_spec, kw_spec, v_spec,
            pl.BlockSpec((tq, LANE), lambda g, b, i: (b * nt + i, C_GATE // LANE)),
            pl.BlockSpec((N_HG, 2 * ncp, tq), lambda g, b, i: (g, 0, 0)),
            pl.BlockSpec((N_HG, N_TABLES, tq, tq), lambda g, b, i: (g, 0, 0, 0)),
            pl.BlockSpec(msel_j.shape, lambda g, b, i: (0, 0)),
        ],
        out_specs=pl.BlockSpec((tq, N_HG * HEAD_DIM), lambda g, b, i: (b * nt + i, g)),
        scratch_shapes=[pltpu.VMEM((nt, 8, tq), F32), pltpu.VMEM((2, FAR_KEYS, N_HG * tq), BF16)],
        compiler_params=_cparams(("arbitrary", "arbitrary", "arbitrary")),
        name="nsa_attn",
    )(qn, kc, vc, ks, vs, kw, vw, z, pat, bw, msel_j)


def _merge_cross_kernel(x_ref, yr_ref, yn_ref, zm_ref, wr_ref, wn_ref, wm_ref,
                        g_ref, wq_ref, qg_ref, k_ref, v_ref, wo_ref, o_ref):
    d = x_ref.shape[1]
    dh = d // CA_HEADS
    zm = zm_ref[...].astype(F32)
    br = _dot(yr_ref[...], wr_ref[...])
    bn = _dot(yn_ref[...], wn_ref[...])
    merged = _sigmoid(zm[:, 0:d]) * br + _sigmoid(zm[:, d:2 * d]) * bn
    h1 = x_ref[...] + _dot(merged.astype(BF16), wm_ref[...])

    xn = (_rms(h1, NORM_EPS) * g_ref[...]).astype(BF16)
    qf = _dot(xn, wq_ref[...])
    k = k_ref[0]
    v = v_ref[0]
    outs = []
    for h in range(CA_HEADS):
        sl = slice(h * dh, (h + 1) * dh)
        qh = (_rms(qf[:, sl], QK_EPS) * qg_ref[...] * (dh ** -0.5)).astype(BF16)
        s = _dot_nt(qh, k[:, sl])
        s = s - jnp.max(s, axis=-1, keepdims=True)
        e = jnp.exp(s)
        p = e / jnp.sum(e, axis=-1, keepdims=True)
        outs.append(_dot(p.astype(BF16), v[:, sl]))
    o = jnp.concatenate(outs, axis=1).astype(BF16)
    o_ref[...] = h1 + _dot(o, wo_ref[...])


def _merge_cross(x, yr, yn, z, wr, wn, wm, g, wq, qg, kn, vv, wo, *, batch, seq, tm):
    m, d = x.shape
    nt = seq // tm
    nm = kn.shape[1]
    row = lambda c: pl.BlockSpec((tm, c), lambda b, i: (b * nt + i, 0))
    full = lambda a: pl.BlockSpec(a.shape, lambda b, i: (0, 0))
    kv_spec = pl.BlockSpec((1, nm, d), lambda b, i: (b, 0, 0))
    return pl.pallas_call(
        _merge_cross_kernel,
        out_shape=jax.ShapeDtypeStruct((m, d), F32),
        grid=(batch, nt),
        in_specs=[row(d), row(d), row(d),
                  pl.BlockSpec((tm, 2 * d), lambda b, i: (b * nt + i, C_M // (2 * d))),
                  full(wr), full(wn), full(wm), full(g), full(wq), full(qg), kv_spec, kv_spec, full(wo)],
        out_specs=row(d),
        compiler_params=_cparams(("arbitrary", "arbitrary")),
        name="merge_cross",
    )(x, yr, yn, z, wr, wn, wm, g, wq, qg, kn, vv, wo)


def _ca_kv_kernel(mem_ref, g_ref, w_ref, kg_ref, k_ref, v_ref):
    d = mem_ref.shape[2]
    dh = d // CA_HEADS
    mn = (_rms(mem_ref[0], NORM_EPS) * g_ref[...]).astype(BF16)
    kv = _dot(mn, w_ref[...])
    ks = [(_rms(kv[:, h * dh:(h + 1) * dh], QK_EPS) * kg_ref[...]) for h in range(CA_HEADS)]
    k_ref[0] = jnp.concatenate(ks, axis=1).astype(k_ref.dtype)
    v_ref[0] = kv[:, d:2 * d].astype(v_ref.dtype)


def _ca_kv(mem, g, wkv, kg):
    batch, nm, d = mem.shape
    o_shape = jax.ShapeDtypeStruct((batch, nm, d), BF16)
    o_spec = pl.BlockSpec((1, nm, d), lambda b: (b, 0, 0))
    full = lambda a: pl.BlockSpec(a.shape, lambda b: (0, 0))
    return pl.pallas_call(
        _ca_kv_kernel,
        out_shape=(o_shape, o_shape),
        grid=(batch,),
        in_specs=[o_spec, full(g), full(wkv), full(kg)],
        out_specs=(o_spec, o_spec),
        compiler_params=_cparams(("arbitrary",)),
        name="ca_kv",
    )(mem, g, wkv, kg)


def _ffn_kernel(x_ref, g_ref, wup_ref, cw_ref, cb_ref, wd_ref, o_ref, act_ref, carry_ref,
                *, tiles_per_seq, tn):
    i = pl.program_id(0)
    tm = x_ref.shape[0]
    dff = wd_ref.shape[0]

    @pl.when(i % tiles_per_seq == 0)
    def _():
        carry_ref[...] = jnp.zeros_like(carry_ref)

    x = x_ref[...]
    xn = (_rms(x, NORM_EPS) * g_ref[...]).astype(BF16)
    rowi = lax.broadcasted_iota(jnp.int32, (tm, tn), 0)
    for j in range(dff // tn):
        cs = slice(j * tn, (j + 1) * tn)
        a = _dot(xn, wup_ref[:, cs])
        b = _dot(xn, wup_ref[:, dff + j * tn:dff + (j + 1) * tn])
        car = carry_ref[:, cs]
        p1 = jnp.where(rowi == 0, car[7:8, :], pltpu.roll(a, 1, axis=0))
        p2 = jnp.where(rowi == 0, car[6:7, :], jnp.where(rowi == 1, car[7:8, :], pltpu.roll(a, 2, axis=0)))
        carry_ref[:, cs] = a[tm - 8:tm, :]
        conv = cw_ref[0:1, cs] * p2 + cw_ref[1:2, cs] * p1 + cw_ref[2:3, cs] * a + cb_ref[:, cs]
        act_ref[:, cs] = (conv * _sigmoid(conv) * b).astype(BF16)
    o_ref[...] = x + _dot(act_ref[...], wd_ref[...])


def _ffn(h2, g, wup, cw, cb, wd, *, seq, tm, tn):
    m, d = h2.shape
    dff = wd.shape[0]
    cw8 = jnp.zeros((8, dff), F32).at[0:CONV_W, :].set(cw)
    full = lambda a: pl.BlockSpec(a.shape, lambda i: (0, 0))
    return pl.pallas_call(
        functools.partial(_ffn_kernel, tiles_per_seq=seq // tm, tn=tn),
        out_shape=jax.ShapeDtypeStruct((m, d), F32),
        grid=(m // tm,),
        in_specs=[pl.BlockSpec((tm, d), lambda i: (i, 0)), full(g), full(wup), full(cw8),
                  pl.BlockSpec((1, dff), lambda i: (0, 0)), full(wd)],
        out_specs=pl.BlockSpec((tm, d), lambda i: (i, 0)),
        scratch_shapes=[pltpu.VMEM((tm, dff), BF16), pltpu.VMEM((8, dff), F32)],
        compiler_params=_cparams(("arbitrary",)),
        name="ffn",
    )(h2, g, wup, cw8, cb.reshape(1, dff), wd)


def _pack_perm(width):
    o_zw = 3 * width
    o_za = o_zw + R_LORA_W
    o_zg = o_za + R_LORA_A
    o_q = o_zg + R_LORA_G
    nkv = N_KV * HEAD_DIM
    o_kc = o_q + N_HEADS * HEAD_DIM
    o_gate = o_kc + 6 * nkv
    o_m = o_gate + 3 * N_HEADS
    perm = np.full((Z_COLS,), -1, np.int64)
    perm[C_RKV:C_RKV + 3 * width] = np.arange(3 * width)
    perm[C_Q:C_Q + N_HEADS * HEAD_DIM] = o_q + np.arange(N_HEADS * HEAD_DIM)
    perm[C_KC:C_KC + 6 * nkv] = o_kc + np.arange(6 * nkv)
    perm[C_ZG:C_ZG + R_LORA_G] = o_zg + np.arange(R_LORA_G)
    perm[C_ZWA:C_ZWA + R_LORA_W + R_LORA_A] = o_zw + np.arange(R_LORA_W + R_LORA_A)
    for g in range(N_KV):
        for c in range(3):
            for h in range(N_HG):
                perm[C_GATE + 16 * g + 4 * c + h] = o_gate + (g * N_HG + h) * 3 + c
    perm[C_M:C_M + 2 * width] = o_m + np.arange(2 * width)
    return perm, o_m + 2 * width


def _pad_rows(w, rows, offset=0):
    out = jnp.zeros((rows, w.shape[1]), w.dtype)
    return out.at[offset:offset + w.shape[0]].set(w)


def kernel(x, mem, rel_bias, norm_mix, w_in, rwkv_mu, rwkv_w0, rwkv_w2, rwkv_a0, rwkv_a2, rwkv_g2,
           rwkv_kk, rwkv_ka, rwkv_rk, rwkv_lnx_w, rwkv_lnx_b, nsa_q_gain, nsa_k_gain, cmp_pe_k, cmp_pe_v,
           cmp_w1_k, cmp_w2_k, cmp_w1_v, cmp_w2_v, w_branch_rwkv, w_branch_nsa, w_mix_out,
           norm_cross, norm_mem, ca_wq, ca_wkv, ca_q_gain, ca_k_gain, ca_wo,
           norm_ffn, ffn_up, ffn_conv, ffn_conv_b, ffn_down):
    batch, seq, d = x.shape
    depth = w_in.shape[0]
    width = d
    assert d == N_HEADS * HEAD_DIM and 2 * (Z_COLS - C_M) == 4 * d
    perm, in_cols = _pack_perm(width)
    assert w_in.shape[2] == in_cols
    perm_j = jnp.asarray(np.maximum(perm, 0), jnp.int32)
    keep = jnp.asarray(perm >= 0)

    tq = 256
    assert seq % 256 == 0 and tq > REL_MAX_DIST - 1 and WINDOW % tq == 0
    ncp = -(-(seq // CMP_STRIDE) // LANE) * LANE
    n_cmp = (seq - CMP_BLOCK) // CMP_STRIDE + 1
    tm = min(1024, seq)
    row = lambda v: v.reshape(1, -1).astype(F32)
    nkv = N_KV * HEAD_DIM

    h = x.reshape(batch * seq, d)
    for l in range(depth):
        w_in_p = jnp.where(keep[None, :], jnp.take(w_in[l], perm_j, axis=1), 0.0).astype(BF16)
        mu = rwkv_mu[l]
        mu_r = row(mu[0:3 * width])
        mu_wa = row(mu[3 * width:3 * width + R_LORA_W + R_LORA_A])
        mu_g = row(jnp.zeros((256,), F32).at[0:R_LORA_G].set(mu[3 * width + R_LORA_W + R_LORA_A:]))
        w2p = _pad_rows(rwkv_w2[l], LANE, 0).astype(BF16)
        a2p = _pad_rows(rwkv_a2[l], LANE, R_LORA_W).astype(BF16)
        g2p = _pad_rows(rwkv_g2[l], 256, 0).astype(BF16)

        z = _norm_matmul(h, row(norm_mix[l]), w_in_p, tm=tm, tn=4096, out_dtype=BF16, name="in_proj")

        y_r = _rwkv(z, mu_r, mu_g, mu_wa, row(rwkv_w0[l]), w2p, row(rwkv_a0[l]), a2p, g2p,
                    row(rwkv_kk[l]), row(rwkv_ka[l]), row(rwkv_rk[l]), row(rwkv_lnx_w[l]),
                    row(rwkv_lnx_b[l]), batch=batch, seq=seq, width=width)

        qn, ks, vs, kw, vw = _nsa_prep(
            z, row(jnp.tile(nsa_q_gain[l], N_HEADS)), row(jnp.tile(nsa_k_gain[l, 1], N_KV)),
            row(jnp.tile(nsa_k_gain[l, 2], N_KV)), batch=batch, seq=seq, tq=tq)
        z3 = z.reshape(batch, seq // CMP_STRIDE, CMP_STRIDE, Z_COLS)

        def groups(c0):
            t = z3[:, :, :, c0:c0 + nkv].reshape(batch, seq // CMP_STRIDE, CMP_STRIDE, N_KV, HEAD_DIM)
            return t.transpose(0, 3, 1, 2, 4).reshape(batch, N_KV, seq // CMP_STRIDE, CMP_STRIDE * HEAD_DIM)

        kc, vc = _nsa_cmp(groups(C_KC), groups(C_KC + nkv), cmp_pe_k[l].reshape(1, -1), cmp_pe_v[l].reshape(1, -1),
                          cmp_w1_k[l].astype(BF16), cmp_w2_k[l].astype(BF16), cmp_w1_v[l].astype(BF16),
                          cmp_w2_v[l].astype(BF16), row(nsa_k_gain[l, 0]), ncp=ncp, n_cmp=n_cmp)
        bw, pat = _nsa_bias(rel_bias.astype(F32), tq=tq, ncp=ncp)
        y_n = _nsa_attn(qn, kc, vc, ks, vs, kw, vw, z, pat, bw,
                        batch=batch, seq=seq, tq=tq, ncp=ncp)

        kn, vv = _ca_kv(mem, row(norm_mem[l]), ca_wkv[l].astype(BF16), row(ca_k_gain[l]))
        h2 = _merge_cross(h, y_r, y_n, z, w_branch_rwkv[l].astype(BF16), w_branch_nsa[l].astype(BF16),
                          w_mix_out[l].astype(BF16), row(norm_cross[l]), ca_wq[l].astype(BF16),
                          row(ca_q_gain[l]), kn, vv, ca_wo[l].astype(BF16),
                          batch=batch, seq=seq, tm=min(512, seq))

        h = _ffn(h2, row(norm_ffn[l]), ffn_up[l].astype(BF16), ffn_conv[l], ffn_conv_b[l],
                 ffn_down[l].astype(BF16), seq=seq, tm=min(512, seq), tn=256)
    return h.reshape(batch, seq, d)
```

```python
import functools
import math

import jax
import jax.numpy as jnp
import numpy as np
from jax import lax
from jax.experimental import pallas as pl
from jax.experimental.pallas import tpu as pltpu

F32 = jnp.float32
BF16 = jnp.bfloat16
NEG = -1e30
LOG2E = 1.4426950408889634
FAR_KEYS = 512
T_DIAG, T_NEAR, T_NONE, T_FAR, T_EDGE, T_DIAG_REL, T_NEAR_REL = range(7)
N_TABLES = 7
V_ROWS = 80

R_HEAD = 64
R_LORA_W = 64
R_LORA_A = 64
R_LORA_G = 160
LNX_EPS = 64e-5
N_HEADS = 16
N_KV = 4
N_HG = N_HEADS // N_KV
HEAD_DIM = 64
CMP_BLOCK = 32
CMP_STRIDE = 16
CMP_HIDDEN = 256
SEL_BLOCK = 64
SEL_TOPK = 16
WINDOW = 512
FORCE_SCORE = 1e4
REL_BUCKETS = 32
REL_MAX_DIST = 128
CA_HEADS = 4
CONV_W = 3
NORM_EPS = 1e-6
QK_EPS = 1e-6

LANE = 128
VMEM_LIMIT = 56 * 1024 * 1024

C_RKV = 0
C_Q = 3072
C_KC = 4096
C_ZG = 5632
C_ZWA = 5888
C_GATE = 6016
C_M = 6144
Z_COLS = 8192


def _dot(a, b):
    return jnp.dot(a, b, preferred_element_type=F32)


def _dot_nt(a, b):
    return lax.dot_general(a, b, (((1,), (1,)), ((), ())), preferred_element_type=F32)


def _split_dot(x, e, parts, mode="xe"):
    acc = None
    rem = x
    for i in range(parts):
        hi = rem.astype(BF16)
        t = _dot(hi, e) if mode == "xe" else (_dot(e, hi) if mode == "ex" else _dot_nt(e, hi))
        acc = t if acc is None else acc + t
        if i + 1 < parts:
            rem = rem - hi.astype(F32)
    return acc


def _rms(x, eps):
    return x * lax.rsqrt(jnp.mean(x * x, axis=-1, keepdims=True) + eps)


def _sigmoid(x):
    return 1.0 / (1.0 + jnp.exp(-x))


def _cparams(sem):
    return pltpu.CompilerParams(dimension_semantics=sem, vmem_limit_bytes=VMEM_LIMIT)


def _norm_matmul_kernel(x_ref, g_ref, w_ref, o_ref, xn_ref):
    @pl.when(pl.program_id(1) == 0)
    def _():
        x = x_ref[...]
        xn_ref[...] = (_rms(x, NORM_EPS) * g_ref[...]).astype(BF16)

    o_ref[...] = _dot(xn_ref[...], w_ref[...]).astype(o_ref.dtype)


def _norm_matmul(x, g, w, *, tm, tn, out_dtype, name):
    m, k = x.shape
    n = w.shape[1]
    return pl.pallas_call(
        _norm_matmul_kernel,
        out_shape=jax.ShapeDtypeStruct((m, n), out_dtype),
        grid=(m // tm, n // tn),
        in_specs=[
            pl.BlockSpec((tm, k), lambda i, j: (i, 0)),
            pl.BlockSpec((1, k), lambda i, j: (0, 0)),
            pl.BlockSpec((k, tn), lambda i, j: (0, j)),
        ],
        out_specs=pl.BlockSpec((tm, tn), lambda i, j: (i, j)),
        scratch_shapes=[pltpu.VMEM((tm, k), BF16)],
        compiler_params=_cparams(("arbitrary", "arbitrary")),
        name=name,
    )(x, g, w)


def _rwkv_kernel(zr_ref, zg_ref, zwa_ref, mur_ref, mug_ref, muwa_ref, w0_ref, w2_ref, a0_ref, a2_ref,
                 g2_ref, kk_ref, ka_ref, rk_ref, lnw_ref, lnb_ref, eh_ref, eht_ref, tri_ref,
                 o_ref, pr_ref, pg_ref, pwa_ref, st_ref, *, L, width):
    i = pl.program_id(1)
    npair = width // LANE
    tb = o_ref.shape[0]

    @pl.when(i == 0)
    def _():
        pr_ref[...] = jnp.zeros_like(pr_ref)
        pg_ref[...] = jnp.zeros_like(pg_ref)
        pwa_ref[...] = jnp.zeros_like(pwa_ref)
        st_ref[...] = jnp.zeros_like(st_ref)

    def shifted8(z, prev_ref, mu):
        rolled = pltpu.roll(z, 1, axis=0)
        row0 = lax.broadcasted_iota(jnp.int32, z.shape, 0) == 0
        prev = jnp.where(row0, prev_ref[7:8, :], rolled)
        prev_ref[...] = z[tb - 8:tb, :]
        return z + (prev - z) * mu

    zs = shifted8(zr_ref[...].astype(F32), pr_ref, mur_ref[...])
    zsg = shifted8(zg_ref[...].astype(F32), pg_ref, mug_ref[...])
    zswa = shifted8(zwa_ref[...].astype(F32), pwa_ref, muwa_ref[...])

    r = zs[:, 0:width]
    k = zs[:, width:2 * width]
    v = zs[:, 2 * width:3 * width]

    w_lin = w0_ref[...] + _dot(jnp.tanh(zswa).astype(BF16), w2_ref[...])
    a_lin = a0_ref[...] + _dot(zswa.astype(BF16), a2_ref[...])
    y = -w_lin
    softplus = jnp.maximum(y, 0.0) + jnp.log(1.0 + jnp.exp(-jnp.abs(y)))
    ld = -jnp.exp(-softplus - 0.5)
    a = _sigmoid(a_lin)
    g = _dot(_sigmoid(zsg).astype(BF16), g2_ref[...])

    eh = eh_ref[...]
    eht = eht_ref[...]

    def sum_heads(t):
        return _split_dot(t, eh, 2)

    def bcast_heads(t):
        return _split_dot(t, eht, 2)

    kkr = k * kk_ref[...]
    nrm = jnp.maximum(jnp.sqrt(sum_heads(kkr * kkr)), 1e-12)
    kkn = kkr * bcast_heads(1.0 / nrm)
    k2 = k * (1.0 + (a - 1.0) * ka_ref[...])
    av = -kkn
    bv = kkn * a
    bonus = bcast_heads(sum_heads(r * k2 * rk_ref[...])) * v

    nch = tb // L
    lg = _split_dot(ld, tri_ref[...], 3, mode="ex")
    lasts = [lg[(c + 1) * L - 1:(c + 1) * L, :] for c in range(nch)]
    lg_last = jnp.concatenate([jnp.broadcast_to(t, (L, width)) for t in lasts], axis=0)
    eg = jnp.exp(lg)
    eng = jnp.exp(-lg)
    egl = jnp.exp(lg_last - lg)
    rt = r * eg
    kt = k2 * eng
    bt = bv * eng
    at = av * jnp.exp(lg - ld)
    kgl = k2 * egl
    bgl = bv * egl
    gl = [jnp.exp(t) for t in lasts]

    lane = lax.broadcasted_iota(jnp.int32, (L, LANE), 1)
    lo = lane < R_HEAD

    def stack(t):
        return jnp.concatenate([jnp.where(lo, t, 0.0), jnp.where(lo, 0.0, t)], axis=0)

    row = lax.broadcasted_iota(jnp.int32, (2 * L, 2 * L), 0)
    col = lax.broadcasted_iota(jnp.int32, (2 * L, 2 * L), 1)
    strict = row > col
    incl = row >= col
    eye = (row == col).astype(F32)
    nsq = int(math.log2(L)) - 1

    pairs = range(npair)
    chains = [(c, p) for c in range(nch) for p in pairs]
    sub = lambda t, c, p: t[c * L:(c + 1) * L, p * LANE:(p + 1) * LANE]
    bf = lambda t: t.astype(BF16)
    ar = {cp: bf(jnp.concatenate([stack(sub(at, *cp)), stack(sub(rt, *cp))], axis=0)) for cp in chains}
    bk = {cp: bf(jnp.concatenate([stack(sub(bt, *cp)), stack(sub(kt, *cp))], axis=0)) for cp in chains}
    v_s = {cp: bf(stack(sub(v, *cp))) for cp in chains}
    gm = {cp: _dot_nt(ar[cp], bk[cp]) for cp in chains}
    a_ab = {cp: jnp.where(strict, gm[cp][0:2 * L, 0:2 * L], 0.0) for cp in chains}
    a_ak = {cp: bf(jnp.where(strict, gm[cp][0:2 * L, 2 * L:4 * L], 0.0)) for cp in chains}
    a_rb = {cp: bf(jnp.where(incl, gm[cp][2 * L:4 * L, 0:2 * L], 0.0)) for cp in chains}
    a_rk = {cp: bf(jnp.where(incl, gm[cp][2 * L:4 * L, 2 * L:4 * L], 0.0)) for cp in chains}
    tinv = {cp: eye + a_ab[cp] for cp in chains}
    pw = a_ab
    for _ in range(nsq):
        pwb = {cp: bf(pw[cp]) for cp in chains}
        pw = {cp: _dot(pwb[cp], pwb[cp]) for cp in chains}
        tinv = {cp: tinv[cp] + _dot(bf(pw[cp]), bf(tinv[cp])) for cp in chains}
    tinv = {cp: bf(tinv[cp]) for cp in chains}
    bkgt = {cp: bf(jnp.concatenate([stack(sub(bgl, *cp)), stack(sub(kgl, *cp))], axis=0).T) for cp in chains}
    glcol = {(c, p): jnp.sum(eye * gl[c][:, p * LANE:(p + 1) * LANE], axis=1, keepdims=True)
             for (c, p) in chains}
    h = [st_ref[p] for p in pairs]
    ys = []
    for c in range(nch):
        hb = [bf(h[p]) for p in pairs]
        wmat = [_dot(jnp.concatenate([ar[c, p][0:2 * L], a_ak[c, p]], axis=1),
                     jnp.concatenate([hb[p], v_s[c, p]], axis=0)) for p in pairs]
        ub = [bf(_dot(tinv[c, p], bf(wmat[p]))) for p in pairs]
        yy = [_dot(jnp.concatenate([ar[c, p][2 * L:4 * L], a_rb[c, p], a_rk[c, p]], axis=1),
                   jnp.concatenate([hb[p], ub[p], v_s[c, p]], axis=0)) for p in pairs]
        h = [h[p] * glcol[c, p] + _dot(bkgt[c, p], jnp.concatenate([ub[p], v_s[c, p]], axis=0))
             for p in pairs]
        ys.append(jnp.concatenate([yy[p][0:L] + yy[p][L:2 * L] for p in pairs], axis=1))
    for p in pairs:
        st_ref[p] = h[p]

    yv = jnp.concatenate(ys, axis=0)
    inv_n = 1.0 / R_HEAD
    mean = bcast_heads(sum_heads(yv) * inv_n)
    yc = yv - mean
    var = bcast_heads(sum_heads(yc * yc) * inv_n)
    yn = yc * lax.rsqrt(var + LNX_EPS) * lnw_ref[...] + lnb_ref[...]
    o_ref[...] = ((yn + bonus) * g).astype(o_ref.dtype)


def _rwkv(z, mu_r, mu_g, mu_wa, w0, w2p, a0, a2p, g2p, kk, ka, rk, lnw, lnb, *, batch, seq, width):
    L = 64
    tb = 4 * L
    nt = seq // tb
    nheads = width // R_HEAD
    eh = np.zeros((width, LANE), np.float32)
    eh[np.arange(width), np.arange(width) // R_HEAD] = 1.0
    ti = np.arange(tb)
    tri = ((ti[:, None] >= ti[None, :]) & (ti[:, None] // L == ti[None, :] // L)).astype(np.float32)
    vec = lambda c: pl.BlockSpec((1, c), lambda b, i: (0, 0))
    full = lambda a: pl.BlockSpec(a.shape, lambda b, i: (0,) * a.ndim)
    eh_j = jnp.asarray(eh, BF16)
    eht_j = jnp.asarray(eh.T, BF16)
    tri_j = jnp.asarray(tri, BF16)
    return pl.pallas_call(
        functools.partial(_rwkv_kernel, L=L, width=width),
        out_shape=jax.ShapeDtypeStruct((batch * seq, width), BF16),
        grid=(batch, nt),
        in_specs=[
            pl.BlockSpec((tb, 3 * width), lambda b, i: (b * nt + i, C_RKV // (3 * width))),
            pl.BlockSpec((tb, 256), lambda b, i: (b * nt + i, C_ZG // 256)),
            pl.BlockSpec((tb, LANE), lambda b, i: (b * nt + i, C_ZWA // LANE)),
            vec(3 * width), vec(256), vec(LANE),
            vec(width), full(w2p), vec(width), full(a2p), full(g2p),
            vec(width), vec(width), vec(width), vec(width), vec(width),
            full(eh_j), full(eht_j), full(tri_j),
        ],
        out_specs=pl.BlockSpec((tb, width), lambda b, i: (b * nt + i, 0)),
        scratch_shapes=[
            pltpu.VMEM((8, 3 * width), F32),
            pltpu.VMEM((8, 256), F32),
            pltpu.VMEM((8, LANE), F32),
            pltpu.VMEM((nheads // 2, 2 * R_HEAD, 2 * R_HEAD), F32),
        ],
        compiler_params=_cparams(("arbitrary", "arbitrary")),
        name="rwkv",
    )(z, z, z, mu_r, mu_g, mu_wa, w0, w2p, a0, a2p, g2p, kk, ka, rk, lnw, lnb, eh_j, eht_j, tri_j)


def _nsa_prep_kernel(zq_ref, zs_ref, zw_ref, qg_ref, kgs_ref, kgw_ref, ehq_ref, ehqt_ref,
                     q_ref, ks_ref, vs_ref, kw_ref, vw_ref):
    inv_d = 1.0 / HEAD_DIM

    def head_norm(x, gain, e, et):
        ss = _split_dot(x * x, e, 2)
        inv = lax.rsqrt(ss * inv_d + QK_EPS)
        return x * _split_dot(inv, et, 2) * gain

    nk = N_KV * HEAD_DIM
    ehq = ehq_ref[...]
    ehqt = ehqt_ref[...]
    q = head_norm(zq_ref[...].astype(F32), qg_ref[...], ehq, ehqt) * (HEAD_DIM ** -0.5 * LOG2E)
    qt = q.T
    for gq in range(N_KV):
        rows = [qt[(gq * N_HG + h) * HEAD_DIM:(gq * N_HG + h + 1) * HEAD_DIM, :] for h in range(N_HG)]
        q_ref[0, gq, 0] = jnp.concatenate(rows, axis=1).astype(q_ref.dtype)
    zs = zs_ref[...].astype(F32)
    zw = zw_ref[...].astype(F32)
    ehk = ehq[0:nk, :]
    ehkt = ehqt[:, 0:nk]
    ksn = head_norm(zs[:, 0:nk], kgs_ref[...], ehk, ehkt)
    kwn = head_norm(zw[:, 0:nk], kgw_ref[...], ehk, ehkt)
    vst = zs[:, nk:2 * nk].T
    vwt = zw[:, nk:2 * nk].T
    tb = zs.shape[0]
    vextra = jnp.where(lax.broadcasted_iota(jnp.int32, (V_ROWS - HEAD_DIM, tb), 0) == 0, 1.0, 0.0)
    for gq in range(N_KV):
        sl = slice(gq * HEAD_DIM, (gq + 1) * HEAD_DIM)
        ks_ref[0, gq] = ksn[:, sl].astype(ks_ref.dtype)
        vs_ref[0, gq] = jnp.concatenate([vst[sl, :], vextra], axis=0).astype(vs_ref.dtype)
        kw_ref[0, gq] = kwn[:, sl].astype(kw_ref.dtype)
        vw_ref[0, gq] = jnp.concatenate([vwt[sl, :], vextra], axis=0).astype(vw_ref.dtype)


def _nsa_prep(z, q_gain_t, kgs_t, kgw_t, *, batch, seq, tq):
    tb = tq
    nt = seq // tb
    nq = N_HEADS * HEAD_DIM
    nk = N_KV * HEAD_DIM
    eh = np.zeros((nq, LANE), np.float32)
    eh[np.arange(nq), np.arange(nq) // HEAD_DIM] = 1.0
    eh_j = jnp.asarray(eh, BF16)
    eht_j = jnp.asarray(eh.T, BF16)
    k_shape = jax.ShapeDtypeStruct((batch, N_KV, seq, HEAD_DIM), BF16)
    k_spec = pl.BlockSpec((1, N_KV, tb, HEAD_DIM), lambda b, i: (b, 0, i, 0))
    v_shape = jax.ShapeDtypeStruct((batch, N_KV, V_ROWS, seq), BF16)
    v_spec = pl.BlockSpec((1, N_KV, V_ROWS, tb), lambda b, i: (b, 0, 0, i))
    return pl.pallas_call(
        _nsa_prep_kernel,
        out_shape=(jax.ShapeDtypeStruct((batch, N_KV, nt, HEAD_DIM, N_HG * tq), BF16),
                   k_shape, v_shape, k_shape, v_shape),
        grid=(batch, nt),
        in_specs=[
            pl.BlockSpec((tb, nq), lambda b, i: (b * nt + i, C_Q // nq)),
            pl.BlockSpec((tb, 2 * nk), lambda b, i: (b * nt + i, C_KC // (2 * nk) + 1)),
            pl.BlockSpec((tb, 2 * nk), lambda b, i: (b * nt + i, C_KC // (2 * nk) + 2)),
            pl.BlockSpec((1, nq), lambda b, i: (0, 0)),
            pl.BlockSpec((1, nk), lambda b, i: (0, 0)),
            pl.BlockSpec((1, nk), lambda b, i: (0, 0)),
            pl.BlockSpec(eh_j.shape, lambda b, i: (0, 0)),
            pl.BlockSpec(eht_j.shape, lambda b, i: (0, 0)),
        ],
        out_specs=(pl.BlockSpec((1, N_KV, 1, HEAD_DIM, N_HG * tq), lambda b, i: (b, 0, i, 0, 0)),
                   k_spec, v_spec, k_spec, v_spec),
        compiler_params=_cparams(("arbitrary", "arbitrary")),
        name="nsa_prep",
    )(z, z, z, q_gain_t, kgs_t, kgw_t, eh_j, eht_j)


def _nsa_cmp_kernel(ak_ref, av_ref, pek_ref, pev_ref, w1k_ref, w2k_ref, w1v_ref, w2v_ref, kg_ref,
                    kc_ref, vc_ref, *, n_cmp):
    ncp = kc_ref.shape[2]
    half = w1k_ref.shape[0] // 2

    def compress(a_ref, pe_ref, w1_ref, w2_ref):
        rows = a_ref.shape[2]
        a = a_ref[0, 0].astype(BF16)
        p1 = _dot(a, w1_ref[0:half, :])
        p2 = _dot(a, w1_ref[half:2 * half, :])
        pe8 = jnp.broadcast_to(pe_ref[...], (8, pe_ref.shape[1])).astype(BF16)
        pe_bias = _dot(pe8, w1_ref[...])[0:1, :]
        hpre = p1 + pltpu.roll(p2, rows - 1, axis=0) + pe_bias
        hid = 0.5 * hpre * (1.0 + jnp.tanh(math.sqrt(2.0 / math.pi) * (hpre + 0.044715 * hpre * hpre * hpre)))
        out = _dot(hid.astype(BF16), w2_ref[...])
        if rows < ncp:
            out = jnp.concatenate([out, jnp.zeros((ncp - rows, out.shape[1]), F32)], axis=0)
        return out

    valid = lax.broadcasted_iota(jnp.int32, (ncp, HEAD_DIM), 0) < n_cmp
    kc = compress(ak_ref, pek_ref, w1k_ref, w2k_ref)
    kc = _rms(kc, QK_EPS) * kg_ref[...]
    kc_ref[0, 0] = jnp.where(valid, kc, 0.0).astype(kc_ref.dtype)
    vc = jnp.where(valid, compress(av_ref, pev_ref, w1v_ref, w2v_ref), 0.0)
    vct = jnp.concatenate([vc, jnp.zeros((ncp, LANE - HEAD_DIM), F32)], axis=1).T
    vc_ref[0, 0] = vct[0:HEAD_DIM, :].astype(vc_ref.dtype)


def _nsa_cmp(ak, av, pek, pev, w1k, w2k, w1v, w2v, kg, *, ncp, n_cmp):
    batch, ng, rows, wid = ak.shape
    a_spec = pl.BlockSpec((1, 1, rows, wid), lambda b, g: (b, g, 0, 0))
    full = lambda a: pl.BlockSpec(a.shape, lambda b, g: (0,) * a.ndim)
    k_shape = jax.ShapeDtypeStruct((batch, ng, ncp, HEAD_DIM), BF16)
    k_spec = pl.BlockSpec((1, 1, ncp, HEAD_DIM), lambda b, g: (b, g, 0, 0))
    v_shape = jax.ShapeDtypeStruct((batch, ng, HEAD_DIM, ncp), BF16)
    v_spec = pl.BlockSpec((1, 1, HEAD_DIM, ncp), lambda b, g: (b, g, 0, 0))
    return pl.pallas_call(
        functools.partial(_nsa_cmp_kernel, n_cmp=n_cmp),
        out_shape=(k_shape, v_shape),
        grid=(batch, ng),
        in_specs=[a_spec, a_spec, full(pek), full(pev), full(w1k), full(w2k), full(w1v), full(w2v), full(kg)],
        out_specs=(k_spec, v_spec),
        compiler_params=_cparams(("arbitrary", "arbitrary")),
        name="nsa_cmp",
    )(ak, av, pek, pev, w1k, w2k, w1v, w2v, kg)


def _bucket_thresholds():
    exact = REL_BUCKETS // 2
    n = np.arange(0, REL_MAX_DIST + 1)
    nf = np.maximum(n, 1).astype(np.float64)
    large = exact + (np.log(nf / exact) / math.log(REL_MAX_DIST / exact) * (REL_BUCKETS - exact)).astype(np.int64)
    large = np.minimum(large, REL_BUCKETS - 1)
    bucket = np.where(n < exact, n, large)
    assert bucket[-1] == REL_BUCKETS - 1
    return [int(np.argmax(bucket >= b)) for b in range(REL_BUCKETS)]


def _nsa_bias_kernel(rel_ref, bw_ref, pat_ref, *, tq, ncp):
    h = pl.program_id(0)
    thr = _bucket_thresholds()

    def bias_of(d):
        val = jnp.full(d.shape, rel_ref[0, h], F32)
        for b in range(1, REL_BUCKETS):
            val = jnp.where(d >= thr[b], rel_ref[b, h], val)
        return jnp.where(d < 0, NEG, val * LOG2E)

    kk = lax.broadcasted_iota(jnp.int32, (tq, tq), 0)
    qq = lax.broadcasted_iota(jnp.int32, (tq, tq), 1)
    far = rel_ref[REL_BUCKETS - 1, h] * LOG2E
    diag = bias_of(qq - kk)
    near = bias_of(qq - kk + tq)
    bw_ref[0, T_DIAG] = diag
    bw_ref[0, T_NEAR] = near
    bw_ref[0, T_NONE] = jnp.full((tq, tq), NEG, F32)
    bw_ref[0, T_FAR] = jnp.full((tq, tq), far, F32)
    bw_ref[0, T_EDGE] = jnp.where(kk > qq, far, NEG)
    bw_ref[0, T_DIAG_REL] = jnp.where(qq >= kk, diag - far, NEG)
    bw_ref[0, T_NEAR_REL] = near - far
    cc = lax.broadcasted_iota(jnp.int32, (2 * ncp, tq), 0)
    qc = lax.broadcasted_iota(jnp.int32, (2 * ncp, tq), 1)
    pat_ref[0] = bias_of(qc - CMP_STRIDE * (cc - ncp) - (CMP_BLOCK - 1))


def _nsa_bias(rel, *, tq, ncp):
    return pl.pallas_call(
        functools.partial(_nsa_bias_kernel, tq=tq, ncp=ncp),
        out_shape=(jax.ShapeDtypeStruct((N_HEADS, N_TABLES, tq, tq), F32),
                   jax.ShapeDtypeStruct((N_HEADS, 2 * ncp, tq), F32)),
        grid=(N_HEADS,),
        in_specs=[pl.BlockSpec(memory_space=pltpu.SMEM)],
        out_specs=(pl.BlockSpec((1, N_TABLES, tq, tq), lambda h: (h, 0, 0, 0)),
                   pl.BlockSpec((1, 2 * ncp, tq), lambda h: (h, 0, 0))),
        compiler_params=_cparams(("arbitrary",)),
        name="nsa_bias",
    )(rel)


def _nsa_attn_kernel(q_ref, kc_ref, vc_ref, ks_ref, vs_ref, kw_ref, vw_ref, zg_ref, pat_ref,
                     bw_ref, msel_ref, o_ref, pen_ref, sbuf_ref, *, tq, ncp, nb):
    g = pl.program_id(0)
    qi = pl.program_id(2)
    cols = N_HG * tq
    qt = q_ref[0, 0, 0]

    def heads(fn):
        return jnp.concatenate([fn(h) for h in range(N_HG)], axis=1)

    def table(which, ok=None):
        idx = which if ok is None else jnp.where(ok, which, T_NONE)
        return heads(lambda h: bw_ref[h, idx])

    vrows = vs_ref.shape[2]

    def flash(carry, s, vt):
        m, acc = carry
        m_new = jnp.maximum(m.astype(BF16), jnp.max(s, axis=0, keepdims=True))
        p = jnp.exp2(s - m_new)
        m_new = m_new.astype(F32)
        acc = jnp.exp2(m - m_new) * acc + _dot(vt, p)
        return m_new, acc

    init = (jnp.full((1, cols), NEG, F32), jnp.zeros((vrows, cols), F32))

    def finish(carry):
        acc = carry[1]
        return acc[0:HEAD_DIM] / acc[HEAD_DIM:HEAD_DIM + 1]

    s_c = _dot(kc_ref[0, 0], qt)
    start = pl.multiple_of(ncp - qi * (tq // CMP_STRIDE), tq // CMP_STRIDE)
    bias_c = heads(lambda h: pat_ref[h, pl.ds(start, ncp), :])

    nd = WINDOW // tq
    s_parts, v_parts = [], []
    for d in range(nd, -1, -1):
        off = pl.multiple_of(jnp.maximum(qi - d, 0) * tq, tq)
        s = _dot(kw_ref[0, 0, pl.ds(off, tq), :], qt)
        if d == 0:
            s = s + table(T_DIAG)
        elif d == nd:
            s = s + table(T_EDGE, qi >= d)
        elif d == 1:
            s = s + table(T_NEAR, qi >= d)
        else:
            s = s + table(T_FAR, qi >= d)
        s_parts.append(s.astype(BF16))
        v_parts.append(vw_ref[0, 0, :, pl.ds(off, tq)])
    o_w = finish(flash(init, jnp.concatenate(s_parts, axis=0), jnp.concatenate(v_parts, axis=1)))

    sc = s_c + bias_c
    m_c = jnp.max(sc, axis=0, keepdims=True)
    e_c = jnp.exp2(sc - m_c)
    l_c = jnp.sum(e_c, axis=0, keepdims=True)
    p_c = e_c * jnp.where(m_c > 0.5 * NEG, 1.0 / l_c, 0.0)
    o_c = _dot(vc_ref[0, 0], p_c.astype(BF16))

    psum = p_c[:, 0:tq]
    for h in range(1, N_HG):
        psum = psum + p_c[:, h * tq:(h + 1) * tq]
    imp_t = _split_dot(psum, msel_ref[...], 3, mode="ex")
    jj = lax.broadcasted_iota(jnp.int32, (nb, tq), 0)
    tt = qi * tq + lax.broadcasted_iota(jnp.int32, (nb, tq), 1)
    cur = tt // SEL_BLOCK
    forced = (jj == 0) | (jj == cur) | (jj == cur - 1)
    valid_b = jj * SEL_BLOCK <= tt
    imp = jnp.where(valid_b, jnp.where(forced, FORCE_SCORE, imp_t), -jnp.inf)
    groups8 = [imp[r:r + 8, :] for r in range(0, nb, 8)]
    jj8 = lax.broadcasted_iota(jnp.int32, (8, tq), 0)
    cnts = [jnp.zeros((8, tq), F32) for _ in groups8]
    for i in range(nb):
        ri = jnp.broadcast_to(imp[i:i + 1, :], (8, tq))
        for r, grp in enumerate(groups8):
            if 8 * r > i:
                beats = ri >= grp
            elif 8 * r + 7 <= i:
                beats = ri > grp
            else:
                beats = jnp.where(jj8 + 8 * r > i, jnp.where(ri >= grp, 1.0, 0.0), jnp.where(ri > grp, 1.0, 0.0)) > 0.5
            cnts[r] = cnts[r] + jnp.where(beats, 1.0, 0.0)
    cnt = jnp.concatenate(cnts, axis=0)
    sel_t = jnp.where((cnt < float(min(SEL_TOPK, nb))) & valid_b, 0.0, NEG)

    per_tile = tq // SEL_BLOCK
    for t in range(pen_ref.shape[0]):
        pen_ref[t] = jnp.concatenate([sel_t[t * per_tile:(t + 1) * per_tile, :],
                                      jnp.zeros((8 - per_tile, tq), F32)], axis=0)
    blk8 = lax.broadcasted_iota(jnp.int32, (8, tq), 0)

    def tile_pen(t, hi):
        rows = jnp.where(t * per_tile + blk8 >= hi, NEG, pen_ref[t])
        return jnp.concatenate([jnp.broadcast_to(rows[u:u + 1, :], (SEL_BLOCK, tq)) for u in range(per_tile)],
                               axis=0)

    def sel_scores(t0, ntile, hi):
        off = pl.multiple_of(t0 * tq, tq)
        s = _dot(ks_ref[0, 0, pl.ds(off, ntile * tq), :], qt)
        spen = jnp.concatenate([tile_pen(t0 + u, hi) for u in range(ntile)], axis=0)
        return s + heads(lambda h: spen)

    FAR_TILES = FAR_KEYS // tq
    nk = FAR_KEYS
    n_far = jnp.maximum(qi - 1, 0)
    nblk = (n_far + FAR_TILES - 1) // FAR_TILES

    def far_t0(j):
        return jnp.maximum(n_far - FAR_TILES * (j + 1), 0)

    def far_scores(j):
        return sel_scores(far_t0(j), FAR_TILES, (n_far - FAR_TILES * j) * per_tile)

    sbuf_ref[0] = far_scores(0).astype(BF16)
    t1 = jnp.maximum(qi - 1, 0)
    s = jnp.concatenate([(sel_scores(t1, 1, qi * per_tile) + table(T_NEAR_REL)).astype(BF16),
                         (sel_scores(qi, 1, (qi + 1) * per_tile) + table(T_DIAG_REL)).astype(BF16)], axis=0)
    vx = jnp.concatenate([vs_ref[0, 0, :, pl.ds(pl.multiple_of(t1 * tq, tq), tq)],
                          vs_ref[0, 0, :, pl.ds(pl.multiple_of(qi * tq, tq), tq)]], axis=1)
    carry = flash(init, s, vx)

    def far_v(j):
        return vs_ref[0, 0, :, pl.ds(pl.multiple_of(far_t0(j) * tq, tq), nk)]

    def far_body(j, carry):
        slot = j % 2
        sb = sbuf_ref[slot]
        sbuf_ref[1 - slot] = far_scores(j + 1).astype(BF16)
        return flash(carry, sb, far_v(j))

    carry = lax.fori_loop(0, jnp.maximum(nblk - 1, 0), far_body, carry)
    jl = jnp.maximum(nblk - 1, 0)
    carry = lax.cond(nblk > 0, lambda c: flash(c, sbuf_ref[jl % 2], far_v(jl)), lambda c: c, carry)
    o_s = finish(carry)

    zg = pltpu.roll(zg_ref[...].astype(F32), (LANE - 16 * g) % LANE, axis=1)
    gates = _sigmoid(zg).T
    out = (heads(lambda h: gates[h:h + 1, :]) * o_c + heads(lambda h: gates[4 + h:5 + h, :]) * o_s
           + heads(lambda h: gates[8 + h:9 + h, :]) * o_w)
    for hp in range(N_HG // 2):
        pair = jnp.concatenate([out[:, (2 * hp) * tq:(2 * hp + 1) * tq],
                                out[:, (2 * hp + 1) * tq:(2 * hp + 2) * tq]], axis=0)
        o_ref[:, hp * 2 * HEAD_DIM:(hp + 1) * 2 * HEAD_DIM] = pair.T.astype(o_ref.dtype)


def _nsa_attn(qn, kc, vc, ks, vs, kw, vw, z, pat, bw, *, batch, seq, tq, ncp):
    nt = seq // tq
    nb = seq // SEL_BLOCK
    n_cmp = (seq - CMP_BLOCK) // CMP_STRIDE + 1
    ci = np.arange(ncp)[None, :] * CMP_STRIDE
    sj = np.arange(nb)[:, None] * SEL_BLOCK
    msel = ((ci <= sj + SEL_BLOCK - 1) & (ci + CMP_BLOCK - 1 >= sj) & (np.arange(ncp)[None, :] < n_cmp))
    msel_j = jnp.asarray(msel.astype(np.float32), BF16)
    assert nb <= LANE and seq >= FAR_KEYS and FAR_KEYS % tq == 0 and WINDOW // tq >= 2
    ks_spec = pl.BlockSpec((1, 1, seq, ks.shape[3]), lambda g, b, i: (b, g, 0, 0))
    kw_spec = pl.BlockSpec((1, 1, seq, HEAD_DIM), lambda g, b, i: (b, g, 0, 0))
    v_spec = pl.BlockSpec((1, 1, vs.shape[2], seq), lambda g, b, i: (b, g, 0, 0))
    kc_spec = pl.BlockSpec((1, 1, ncp, HEAD_DIM), lambda g, b, i: (b, g, 0, 0))
    vc_spec = pl.BlockSpec((1, 1, HEAD_DIM, ncp), lambda g, b, i: (b, g, 0, 0))
    width = N_HEADS * HEAD_DIM
    return pl.pallas_call(
        functools.partial(_nsa_attn_kernel, tq=tq, ncp=ncp, nb=nb),
        out_shape=jax.ShapeDtypeStruct((batch * seq, width), BF16),
        grid=(N_KV, batch, nt),
        in_specs=[
            pl.BlockSpec((1, 1, 1, HEAD_DIM, N_HG * tq), lambda g, b, i: (b, g, i, 0, 0)),
            kc_spec, vc_spec, ks_spec, v_spec, kw_spec, v_spec,
            pl.BlockSpec((tq, LANE), lambda g, b, i: (b * nt + i, C_GATE // LANE)),
            pl.BlockSpec((N_HG, 2 * ncp, tq), lambda g, b, i: (g, 0, 0)),
            pl.BlockSpec((N_HG, N_TABLES, tq, tq), lambda g, b, i: (g, 0, 0, 0)),
            pl.BlockSpec(msel_j.shape, lambda g, b, i: (0, 0)),
        ],
        out_specs=pl.BlockSpec((tq, N_HG * HEAD_DIM), lambda g, b, i: (b * nt + i, g)),
        scratch_shapes=[pltpu.VMEM((nt, 8, tq), F32), pltpu.VMEM((2, FAR_KEYS, N_HG * tq), BF16)],
        compiler_params=_cparams(("arbitrary", "arbitrary", "arbitrary")),
        name="nsa_attn",
    )(qn, kc, vc, ks, vs, kw, vw, z, pat, bw, msel_j)


def _merge_cross_kernel(x_ref, yr_ref, yn_ref, zm_ref, wr_ref, wn_ref, wm_ref,
                        g_ref, wq_ref, qg_ref, k_ref, v_ref, wo_ref, o_ref):
    d = x_ref.shape[1]
    dh = d // CA_HEADS
    zm = zm_ref[...].astype(F32)
    br = _dot(yr_ref[...], wr_ref[...])
    bn = _dot(yn_ref[...], wn_ref[...])
    merged = _sigmoid(zm[:, 0:d]) * br + _sigmoid(zm[:, d:2 * d]) * bn
    h1 = x_ref[...] + _dot(merged.astype(BF16), wm_ref[...])

    xn = (_rms(h1, NORM_EPS) * g_ref[...]).astype(BF16)
    qf = _dot(xn, wq_ref[...])
    k = k_ref[0]
    v = v_ref[0]
    outs = []
    for h in range(CA_HEADS):
        sl = slice(h * dh, (h + 1) * dh)
        qh = (_rms(qf[:, sl], QK_EPS) * qg_ref[...] * (dh ** -0.5)).astype(BF16)
        s = _dot_nt(qh, k[:, sl])
        s = s - jnp.max(s, axis=-1, keepdims=True)
        e = jnp.exp(s)
        p = e / jnp.sum(e, axis=-1, keepdims=True)
        outs.append(_dot(p.astype(BF16), v[:, sl]))
    o = jnp.concatenate(outs, axis=1).astype(BF16)
    o_ref[...] = h1 + _dot(o, wo_ref[...])


def _merge_cross(x, yr, yn, z, wr, wn, wm, g, wq, qg, kn, vv, wo, *, batch, seq, tm):
    m, d = x.shape
    nt = seq // tm
    nm = kn.shape[1]
    row = lambda c: pl.BlockSpec((tm, c), lambda b, i: (b * nt + i, 0))
    full = lambda a: pl.BlockSpec(a.shape, lambda b, i: (0, 0))
    kv_spec = pl.BlockSpec((1, nm, d), lambda b, i: (b, 0, 0))
    return pl.pallas_call(
        _merge_cross_kernel,
        out_shape=jax.ShapeDtypeStruct((m, d), F32),
        grid=(batch, nt),
        in_specs=[row(d), row(d), row(d),
                  pl.BlockSpec((tm, 2 * d), lambda b, i: (b * nt + i, C_M // (2 * d))),
                  full(wr), full(wn), full(wm), full(g), full(wq), full(qg), kv_spec, kv_spec, full(wo)],
        out_specs=row(d),
        compiler_params=_cparams(("arbitrary", "arbitrary")),
        name="merge_cross",
    )(x, yr, yn, z, wr, wn, wm, g, wq, qg, kn, vv, wo)


def _ca_kv_kernel(mem_ref, g_ref, w_ref, kg_ref, k_ref, v_ref):
    d = mem_ref.shape[2]
    dh = d // CA_HEADS
    mn = (_rms(mem_ref[0], NORM_EPS) * g_ref[...]).astype(BF16)
    kv = _dot(mn, w_ref[...])
    ks = [(_rms(kv[:, h * dh:(h + 1) * dh], QK_EPS) * kg_ref[...]) for h in range(CA_HEADS)]
    k_ref[0] = jnp.concatenate(ks, axis=1).astype(k_ref.dtype)
    v_ref[0] = kv[:, d:2 * d].astype(v_ref.dtype)


def _ca_kv(mem, g, wkv, kg):
    batch, nm, d = mem.shape
    o_shape = jax.ShapeDtypeStruct((batch, nm, d), BF16)
    o_spec = pl.BlockSpec((1, nm, d), lambda b: (b, 0, 0))
    full = lambda a: pl.BlockSpec(a.shape, lambda b: (0, 0))
    return pl.pallas_call(
        _ca_kv_kernel,
        out_shape=(o_shape, o_shape),
        grid=(batch,),
        in_specs=[o_spec, full(g), full(wkv), full(kg)],
        out_specs=(o_spec, o_spec),
        compiler_params=_cparams(("arbitrary",)),
        name="ca_kv",
    )(mem, g, wkv, kg)


def _ffn_kernel(x_ref, g_ref, wup_ref, cw_ref, cb_ref, wd_ref, o_ref, act_ref, carry_ref,
                *, tiles_per_seq, tn):
    i = pl.program_id(0)
    tm = x_ref.shape[0]
    dff = wd_ref.shape[0]

    @pl.when(i % tiles_per_seq == 0)
    def _():
        carry_ref[...] = jnp.zeros_like(carry_ref)

    x = x_ref[...]
    xn = (_rms(x, NORM_EPS) * g_ref[...]).astype(BF16)
    rowi = lax.broadcasted_iota(jnp.int32, (tm, tn), 0)
    for j in range(dff // tn):
        cs = slice(j * tn, (j + 1) * tn)
        a = _dot(xn, wup_ref[:, cs])
        b = _dot(xn, wup_ref[:, dff + j * tn:dff + (j + 1) * tn])
        car = carry_ref[:, cs]
        p1 = jnp.where(rowi == 0, car[7:8, :], pltpu.roll(a, 1, axis=0))
        p2 = jnp.where(rowi == 0, car[6:7, :], jnp.where(rowi == 1, car[7:8, :], pltpu.roll(a, 2, axis=0)))
        carry_ref[:, cs] = a[tm - 8:tm, :]
        conv = cw_ref[0:1, cs] * p2 + cw_ref[1:2, cs] * p1 + cw_ref[2:3, cs] * a + cb_ref[:, cs]
        act_ref[:, cs] = (conv * _sigmoid(conv) * b).astype(BF16)
    o_ref[...] = x + _dot(act_ref[...], wd_ref[...])


def _ffn(h2, g, wup, cw, cb, wd, *, seq, tm, tn):
    m, d = h2.shape
    dff = wd.shape[0]
    cw8 = jnp.zeros((8, dff), F32).at[0:CONV_W, :].set(cw)
    full = lambda a: pl.BlockSpec(a.shape, lambda i: (0, 0))
    return pl.pallas_call(
        functools.partial(_ffn_kernel, tiles_per_seq=seq // tm, tn=tn),
        out_shape=jax.ShapeDtypeStruct((m, d), F32),
        grid=(m // tm,),
        in_specs=[pl.BlockSpec((tm, d), lambda i: (i, 0)), full(g), full(wup), full(cw8),
                  pl.BlockSpec((1, dff), lambda i: (0, 0)), full(wd)],
        out_specs=pl.BlockSpec((tm, d), lambda i: (i, 0)),
        scratch_shapes=[pltpu.VMEM((tm, dff), BF16), pltpu.VMEM((8, dff), F32)],
        compiler_params=_cparams(("arbitrary",)),
        name="ffn",
    )(h2, g, wup, cw8, cb.reshape(1, dff), wd)


def _pack_perm(width):
    o_zw = 3 * width
    o_za = o_zw + R_LORA_W
    o_zg = o_za + R_LORA_A
    o_q = o_zg + R_LORA_G
    nkv = N_KV * HEAD_DIM
    o_kc = o_q + N_HEADS * HEAD_DIM
    o_gate = o_kc + 6 * nkv
    o_m = o_gate + 3 * N_HEADS
    perm = np.full((Z_COLS,), -1, np.int64)
    perm[C_RKV:C_RKV + 3 * width] = np.arange(3 * width)
    perm[C_Q:C_Q + N_HEADS * HEAD_DIM] = o_q + np.arange(N_HEADS * HEAD_DIM)
    perm[C_KC:C_KC + 6 * nkv] = o_kc + np.arange(6 * nkv)
    perm[C_ZG:C_ZG + R_LORA_G] = o_zg + np.arange(R_LORA_G)
    perm[C_ZWA:C_ZWA + R_LORA_W + R_LORA_A] = o_zw + np.arange(R_LORA_W + R_LORA_A)
    for g in range(N_KV):
        for c in range(3):
            for h in range(N_HG):
                perm[C_GATE + 16 * g + 4 * c + h] = o_gate + (g * N_HG + h) * 3 + c
    perm[C_M:C_M + 2 * width] = o_m + np.arange(2 * width)
    return perm, o_m + 2 * width


def _pad_rows(w, rows, offset=0):
    out = jnp.zeros((rows, w.shape[1]), w.dtype)
    return out.at[offset:offset + w.shape[0]].set(w)


def kernel(x, mem, rel_bias, norm_mix, w_in, rwkv_mu, rwkv_w0, rwkv_w2, rwkv_a0, rwkv_a2, rwkv_g2,
           rwkv_kk, rwkv_ka, rwkv_rk, rwkv_lnx_w, rwkv_lnx_b, nsa_q_gain, nsa_k_gain, cmp_pe_k, cmp_pe_v,
           cmp_w1_k, cmp_w2_k, cmp_w1_v, cmp_w2_v, w_branch_rwkv, w_branch_nsa, w_mix_out,
           norm_cross, norm_mem, ca_wq, ca_wkv, ca_q_gain, ca_k_gain, ca_wo,
           norm_ffn, ffn_up, ffn_conv, ffn_conv_b, ffn_down):
    batch, seq, d = x.shape
    depth = w_in.shape[0]
    width = d
    assert d == N_HEADS * HEAD_DIM and 2 * (Z_COLS - C_M) == 4 * d
    perm, in_cols = _pack_perm(width)
    assert w_in.shape[2] == in_cols
    perm_j = jnp.asarray(np.maximum(perm, 0), jnp.int32)
    keep = jnp.asarray(perm >= 0)

    tq = 256
    assert seq % 256 == 0 and tq > REL_MAX_DIST - 1 and WINDOW % tq == 0
    ncp = -(-(seq // CMP_STRIDE) // LANE) * LANE
    n_cmp = (seq - CMP_BLOCK) // CMP_STRIDE + 1
    tm = min(1024, seq)
    row = lambda v: v.reshape(1, -1).astype(F32)
    nkv = N_KV * HEAD_DIM

    h = x.reshape(batch * seq, d)
    for l in range(depth):
        w_in_p = jnp.where(keep[None, :], jnp.take(w_in[l], perm_j, axis=1), 0.0).astype(BF16)
        mu = rwkv_mu[l]
        mu_r = row(mu[0:3 * width])
        mu_wa = row(mu[3 * width:3 * width + R_LORA_W + R_LORA_A])
        mu_g = row(jnp.zeros((256,), F32).at[0:R_LORA_G].set(mu[3 * width + R_LORA_W + R_LORA_A:]))
        w2p = _pad_rows(rwkv_w2[l], LANE, 0).astype(BF16)
        a2p = _pad_rows(rwkv_a2[l], LANE, R_LORA_W).astype(BF16)
        g2p = _pad_rows(rwkv_g2[l], 256, 0).astype(BF16)

        z = _norm_matmul(h, row(norm_mix[l]), w_in_p, tm=tm, tn=4096, out_dtype=BF16, name="in_proj")

        y_r = _rwkv(z, mu_r, mu_g, mu_wa, row(rwkv_w0[l]), w2p, row(rwkv_a0[l]), a2p, g2p,
                    row(rwkv_kk[l]), row(rwkv_ka[l]), row(rwkv_rk[l]), row(rwkv_lnx_w[l]),
                    row(rwkv_lnx_b[l]), batch=batch, seq=seq, width=width)

        qn, ks, vs, kw, vw = _nsa_prep(
            z, row(jnp.tile(nsa_q_gain[l], N_HEADS)), row(jnp.tile(nsa_k_gain[l, 1], N_KV)),
            row(jnp.tile(nsa_k_gain[l, 2], N_KV)), batch=batch, seq=seq, tq=tq)
        z3 = z.reshape(batch, seq // CMP_STRIDE, CMP_STRIDE, Z_COLS)

        def groups(c0):
            t = z3[:, :, :, c0:c0 + nkv].reshape(batch, seq // CMP_STRIDE, CMP_STRIDE, N_KV, HEAD_DIM)
            return t.transpose(0, 3, 1, 2, 4).reshape(batch, N_KV, seq // CMP_STRIDE, CMP_STRIDE * HEAD_DIM)

        kc, vc = _nsa_cmp(groups(C_KC), groups(C_KC + nkv), cmp_pe_k[l].reshape(1, -1), cmp_pe_v[l].reshape(1, -1),
                          cmp_w1_k[l].astype(BF16), cmp_w2_k[l].astype(BF16), cmp_w1_v[l].astype(BF16),
                          cmp_w2_v[l].astype(BF16), row(nsa_k_gain[l, 0]), ncp=ncp, n_cmp=n_cmp)
        bw, pat = _nsa_bias(rel_bias.astype(F32), tq=tq, ncp=ncp)
        y_n = _nsa_attn(qn, kc, vc, ks, vs, kw, vw, z, pat, bw,
                        batch=batch, seq=seq, tq=tq, ncp=ncp)

        kn, vv = _ca_kv(mem, row(norm_mem[l]), ca_wkv[l].astype(BF16), row(ca_k_gain[l]))
        h2 = _merge_cross(h, y_r, y_n, z, w_branch_rwkv[l].astype(BF16), w_branch_nsa[l].astype(BF16),
                          w_mix_out[l].astype(BF16), row(norm_cross[l]), ca_wq[l].astype(BF16),
                          row(ca_q_gain[l]), kn, vv, ca_wo[l].astype(BF16),
                          batch=batch, seq=seq, tm=min(512, seq))

        h = _ffn(h2, row(norm_ffn[l]), ffn_up[l].astype(BF16), ffn_conv[l], ffn_conv_b[l],
                 ffn_down[l].astype(BF16), seq=seq, tm=min(512, seq), tn=256)
    return h.reshape(batch, seq, d)
```

```python
import functools
import math

import jax
import jax.numpy as jnp
import numpy as np
from jax import lax
from jax.experimental import pallas as pl
from jax.experimental.pallas import tpu as pltpu

F32 = jnp.float32
BF16 = jnp.bfloat16
NEG = -1e30
LOG2E = 1.4426950408889634
FAR_KEYS = 512
T_DIAG, T_NEAR, T_NONE, T_FAR, T_EDGE, T_DIAG_REL, T_NEAR_REL = range(7)
N_TABLES = 7
V_ROWS = 80

R_HEAD = 64
R_LORA_W = 64
R_LORA_A = 64
R_LORA_G = 160
LNX_EPS = 64e-5
N_HEADS = 16
N_KV = 4
N_HG = N_HEADS // N_KV
HEAD_DIM = 64
CMP_BLOCK = 32
CMP_STRIDE = 16
CMP_HIDDEN = 256
SEL_BLOCK = 64
SEL_TOPK = 16
WINDOW = 512
FORCE_SCORE = 1e4
REL_BUCKETS = 32
REL_MAX_DIST = 128
CA_HEADS = 4
CONV_W = 3
NORM_EPS = 1e-6
QK_EPS = 1e-6

LANE = 128
VMEM_LIMIT = 56 * 1024 * 1024

C_RKV = 0
C_Q = 3072
C_KC = 4096
C_ZG = 5632
C_ZWA = 5888
C_GATE = 6016
C_M = 6144
Z_COLS = 8192


def _dot(a, b):
    return jnp.dot(a, b, preferred_element_type=F32)


def _dot_nt(a, b):
    return lax.dot_general(a, b, (((1,), (1,)), ((), ())), preferred_element_type=F32)


def _split_dot(x, e, parts, mode="xe"):
    acc = None
    rem = x
    for i in range(parts):
        hi = rem.astype(BF16)
        t = _dot(hi, e) if mode == "xe" else (_dot(e, hi) if mode == "ex" else _dot_nt(e, hi))
        acc = t if acc is None else acc + t
        if i + 1 < parts:
            rem = rem - hi.astype(F32)
    return acc


def _rms(x, eps):
    return x * lax.rsqrt(jnp.mean(x * x, axis=-1, keepdims=True) + eps)


def _sigmoid(x):
    return 1.0 / (1.0 + jnp.exp(-x))


def _cparams(sem):
    return pltpu.CompilerParams(dimension_semantics=sem, vmem_limit_bytes=VMEM_LIMIT)


def _norm_matmul_kernel(x_ref, g_ref, w_ref, o_ref, xn_ref):
    @pl.when(pl.program_id(1) == 0)
    def _():
        x = x_ref[...]
        xn_ref[...] = (_rms(x, NORM_EPS) * g_ref[...]).astype(BF16)

    o_ref[...] = _dot(xn_ref[...], w_ref[...]).astype(o_ref.dtype)


def _norm_matmul(x, g, w, *, tm, tn, out_dtype, name):
    m, k = x.shape
    n = w.shape[1]
    return pl.pallas_call(
        _norm_matmul_kernel,
        out_shape=jax.ShapeDtypeStruct((m, n), out_dtype),
        grid=(m // tm, n // tn),
        in_specs=[
            pl.BlockSpec((tm, k), lambda i, j: (i, 0)),
            pl.BlockSpec((1, k), lambda i, j: (0, 0)),
            pl.BlockSpec((k, tn), lambda i, j: (0, j)),
        ],
        out_specs=pl.BlockSpec((tm, tn), lambda i, j: (i, j)),
        scratch_shapes=[pltpu.VMEM((tm, k), BF16)],
        compiler_params=_cparams(("arbitrary", "arbitrary")),
        name=name,
    )(x, g, w)


def _rwkv_kernel(zr_ref, zg_ref, zwa_ref, mur_ref, mug_ref, muwa_ref, w0_ref, w2_ref, a0_ref, a2_ref,
                 g2_ref, kk_ref, ka_ref, rk_ref, lnw_ref, lnb_ref, eh_ref, eht_ref, tri_ref,
                 o_ref, pr_ref, pg_ref, pwa_ref, st_ref, *, L, width):
    i = pl.program_id(1)
    npair = width // LANE
    tb = o_ref.shape[0]

    @pl.when(i == 0)
    def _():
        pr_ref[...] = jnp.zeros_like(pr_ref)
        pg_ref[...] = jnp.zeros_like(pg_ref)
        pwa_ref[...] = jnp.zeros_like(pwa_ref)
        st_ref[...] = jnp.zeros_like(st_ref)

    def shifted8(z, prev_ref, mu):
        rolled = pltpu.roll(z, 1, axis=0)
        row0 = lax.broadcasted_iota(jnp.int32, z.shape, 0) == 0
        prev = jnp.where(row0, prev_ref[7:8, :], rolled)
        prev_ref[...] = z[tb - 8:tb, :]
        return z + (prev - z) * mu

    zs = shifted8(zr_ref[...].astype(F32), pr_ref, mur_ref[...])
    zsg = shifted8(zg_ref[...].astype(F32), pg_ref, mug_ref[...])
    zswa = shifted8(zwa_ref[...].astype(F32), pwa_ref, muwa_ref[...])

    r = zs[:, 0:width]
    k = zs[:, width:2 * width]
    v = zs[:, 2 * width:3 * width]

    w_lin = w0_ref[...] + _dot(jnp.tanh(zswa).astype(BF16), w2_ref[...])
    a_lin = a0_ref[...] + _dot(zswa.astype(BF16), a2_ref[...])
    y = -w_lin
    softplus = jnp.maximum(y, 0.0) + jnp.log(1.0 + jnp.exp(-jnp.abs(y)))
    ld = -jnp.exp(-softplus - 0.5)
    a = _sigmoid(a_lin)
    g = _dot(_sigmoid(zsg).astype(BF16), g2_ref[...])

    eh = eh_ref[...]
    eht = eht_ref[...]

    def sum_heads(t):
        return _split_dot(t, eh, 2)

    def bcast_heads(t):
        return _split_dot(t, eht, 2)

    kkr = k * kk_ref[...]
    nrm = jnp.maximum(jnp.sqrt(sum_heads(kkr * kkr)), 1e-12)
    kkn = kkr * bcast_heads(1.0 / nrm)
    k2 = k * (1.0 + (a - 1.0) * ka_ref[...])
    av = -kkn
    bv = kkn * a
    bonus = bcast_heads(sum_heads(r * k2 * rk_ref[...])) * v

    nch = tb // L
    lg = _split_dot(ld, tri_ref[...], 3, mode="ex")
    lasts = [lg[(c + 1) * L - 1:(c + 1) * L, :] for c in range(nch)]
    lg_last = jnp.concatenate([jnp.broadcast_to(t, (L, width)) for t in lasts], axis=0)
    eg = jnp.exp(lg)
    eng = jnp.exp(-lg)
    egl = jnp.exp(lg_last - lg)
    rt = r * eg
    kt = k2 * eng
    bt = bv * eng
    at = av * jnp.exp(lg - ld)
    kgl = k2 * egl
    bgl = bv * egl
    gl = [jnp.exp(t) for t in lasts]

    lane = lax.broadcasted_iota(jnp.int32, (L, LANE), 1)
    lo = lane < R_HEAD

    def stack(t):
        return jnp.concatenate([jnp.where(lo, t, 0.0), jnp.where(lo, 0.0, t)], axis=0)

    row = lax.broadcasted_iota(jnp.int32, (2 * L, 2 * L), 0)
    col = lax.broadcasted_iota(jnp.int32, (2 * L, 2 * L), 1)
    strict = row > col
    incl = row >= col
    eye = (row == col).astype(F32)
    nsq = int(math.log2(L)) - 1

    pairs = range(npair)
    chains = [(c, p) for c in range(nch) for p in pairs]
    sub = lambda t, c, p: t[c * L:(c + 1) * L, p * LANE:(p + 1) * LANE]
    bf = lambda t: t.astype(BF16)
    ar = {cp: bf(jnp.concatenate([stack(sub(at, *cp)), stack(sub(rt, *cp))], axis=0)) for cp in chains}
    bk = {cp: bf(jnp.concatenate([stack(sub(bt, *cp)), stack(sub(kt, *cp))], axis=0)) for cp in chains}
    v_s = {cp: bf(stack(sub(v, *cp))) for cp in chains}
    gm = {cp: _dot_nt(ar[cp], bk[cp]) for cp in chains}
    a_ab = {cp: jnp.where(strict, gm[cp][0:2 * L, 0:2 * L], 0.0) for cp in chains}
    a_ak = {cp: bf(jnp.where(strict, gm[cp][0:2 * L, 2 * L:4 * L], 0.0)) for cp in chains}
    a_rb = {cp: bf(jnp.where(incl, gm[cp][2 * L:4 * L, 0:2 * L], 0.0)) for cp in chains}
    a_rk = {cp: bf(jnp.where(incl, gm[cp][2 * L:4 * L, 2 * L:4 * L], 0.0)) for cp in chains}
    tinv = {cp: eye + a_ab[cp] for cp in chains}
    pw = a_ab
    for _ in range(nsq):
        pwb = {cp: bf(pw[cp]) for cp in chains}
        pw = {cp: _dot(pwb[cp], pwb[cp]) for cp in chains}
        tinv = {cp: tinv[cp] + _dot(bf(pw[cp]), bf(tinv[cp])) for cp in chains}
    tinv = {cp: bf(tinv[cp]) for cp in chains}
    bkgt = {cp: bf(jnp.concatenate([stack(sub(bgl, *cp)), stack(sub(kgl, *cp))], axis=0).T) for cp in chains}
    glcol = {(c, p): jnp.sum(eye * gl[c][:, p * LANE:(p + 1) * LANE], axis=1, keepdims=True)
             for (c, p) in chains}
    h = [st_ref[p] for p in pairs]
    ys = []
    for c in range(nch):
        hb = [bf(h[p]) for p in pairs]
        wmat = [_dot(jnp.concatenate([ar[c, p][0:2 * L], a_ak[c, p]], axis=1),
                     jnp.concatenate([hb[p], v_s[c, p]], axis=0)) for p in pairs]
        ub = [bf(_dot(tinv[c, p], bf(wmat[p]))) for p in pairs]
        yy = [_dot(jnp.concatenate([ar[c, p][2 * L:4 * L], a_rb[c, p], a_rk[c, p]], axis=1),
                   jnp.concatenate([hb[p], ub[p], v_s[c, p]], axis=0)) for p in pairs]
        h = [h[p] * glcol[c, p] + _dot(bkgt[c, p], jnp.concatenate([ub[p], v_s[c, p]], axis=0))
             for p in pairs]
        ys.append(jnp.concatenate([yy[p][0:L] + yy[p][L:2 * L] for p in pairs], axis=1))
    for p in pairs:
        st_ref[p] = h[p]

    yv = jnp.concatenate(ys, axis=0)
    inv_n = 1.0 / R_HEAD
    mean = bcast_heads(sum_heads(yv) * inv_n)
    yc = yv - mean
    var = bcast_heads(sum_heads(yc * yc) * inv_n)
    yn = yc * lax.rsqrt(var + LNX_EPS) * lnw_ref[...] + lnb_ref[...]
    o_ref[...] = ((yn + bonus) * g).astype(o_ref.dtype)


def _rwkv(z, mu_r, mu_g, mu_wa, w0, w2p, a0, a2p, g2p, kk, ka, rk, lnw, lnb, *, batch, seq, width):
    L = 64
    tb = 4 * L
    nt = seq // tb
    nheads = width // R_HEAD
    eh = np.zeros((width, LANE), np.float32)
    eh[np.arange(width), np.arange(width) // R_HEAD] = 1.0
    ti = np.arange(tb)
    tri = ((ti[:, None] >= ti[None, :]) & (ti[:, None] // L == ti[None, :] // L)).astype(np.float32)
    vec = lambda c: pl.BlockSpec((1, c), lambda b, i: (0, 0))
    full = lambda a: pl.BlockSpec(a.shape, lambda b, i: (0,) * a.ndim)
    eh_j = jnp.asarray(eh, BF16)
    eht_j = jnp.asarray(eh.T, BF16)
    tri_j = jnp.asarray(tri, BF16)
    return pl.pallas_call(
        functools.partial(_rwkv_kernel, L=L, width=width),
        out_shape=jax.ShapeDtypeStruct((batch * seq, width), BF16),
        grid=(batch, nt),
        in_specs=[
            pl.BlockSpec((tb, 3 * width), lambda b, i: (b * nt + i, C_RKV // (3 * width))),
            pl.BlockSpec((tb, 256), lambda b, i: (b * nt + i, C_ZG // 256)),
            pl.BlockSpec((tb, LANE), lambda b, i: (b * nt + i, C_ZWA // LANE)),
            vec(3 * width), vec(256), vec(LANE),
            vec(width), full(w2p), vec(width), full(a2p), full(g2p),
            vec(width), vec(width), vec(width), vec(width), vec(width),
            full(eh_j), full(eht_j), full(tri_j),
        ],
        out_specs=pl.BlockSpec((tb, width), lambda b, i: (b * nt + i, 0)),
        scratch_shapes=[
            pltpu.VMEM((8, 3 * width), F32),
            pltpu.VMEM((8, 256), F32),
            pltpu.VMEM((8, LANE), F32),
            pltpu.VMEM((nheads // 2, 2 * R_HEAD, 2 * R_HEAD), F32),
        ],
        compiler_params=_cparams(("arbitrary", "arbitrary")),
        name="rwkv",
    )(z, z, z, mu_r, mu_g, mu_wa, w0, w2p, a0, a2p, g2p, kk, ka, rk, lnw, lnb, eh_j, eht_j, tri_j)


def _nsa_prep_kernel(zq_ref, zs_ref, zw_ref, qg_ref, kgs_ref, kgw_ref, ehq_ref, ehqt_ref,
                     q_ref, ks_ref, vs_ref, kw_ref, vw_ref):
    inv_d = 1.0 / HEAD_DIM

    def head_norm(x, gain, e, et):
        ss = _split_dot(x * x, e, 2)
        inv = lax.rsqrt(ss * inv_d + QK_EPS)
        return x * _split_dot(inv, et, 2) * gain

    nk = N_KV * HEAD_DIM
    ehq = ehq_ref[...]
    ehqt = ehqt_ref[...]
    q = head_norm(zq_ref[...].astype(F32), qg_ref[...], ehq, ehqt) * (HEAD_DIM ** -0.5 * LOG2E)
    qt = q.T
    for gq in range(N_KV):
        rows = [qt[(gq * N_HG + h) * HEAD_DIM:(gq * N_HG + h + 1) * HEAD_DIM, :] for h in range(N_HG)]
        q_ref[0, gq, 0] = jnp.concatenate(rows, axis=1).astype(q_ref.dtype)
    zs = zs_ref[...].astype(F32)
    zw = zw_ref[...].astype(F32)
    ehk = ehq[0:nk, :]
    ehkt = ehqt[:, 0:nk]
    ksn = head_norm(zs[:, 0:nk], kgs_ref[...], ehk, ehkt)
    kwn = head_norm(zw[:, 0:nk], kgw_ref[...], ehk, ehkt)
    vst = zs[:, nk:2 * nk].T
    vwt = zw[:, nk:2 * nk].T
    tb = zs.shape[0]
    vextra = jnp.where(lax.broadcasted_iota(jnp.int32, (V_ROWS - HEAD_DIM, tb), 0) == 0, 1.0, 0.0)
    for gq in range(N_KV):
        sl = slice(gq * HEAD_DIM, (gq + 1) * HEAD_DIM)
        ks_ref[0, gq] = ksn[:, sl].astype(ks_ref.dtype)
        vs_ref[0, gq] = jnp.concatenate([vst[sl, :], vextra], axis=0).astype(vs_ref.dtype)
        kw_ref[0, gq] = kwn[:, sl].astype(kw_ref.dtype)
        vw_ref[0, gq] = jnp.concatenate([vwt[sl, :], vextra], axis=0).astype(vw_ref.dtype)


def _nsa_prep(z, q_gain_t, kgs_t, kgw_t, *, batch, seq, tq):
    tb = tq
    nt = seq // tb
    nq = N_HEADS * HEAD_DIM
    nk = N_KV * HEAD_DIM
    eh = np.zeros((nq, LANE), np.float32)
    eh[np.arange(nq), np.arange(nq) // HEAD_DIM] = 1.0
    eh_j = jnp.asarray(eh, BF16)
    eht_j = jnp.asarray(eh.T, BF16)
    k_shape = jax.ShapeDtypeStruct((batch, N_KV, seq, HEAD_DIM), BF16)
    k_spec = pl.BlockSpec((1, N_KV, tb, HEAD_DIM), lambda b, i: (b, 0, i, 0))
    v_shape = jax.ShapeDtypeStruct((batch, N_KV, V_ROWS, seq), BF16)
    v_spec = pl.BlockSpec((1, N_KV, V_ROWS, tb), lambda b, i: (b, 0, 0, i))
    return pl.pallas_call(
        _nsa_prep_kernel,
        out_shape=(jax.ShapeDtypeStruct((batch, N_KV, nt, HEAD_DIM, N_HG * tq), BF16),
                   k_shape, v_shape, k_shape, v_shape),
        grid=(batch, nt),
        in_specs=[
            pl.BlockSpec((tb, nq), lambda b, i: (b * nt + i, C_Q // nq)),
            pl.BlockSpec((tb, 2 * nk), lambda b, i: (b * nt + i, C_KC // (2 * nk) + 1)),
            pl.BlockSpec((tb, 2 * nk), lambda b, i: (b * nt + i, C_KC // (2 * nk) + 2)),
            pl.BlockSpec((1, nq), lambda b, i: (0, 0)),
            pl.BlockSpec((1, nk), lambda b, i: (0, 0)),
            pl.BlockSpec((1, nk), lambda b, i: (0, 0)),
            pl.BlockSpec(eh_j.shape, lambda b, i: (0, 0)),
            pl.BlockSpec(eht_j.shape, lambda b, i: (0, 0)),
        ],
        out_specs=(pl.BlockSpec((1, N_KV, 1, HEAD_DIM, N_HG * tq), lambda b, i: (b, 0, i, 0, 0)),
                   k_spec, v_spec, k_spec, v_spec),
        compiler_params=_cparams(("arbitrary", "arbitrary")),
        name="nsa_prep",
    )(z, z, z, q_gain_t, kgs_t, kgw_t, eh_j, eht_j)


def _nsa_cmp_kernel(ak_ref, av_ref, pek_ref, pev_ref, w1k_ref, w2k_ref, w1v_ref, w2v_ref, kg_ref,
                    kc_ref, vc_ref, *, n_cmp):
    ncp = kc_ref.shape[2]
    half = w1k_ref.shape[0] // 2

    def compress(a_ref, pe_ref, w1_ref, w2_ref):
        rows = a_ref.shape[2]
        a = a_ref[0, 0].astype(BF16)
        p1 = _dot(a, w1_ref[0:half, :])
        p2 = _dot(a, w1_ref[half:2 * half, :])
        pe8 = jnp.broadcast_to(pe_ref[...], (8, pe_ref.shape[1])).astype(BF16)
        pe_bias = _dot(pe8, w1_ref[...])[0:1, :]
        hpre = p1 + pltpu.roll(p2, rows - 1, axis=0) + pe_bias
        hid = 0.5 * hpre * (1.0 + jnp.tanh(math.sqrt(2.0 / math.pi) * (hpre + 0.044715 * hpre * hpre * hpre)))
        out = _dot(hid.astype(BF16), w2_ref[...])
        if rows < ncp:
            out = jnp.concatenate([out, jnp.zeros((ncp - rows, out.shape[1]), F32)], axis=0)
        return out

    valid = lax.broadcasted_iota(jnp.int32, (ncp, HEAD_DIM), 0) < n_cmp
    kc = compress(ak_ref, pek_ref, w1k_ref, w2k_ref)
    kc = _rms(kc, QK_EPS) * kg_ref[...]
    kc_ref[0, 0] = jnp.where(valid, kc, 0.0).astype(kc_ref.dtype)
    vc = jnp.where(valid, compress(av_ref, pev_ref, w1v_ref, w2v_ref), 0.0)
    vct = jnp.concatenate([vc, jnp.zeros((ncp, LANE - HEAD_DIM), F32)], axis=1).T
    vc_ref[0, 0] = vct[0:HEAD_DIM, :].astype(vc_ref.dtype)


def _nsa_cmp(ak, av, pek, pev, w1k, w2k, w1v, w2v, kg, *, ncp, n_cmp):
    batch, ng, rows, wid = ak.shape
    a_spec = pl.BlockSpec((1, 1, rows, wid), lambda b, g: (b, g, 0, 0))
    full = lambda a: pl.BlockSpec(a.shape, lambda b, g: (0,) * a.ndim)
    k_shape = jax.ShapeDtypeStruct((batch, ng, ncp, HEAD_DIM), BF16)
    k_spec = pl.BlockSpec((1, 1, ncp, HEAD_DIM), lambda b, g: (b, g, 0, 0))
    v_shape = jax.ShapeDtypeStruct((batch, ng, HEAD_DIM, ncp), BF16)
    v_spec = pl.BlockSpec((1, 1, HEAD_DIM, ncp), lambda b, g: (b, g, 0, 0))
    return pl.pallas_call(
        functools.partial(_nsa_cmp_kernel, n_cmp=n_cmp),
        out_shape=(k_shape, v_shape),
        grid=(batch, ng),
        in_specs=[a_spec, a_spec, full(pek), full(pev), full(w1k), full(w2k), full(w1v), full(w2v), full(kg)],
        out_specs=(k_spec, v_spec),
        compiler_params=_cparams(("arbitrary", "arbitrary")),
        name="nsa_cmp",
    )(ak, av, pek, pev, w1k, w2k, w1v, w2v, kg)


def _bucket_thresholds():
    exact = REL_BUCKETS // 2
    n = np.arange(0, REL_MAX_DIST + 1)
    nf = np.maximum(n, 1).astype(np.float64)
    large = exact + (np.log(nf / exact) / math.log(REL_MAX_DIST / exact) * (REL_BUCKETS - exact)).astype(np.int64)
    large = np.minimum(large, REL_BUCKETS - 1)
    bucket = np.where(n < exact, n, large)
    assert bucket[-1] == REL_BUCKETS - 1
    return [int(np.argmax(bucket >= b)) for b in range(REL_BUCKETS)]


def _nsa_bias_kernel(rel_ref, bw_ref, pat_ref, *, tq, ncp):
    h = pl.program_id(0)
    thr = _bucket_thresholds()

    def bias_of(d):
        val = jnp.full(d.shape, rel_ref[0, h], F32)
        for b in range(1, REL_BUCKETS):
            val = jnp.where(d >= thr[b], rel_ref[b, h], val)
        return jnp.where(d < 0, NEG, val * LOG2E)

    kk = lax.broadcasted_iota(jnp.int32, (tq, tq), 0)
    qq = lax.broadcasted_iota(jnp.int32, (tq, tq), 1)
    far = rel_ref[REL_BUCKETS - 1, h] * LOG2E
    diag = bias_of(qq - kk)
    near = bias_of(qq - kk + tq)
    bw_ref[0, T_DIAG] = diag
    bw_ref[0, T_NEAR] = near
    bw_ref[0, T_NONE] = jnp.full((tq, tq), NEG, F32)
    bw_ref[0, T_FAR] = jnp.full((tq, tq), far, F32)
    bw_ref[0, T_EDGE] = jnp.where(kk > qq, far, NEG)
    bw_ref[0, T_DIAG_REL] = jnp.where(qq >= kk, diag - far, NEG)
    bw_ref[0, T_NEAR_REL] = near - far
    cc = lax.broadcasted_iota(jnp.int32, (2 * ncp, tq), 0)
    qc = lax.broadcasted_iota(jnp.int32, (2 * ncp, tq), 1)
    pat_ref[0] = bias_of(qc - CMP_STRIDE * (cc - ncp) - (CMP_BLOCK - 1))


def _nsa_bias(rel, *, tq, ncp):
    return pl.pallas_call(
        functools.partial(_nsa_bias_kernel, tq=tq, ncp=ncp),
        out_shape=(jax.ShapeDtypeStruct((N_HEADS, N_TABLES, tq, tq), F32),
                   jax.ShapeDtypeStruct((N_HEADS, 2 * ncp, tq), F32)),
        grid=(N_HEADS,),
        in_specs=[pl.BlockSpec(memory_space=pltpu.SMEM)],
        out_specs=(pl.BlockSpec((1, N_TABLES, tq, tq), lambda h: (h, 0, 0, 0)),
                   pl.BlockSpec((1, 2 * ncp, tq), lambda h: (h, 0, 0))),
        compiler_params=_cparams(("arbitrary",)),
        name="nsa_bias",
    )(rel)


def _nsa_attn_kernel(q_ref, kc_ref, vc_ref, ks_ref, vs_ref, kw_ref, vw_ref, zg_ref, pat_ref,
                     bw_ref, msel_ref, o_ref, pen_ref, sbuf_ref, *, tq, ncp, nb):
    g = pl.program_id(0)
    qi = pl.program_id(2)
    cols = N_HG * tq
    qt = q_ref[0, 0, 0]

    def heads(fn):
        return jnp.concatenate([fn(h) for h in range(N_HG)], axis=1)

    def table(which, ok=None):
        idx = which if ok is None else jnp.where(ok, which, T_NONE)
        return heads(lambda h: bw_ref[h, idx])

    vrows = vs_ref.shape[2]

    def flash(carry, s, vt):
        m, acc = carry
        m_new = jnp.maximum(m.astype(BF16), jnp.max(s, axis=0, keepdims=True))
        p = jnp.exp2(s - m_new)
        m_new = m_new.astype(F32)
        acc = jnp.exp2(m - m_new) * acc + _dot(vt, p)
        return m_new, acc

    init = (jnp.full((1, cols), NEG, F32), jnp.zeros((vrows, cols), F32))

    def finish(carry):
        acc = carry[1]
        return acc[0:HEAD_DIM] / acc[HEAD_DIM:HEAD_DIM + 1]

    s_c = _dot(kc_ref[0, 0], qt)
    start = pl.multiple_of(ncp - qi * (tq // CMP_STRIDE), tq // CMP_STRIDE)
    bias_c = heads(lambda h: pat_ref[h, pl.ds(start, ncp), :])

    nd = WINDOW // tq
    s_parts, v_parts = [], []
    for d in range(nd, -1, -1):
        off = pl.multiple_of(jnp.maximum(qi - d, 0) * tq, tq)
        s = _dot(kw_ref[0, 0, pl.ds(off, tq), :], qt)
        if d == 0:
            s = s + table(T_DIAG)
        elif d == nd:
            s = s + table(T_EDGE, qi >= d)
        elif d == 1:
            s = s + table(T_NEAR, qi >= d)
        else:
            s = s + table(T_FAR, qi >= d)
        s_parts.append(s.astype(BF16))
        v_parts.append(vw_ref[0, 0, :, pl.ds(off, tq)])
    o_w = finish(flash(init, jnp.concatenate(s_parts, axis=0), jnp.concatenate(v_parts, axis=1)))

    sc = s_c + bias_c
    m_c = jnp.max(sc, axis=0, keepdims=True)
    e_c = jnp.exp2(sc - m_c)
    l_c = jnp.sum(e_c, axis=0, keepdims=True)
    p_c = e_c * jnp.where(m_c > 0.5 * NEG, 1.0 / l_c, 0.0)
    o_c = _dot(vc_ref[0, 0], p_c.astype(BF16))

    psum = p_c[:, 0:tq]
    for h in range(1, N_HG):
        psum = psum + p_c[:, h * tq:(h + 1) * tq]
    imp_t = _split_dot(psum, msel_ref[...], 3, mode="ex")
    jj = lax.broadcasted_iota(jnp.int32, (nb, tq), 0)
    tt = qi * tq + lax.broadcasted_iota(jnp.int32, (nb, tq), 1)
    cur = tt // SEL_BLOCK
    forced = (jj == 0) | (jj == cur) | (jj == cur - 1)
    valid_b = jj * SEL_BLOCK <= tt
    imp = jnp.where(valid_b, jnp.where(forced, FORCE_SCORE, imp_t), -jnp.inf)
    groups8 = [imp[r:r + 8, :] for r in range(0, nb, 8)]
    jj8 = lax.broadcasted_iota(jnp.int32, (8, tq), 0)
    cnts = [jnp.zeros((8, tq), F32) for _ in groups8]
    for i in range(nb):
        ri = jnp.broadcast_to(imp[i:i + 1, :], (8, tq))
        for r, grp in enumerate(groups8):
            if 8 * r > i:
                beats = ri >= grp
            elif 8 * r + 7 <= i:
                beats = ri > grp
            else:
                beats = jnp.where(jj8 + 8 * r > i, jnp.where(ri >= grp, 1.0, 0.0), jnp.where(ri > grp, 1.0, 0.0)) > 0.5
            cnts[r] = cnts[r] + jnp.where(beats, 1.0, 0.0)
    cnt = jnp.concatenate(cnts, axis=0)
    sel_t = jnp.where((cnt < float(min(SEL_TOPK, nb))) & valid_b, 0.0, NEG)

    per_tile = tq // SEL_BLOCK
    for t in range(pen_ref.shape[0]):
        pen_ref[t] = jnp.concatenate([sel_t[t * per_tile:(t + 1) * per_tile, :],
                                      jnp.zeros((8 - per_tile, tq), F32)], axis=0)
    blk8 = lax.broadcasted_iota(jnp.int32, (8, tq), 0)

    def tile_pen(t, hi):
        rows = jnp.where(t * per_tile + blk8 >= hi, NEG, pen_ref[t])
        return jnp.concatenate([jnp.broadcast_to(rows[u:u + 1, :], (SEL_BLOCK, tq)) for u in range(per_tile)],
                               axis=0)

    def sel_scores(t0, ntile, hi):
        off = pl.multiple_of(t0 * tq, tq)
        s = _dot(ks_ref[0, 0, pl.ds(off, ntile * tq), :], qt)
        spen = jnp.concatenate([tile_pen(t0 + u, hi) for u in range(ntile)], axis=0)
        return s + heads(lambda h: spen)

    FAR_TILES = FAR_KEYS // tq
    nk = FAR_KEYS
    n_far = jnp.maximum(qi - 1, 0)
    nblk = (n_far + FAR_TILES - 1) // FAR_TILES

    def far_t0(j):
        return jnp.maximum(n_far - FAR_TILES * (j + 1), 0)

    def far_scores(j):
        return sel_scores(far_t0(j), FAR_TILES, (n_far - FAR_TILES * j) * per_tile)

    sbuf_ref[0] = far_scores(0).astype(BF16)
    t1 = jnp.maximum(qi - 1, 0)
    s = jnp.concatenate([(sel_scores(t1, 1, qi * per_tile) + table(T_NEAR_REL)).astype(BF16),
                         (sel_scores(qi, 1, (qi + 1) * per_tile) + table(T_DIAG_REL)).astype(BF16)], axis=0)
    vx = jnp.concatenate([vs_ref[0, 0, :, pl.ds(pl.multiple_of(t1 * tq, tq), tq)],
                          vs_ref[0, 0, :, pl.ds(pl.multiple_of(qi * tq, tq), tq)]], axis=1)
    carry = flash(init, s, vx)

    def far_v(j):
        return vs_ref[0, 0, :, pl.ds(pl.multiple_of(far_t0(j) * tq, tq), nk)]

    def far_body(j, carry):
        slot = j % 2
        sb = sbuf_ref[slot]
        sbuf_ref[1 - slot] = far_scores(j + 1).astype(BF16)
        return flash(carry, sb, far_v(j))

    carry = lax.fori_loop(0, jnp.maximum(nblk - 1, 0), far_body, carry)
    jl = jnp.maximum(nblk - 1, 0)
    carry = lax.cond(nblk > 0, lambda c: flash(c, sbuf_ref[jl % 2], far_v(jl)), lambda c: c, carry)
    o_s = finish(carry)

    zg = pltpu.roll(zg_ref[...].astype(F32), (LANE - 16 * g) % LANE, axis=1)
    gates = _sigmoid(zg).T
    out = (heads(lambda h: gates[h:h + 1, :]) * o_c + heads(lambda h: gates[4 + h:5 + h, :]) * o_s
           + heads(lambda h: gates[8 + h:9 + h, :]) * o_w)
    for hp in range(N_HG // 2):
        pair = jnp.concatenate([out[:, (2 * hp) * tq:(2 * hp + 1) * tq],
                                out[:, (2 * hp + 1) * tq:(2 * hp + 2) * tq]], axis=0)
        o_ref[:, hp * 2 * HEAD_DIM:(hp + 1) * 2 * HEAD_DIM] = pair.T.astype(o_ref.dtype)


def _nsa_attn(qn, kc, vc, ks, vs, kw, vw, z, pat, bw, *, batch, seq, tq, ncp):
    nt = seq // tq
    nb = seq // SEL_BLOCK
    n_cmp = (seq - CMP_BLOCK) // CMP_STRIDE + 1
    ci = np.arange(ncp)[None, :] * CMP_STRIDE
    sj = np.arange(nb)[:, None] * SEL_BLOCK
    msel = ((ci <= sj + SEL_BLOCK - 1) & (ci + CMP_BLOCK - 1 >= sj) & (np.arange(ncp)[None, :] < n_cmp))
    msel_j = jnp.asarray(msel.astype(np.float32), BF16)
    assert nb <= LANE and seq >= FAR_KEYS and FAR_KEYS % tq == 0 and WINDOW // tq >= 2
    ks_spec = pl.BlockSpec((1, 1, seq, ks.shape[3]), lambda g, b, i: (b, g, 0, 0))
    kw_spec = pl.BlockSpec((1, 1, seq, HEAD_DIM), lambda g, b, i: (b, g, 0, 0))
    v_spec = pl.BlockSpec((1, 1, vs.shape[2], seq), lambda g, b, i: (b, g, 0, 0))
    kc_spec = pl.BlockSpec((1, 1, ncp, HEAD_DIM), lambda g, b, i: (b, g, 0, 0))
    vc_spec = pl.BlockSpec((1, 1, HEAD_DIM, ncp), lambda g, b, i: (b, g, 0, 0))
    width = N_HEADS * HEAD_DIM
    return pl.pallas_call(
        functools.partial(_nsa_attn_kernel, tq=tq, ncp=ncp, nb=nb),
        out_shape=jax.ShapeDtypeStruct((batch * seq, width), BF16),
        grid=(N_KV, batch, nt),
        in_specs=[
            pl.BlockSpec((1, 1, 1, HEAD_DIM, N_HG * tq), lambda g, b, i: (b, g, i, 0, 0)),
            kc_spec, vc_spec, ks_spec, v_spec, kw_spec, v_spec,
            pl.BlockSpec((tq, LANE), lambda g, b, i: (b * nt + i, C_GATE // LANE)),
            pl.BlockSpec((N_HG, 2 * ncp, tq), lambda g, b, i: (g, 0, 0)),
            pl.BlockSpec((N_HG, N_TABLES, tq, tq), lambda g, b, i: (g, 0, 0, 0)),
            pl.BlockSpec(msel_j.shape, lambda g, b, i: (0, 0)),
        ],
        out_specs=pl.BlockSpec((tq, N_HG * HEAD_DIM), lambda g, b, i: (b * nt + i, g)),
        scratch_shapes=[pltpu.VMEM((nt, 8, tq), F32), pltpu.VMEM((2, FAR_KEYS, N_HG * tq), BF16)],
        compiler_params=_cparams(("arbitrary", "arbitrary", "arbitrary")),
        name="nsa_attn",
    )(qn, kc, vc, ks, vs, kw, vw, z, pat, bw, msel_j)


def _merge_cross_kernel(x_ref, yr_ref, yn_ref, zm_ref, wr_ref, wn_ref, wm_ref,
                        g_ref, wq_ref, qg_ref, k_ref, v_ref, wo_ref, o_ref):
    d = x_ref.shape[1]
    dh = d // CA_HEADS
    zm = zm_ref[...].astype(F32)
    br = _dot(yr_ref[...], wr_ref[...])
    bn = _dot(yn_ref[...], wn_ref[...])
    merged = _sigmoid(zm[:, 0:d]) * br + _sigmoid(zm[:, d:2 * d]) * bn
    h1 = x_ref[...] + _dot(merged.astype(BF16), wm_ref[...])

    xn = (_rms(h1, NORM_EPS) * g_ref[...]).astype(BF16)
    qf = _dot(xn, wq_ref[...])
    k = k_ref[0]
    v = v_ref[0]
    outs = []
    for h in range(CA_HEADS):
        sl = slice(h * dh, (h + 1) * dh)
        qh = (_rms(qf[:, sl], QK_EPS) * qg_ref[...] * (dh ** -0.5)).astype(BF16)
        s = _dot_nt(qh, k[:, sl])
        s = s - jnp.max(s, axis=-1, keepdims=True)
        e = jnp.exp(s)
        p = e / jnp.sum(e, axis=-1, keepdims=True)
        outs.append(_dot(p.astype(BF16), v[:, sl]))
    o = jnp.concatenate(outs, axis=1).astype(BF16)
    o_ref[...] = h1 + _dot(o, wo_ref[...])


def _merge_cross(x, yr, yn, z, wr, wn, wm, g, wq, qg, kn, vv, wo, *, batch, seq, tm):
    m, d = x.shape
    nt = seq // tm
    nm = kn.shape[1]
    row = lambda c: pl.BlockSpec((tm, c), lambda b, i: (b * nt + i, 0))
    full = lambda a: pl.BlockSpec(a.shape, lambda b, i: (0, 0))
    once = lambda a: pl.BlockSpec(a.shape, lambda b, i: (0, 0), pipeline_mode=pl.Buffered(1))
    kv_spec = pl.BlockSpec((1, nm, d), lambda b, i: (b, 0, 0))
    return pl.pallas_call(
        _merge_cross_kernel,
        out_shape=jax.ShapeDtypeStruct((m, d), F32),
        grid=(batch, nt),
        in_specs=[row(d), row(d), row(d),
                  pl.BlockSpec((tm, 2 * d), lambda b, i: (b * nt + i, C_M // (2 * d))),
                  once(wr), once(wn), once(wm), full(g), once(wq), full(qg), kv_spec, kv_spec, once(wo)],
        out_specs=row(d),
        compiler_params=_cparams(("arbitrary", "arbitrary")),
        name="merge_cross",
    )(x, yr, yn, z, wr, wn, wm, g, wq, qg, kn, vv, wo)


def _ca_kv_kernel(mem_ref, g_ref, w_ref, kg_ref, k_ref, v_ref):
    d = mem_ref.shape[2]
    dh = d // CA_HEADS
    mn = (_rms(mem_ref[0], NORM_EPS) * g_ref[...]).astype(BF16)
    kv = _dot(mn, w_ref[...])
    ks = [(_rms(kv[:, h * dh:(h + 1) * dh], QK_EPS) * kg_ref[...]) for h in range(CA_HEADS)]
    k_ref[0] = jnp.concatenate(ks, axis=1).astype(k_ref.dtype)
    v_ref[0] = kv[:, d:2 * d].astype(v_ref.dtype)


def _ca_kv(mem, g, wkv, kg):
    batch, nm, d = mem.shape
    o_shape = jax.ShapeDtypeStruct((batch, nm, d), BF16)
    o_spec = pl.BlockSpec((1, nm, d), lambda b: (b, 0, 0))
    full = lambda a: pl.BlockSpec(a.shape, lambda b: (0, 0))
    return pl.pallas_call(
        _ca_kv_kernel,
        out_shape=(o_shape, o_shape),
        grid=(batch,),
        in_specs=[o_spec, full(g), full(wkv), full(kg)],
        out_specs=(o_spec, o_spec),
        compiler_params=_cparams(("arbitrary",)),
        name="ca_kv",
    )(mem, g, wkv, kg)


def _ffn_kernel(x_ref, g_ref, wup_ref, cw_ref, cb_ref, wd_ref, o_ref, act_ref, carry_ref,
                *, tiles_per_seq, tn):
    i = pl.program_id(0)
    tm = x_ref.shape[0]
    dff = wd_ref.shape[0]

    @pl.when(i % tiles_per_seq == 0)
    def _():
        carry_ref[...] = jnp.zeros_like(carry_ref)

    x = x_ref[...]
    xn = (_rms(x, NORM_EPS) * g_ref[...]).astype(BF16)
    rowi = lax.broadcasted_iota(jnp.int32, (tm, tn), 0)
    for j in range(dff // tn):
        cs = slice(j * tn, (j + 1) * tn)
        a = _dot(xn, wup_ref[:, cs])
        b = _dot(xn, wup_ref[:, dff + j * tn:dff + (j + 1) * tn])
        car = carry_ref[:, cs]
        p1 = jnp.where(rowi == 0, car[7:8, :], pltpu.roll(a, 1, axis=0))
        p2 = jnp.where(rowi == 0, car[6:7, :], jnp.where(rowi == 1, car[7:8, :], pltpu.roll(a, 2, axis=0)))
        carry_ref[:, cs] = a[tm - 8:tm, :]
        conv = cw_ref[0:1, cs] * p2 + cw_ref[1:2, cs] * p1 + cw_ref[2:3, cs] * a + cb_ref[:, cs]
        act_ref[:, cs] = (conv * _sigmoid(conv) * b).astype(BF16)
    o_ref[...] = x + _dot(act_ref[...], wd_ref[...])


def _ffn(h2, g, wup, cw, cb, wd, *, seq, tm, tn):
    m, d = h2.shape
    dff = wd.shape[0]
    cw8 = jnp.zeros((8, dff), F32).at[0:CONV_W, :].set(cw)
    full = lambda a: pl.BlockSpec(a.shape, lambda i: (0, 0))
    once = lambda a: pl.BlockSpec(a.shape, lambda i: (0, 0), pipeline_mode=pl.Buffered(1))
    return pl.pallas_call(
        functools.partial(_ffn_kernel, tiles_per_seq=seq // tm, tn=tn),
        out_shape=jax.ShapeDtypeStruct((m, d), F32),
        grid=(m // tm,),
        in_specs=[pl.BlockSpec((tm, d), lambda i: (i, 0)), full(g), once(wup), full(cw8),
                  pl.BlockSpec((1, dff), lambda i: (0, 0)), once(wd)],
        out_specs=pl.BlockSpec((tm, d), lambda i: (i, 0)),
        scratch_shapes=[pltpu.VMEM((tm, dff), BF16), pltpu.VMEM((8, dff), F32)],
        compiler_params=_cparams(("arbitrary",)),
        name="ffn",
    )(h2, g, wup, cw8, cb.reshape(1, dff), wd)


def _pack_perm(width):
    o_zw = 3 * width
    o_za = o_zw + R_LORA_W
    o_zg = o_za + R_LORA_A
    o_q = o_zg + R_LORA_G
    nkv = N_KV * HEAD_DIM
    o_kc = o_q + N_HEADS * HEAD_DIM
    o_gate = o_kc + 6 * nkv
    o_m = o_gate + 3 * N_HEADS
    perm = np.full((Z_COLS,), -1, np.int64)
    perm[C_RKV:C_RKV + 3 * width] = np.arange(3 * width)
    perm[C_Q:C_Q + N_HEADS * HEAD_DIM] = o_q + np.arange(N_HEADS * HEAD_DIM)
    perm[C_KC:C_KC + 6 * nkv] = o_kc + np.arange(6 * nkv)
    perm[C_ZG:C_ZG + R_LORA_G] = o_zg + np.arange(R_LORA_G)
    perm[C_ZWA:C_ZWA + R_LORA_W + R_LORA_A] = o_zw + np.arange(R_LORA_W + R_LORA_A)
    for g in range(N_KV):
        for c in range(3):
            for h in range(N_HG):
                perm[C_GATE + 16 * g + 4 * c + h] = o_gate + (g * N_HG + h) * 3 + c
    perm[C_M:C_M + 2 * width] = o_m + np.arange(2 * width)
    return perm, o_m + 2 * width


def _pad_rows(w, rows, offset=0):
    out = jnp.zeros((rows, w.shape[1]), w.dtype)
    return out.at[offset:offset + w.shape[0]].set(w)


def kernel(x, mem, rel_bias, norm_mix, w_in, rwkv_mu, rwkv_w0, rwkv_w2, rwkv_a0, rwkv_a2, rwkv_g2,
           rwkv_kk, rwkv_ka, rwkv_rk, rwkv_lnx_w, rwkv_lnx_b, nsa_q_gain, nsa_k_gain, cmp_pe_k, cmp_pe_v,
           cmp_w1_k, cmp_w2_k, cmp_w1_v, cmp_w2_v, w_branch_rwkv, w_branch_nsa, w_mix_out,
           norm_cross, norm_mem, ca_wq, ca_wkv, ca_q_gain, ca_k_gain, ca_wo,
           norm_ffn, ffn_up, ffn_conv, ffn_conv_b, ffn_down):
    batch, seq, d = x.shape
    depth = w_in.shape[0]
    width = d
    assert d == N_HEADS * HEAD_DIM and 2 * (Z_COLS - C_M) == 4 * d
    perm, in_cols = _pack_perm(width)
    assert w_in.shape[2] == in_cols
    perm_j = jnp.asarray(np.maximum(perm, 0), jnp.int32)
    keep = jnp.asarray(perm >= 0)

    tq = 256
    assert seq % 256 == 0 and tq > REL_MAX_DIST - 1 and WINDOW % tq == 0
    ncp = -(-(seq // CMP_STRIDE) // LANE) * LANE
    n_cmp = (seq - CMP_BLOCK) // CMP_STRIDE + 1
    tm = min(1024, seq)
    row = lambda v: v.reshape(1, -1).astype(F32)
    nkv = N_KV * HEAD_DIM

    h = x.reshape(batch * seq, d)
    for l in range(depth):
        w_in_p = jnp.where(keep[None, :], jnp.take(w_in[l], perm_j, axis=1), 0.0).astype(BF16)
        mu = rwkv_mu[l]
        mu_r = row(mu[0:3 * width])
        mu_wa = row(mu[3 * width:3 * width + R_LORA_W + R_LORA_A])
        mu_g = row(jnp.zeros((256,), F32).at[0:R_LORA_G].set(mu[3 * width + R_LORA_W + R_LORA_A:]))
        w2p = _pad_rows(rwkv_w2[l], LANE, 0).astype(BF16)
        a2p = _pad_rows(rwkv_a2[l], LANE, R_LORA_W).astype(BF16)
        g2p = _pad_rows(rwkv_g2[l], 256, 0).astype(BF16)

        z = _norm_matmul(h, row(norm_mix[l]), w_in_p, tm=tm, tn=4096, out_dtype=BF16, name="in_proj")

        y_r = _rwkv(z, mu_r, mu_g, mu_wa, row(rwkv_w0[l]), w2p, row(rwkv_a0[l]), a2p, g2p,
                    row(rwkv_kk[l]), row(rwkv_ka[l]), row(rwkv_rk[l]), row(rwkv_lnx_w[l]),
                    row(rwkv_lnx_b[l]), batch=batch, seq=seq, width=width)

        qn, ks, vs, kw, vw = _nsa_prep(
            z, row(jnp.tile(nsa_q_gain[l], N_HEADS)), row(jnp.tile(nsa_k_gain[l, 1], N_KV)),
            row(jnp.tile(nsa_k_gain[l, 2], N_KV)), batch=batch, seq=seq, tq=tq)
        z3 = z.reshape(batch, seq // CMP_STRIDE, CMP_STRIDE, Z_COLS)

        def groups(c0):
            t = z3[:, :, :, c0:c0 + nkv].reshape(batch, seq // CMP_STRIDE, CMP_STRIDE, N_KV, HEAD_DIM)
            return t.transpose(0, 3, 1, 2, 4).reshape(batch, N_KV, seq // CMP_STRIDE, CMP_STRIDE * HEAD_DIM)

        kc, vc = _nsa_cmp(groups(C_KC), groups(C_KC + nkv), cmp_pe_k[l].reshape(1, -1), cmp_pe_v[l].reshape(1, -1),
                          cmp_w1_k[l].astype(BF16), cmp_w2_k[l].astype(BF16), cmp_w1_v[l].astype(BF16),
                          cmp_w2_v[l].astype(BF16), row(nsa_k_gain[l, 0]), ncp=ncp, n_cmp=n_cmp)
        bw, pat = _nsa_bias(rel_bias.astype(F32), tq=tq, ncp=ncp)
        y_n = _nsa_attn(qn, kc, vc, ks, vs, kw, vw, z, pat, bw,
                        batch=batch, seq=seq, tq=tq, ncp=ncp)

        kn, vv = _ca_kv(mem, row(norm_mem[l]), ca_wkv[l].astype(BF16), row(ca_k_gain[l]))
        h2 = _merge_cross(h, y_r, y_n, z, w_branch_rwkv[l].astype(BF16), w_branch_nsa[l].astype(BF16),
                          w_mix_out[l].astype(BF16), row(norm_cross[l]), ca_wq[l].astype(BF16),
                          row(ca_q_gain[l]), kn, vv, ca_wo[l].astype(BF16),
                          batch=batch, seq=seq, tm=tm)

        h = _ffn(h2, row(norm_ffn[l]), ffn_up[l].astype(BF16), ffn_conv[l], ffn_conv_b[l],
                 ffn_down[l].astype(BF16), seq=seq, tm=tm, tn=256)
    return h.reshape(batch, seq, d)
```
